```python
import math
import jax
import jax.numpy as jnp
from jax import lax
import numpy as np

D_MODEL = 1024
BATCH = 2
SEQ = 16384
DEPTH = 2

F32 = jnp.float32
CTX_LEN = 256
GRID_W = 64
CHUNK = 64
NORM_EPS = 1e-6
HEAD_NORM_EPS = 1e-5

N_RET_HEADS = 4
RET_DK = 64
RET_DV = 128
RET_QK = N_RET_HEADS * RET_DK
RET_V = N_RET_HEADS * RET_DV
RET_LOG_DECAY_FWD = tuple(math.log1p(-2.0 ** (-5.0 - h)) for h in range(N_RET_HEADS))
RET_LOG_DECAY_BWD = tuple(math.log1p(-2.0 ** (-5.5 - h)) for h in range(N_RET_HEADS))

N_GLA_HEADS = 4
GLA_DK = 64
GLA_DV = 128
GLA_QK = N_GLA_HEADS * GLA_DK
GLA_V = N_GLA_HEADS * GLA_DV
GLA_LOWRANK = 16
GLA_GATE_NORM = 16.0

D_HY = 512
HY_BANDS = 16
HY_EMB = 1 + 2 * HY_BANDS
HY_FILTER_WIDTH = 64
HY_INNER = 2
HY_TARGET = 1e-2
HY_FAST_PCT = 0.3
HY_SLOW_PCT = 1.5
HY_MIN_DECAY = math.log(HY_TARGET) / HY_SLOW_PCT
HY_MAX_DECAY = math.log(HY_TARGET) / HY_FAST_PCT
HY_SHIFT = 0.05
HY_FILTER_INIT = 0.05

N_BRANCH = 3
BRANCH_W = 512
D_FF = 2816

IN_SPLITS = (RET_QK, RET_QK, RET_V, RET_V, GLA_QK, GLA_QK, GLA_V, GLA_V, 2 * GLA_LOWRANK, 3 * D_HY, N_BRANCH * D_MODEL)
D_IN = sum(IN_SPLITS)
IN_OFFSETS = tuple(sum(IN_SPLITS[:i + 1]) for i in range(len(IN_SPLITS) - 1))

kernel_name = 'hybrid_ret_gla_hyena_dit'


def rms_norm(x, g):
    xf = x.astype(F32)
    y = xf * lax.rsqrt(jnp.mean(xf * xf, axis=-1, keepdims=True) + NORM_EPS)
    return (y * g.astype(F32)).astype(x.dtype)


def modulate(x, g, shift, scale):
    return rms_norm(x, g) * (1.0 + scale) + shift


def head_norm(o):
    mu = jnp.mean(o, axis=-1, keepdims=True)
    var = jnp.mean(jnp.square(o - mu), axis=-1, keepdims=True)
    return (o - mu) * lax.rsqrt(var + HEAD_NORM_EPS)


def to_heads(z, n):
    b, t, w = z.shape
    return z.reshape(b, t, n, w // n).transpose(0, 2, 1, 3)


def from_heads(z):
    b, n, t, d = z.shape
    return z.transpose(0, 2, 1, 3).reshape(b, t, n * d)


def dwconv1d(z, w, bias):
    L = z.shape[1]
    zp = jnp.pad(z, ((0, 0), (1, 1), (0, 0)))
    return bias + zp[:, :L] * w[0] + zp[:, 1:L + 1] * w[1] + zp[:, 2:] * w[2]


def dwconv2d(z, w, bias):
    R, W = z.shape[1], z.shape[2]
    zp = jnp.pad(z, ((0, 0), (1, 1), (1, 1), (0, 0)))
    out = bias
    for i in range(3):
        for j in range(3):
            out = out + zp[:, i:i + R, j:j + W] * w[i, j]
    return out


def chunked_scan(q, k, v, log_a, h0, strict, need_out=True):
    q, k, v, log_a = (z.astype(F32) for z in (q, k, v, log_a))
    b_, h_, t, _ = q.shape
    dv = v.shape[-1]
    n = t // CHUNK
    split = lambda z: z.reshape(b_, h_, n, CHUNK, z.shape[-1])
    qc, kc, vc, la = split(q), split(k), split(v), split(log_a)
    cum = jnp.cumsum(la, axis=3)
    cum_last = cum[:, :, :, -1:]
    k_out = kc * jnp.exp(cum_last - cum)
    a_chunk = jnp.exp(cum_last[:, :, :, 0])
    lead = lambda z: jnp.moveaxis(z, 2, 0)
    xs = (lead(k_out), lead(vc), lead(a_chunk))
    if need_out:
        q_in = qc * jnp.exp(cum)
        idx = jnp.arange(CHUNK)
        mask = idx[:, None] > idx[None, :] if strict else idx[:, None] >= idx[None, :]
        if la.shape[-1] == 1:
            cs = cum[..., 0]
            diff = jnp.where(mask, cs[..., :, None] - cs[..., None, :], 0.0)
            decay = jnp.where(mask, jnp.exp(diff), 0.0)
            scores = jnp.einsum('bhncd,bhnsd->bhncs', qc, kc) * decay
        else:
            scores = jnp.einsum('bhncd,bhnsd->bhncs', q_in, kc * jnp.exp(-cum))
            scores = jnp.where(mask, scores, 0.0)
        o_intra = jnp.einsum('bhncs,bhnse->bhnce', scores, vc)
        xs = xs + (lead(q_in),)

    def step(h, xs_i):
        ki, vi, ai = xs_i[:3]
        o = jnp.einsum('bhcd,bhde->bhce', xs_i[3], h) if need_out else None
        h_new = ai[..., None] * h + jnp.einsum('bhcd,bhce->bhde', ki, vi)
        return h_new, o

    h_t, o_inter = lax.scan(step, h0, xs)
    if not need_out:
        return None, h_t
    o = o_intra + jnp.moveaxis(o_inter, 0, 2)
    return o.reshape(b_, h_, t, dv), h_t


def bidir_recurrence(ctx_qkv, lat_qkv, ctx_log_a, lat_log_a, need_ctx_out):
    bsz, nh, _, dk = ctx_qkv[0].shape
    dv = ctx_qkv[2].shape[-1]
    o_ctx, o_lat = None, None
    for d in range(2):
        rev = d == 1
        f = (lambda z: jnp.flip(z, axis=2)) if rev else (lambda z: z)
        h0 = jnp.zeros((bsz, nh, dk, dv), F32)
        oc, hc = chunked_scan(*[f(z) for z in ctx_qkv], f(ctx_log_a[d]), h0, rev, need_ctx_out)
        ol, _ = chunked_scan(*[f(z) for z in lat_qkv], f(lat_log_a[d]), hc, rev)
        o_lat = f(ol) if o_lat is None else o_lat + f(ol)
        if need_ctx_out:
            o_ctx = f(oc) if o_ctx is None else o_ctx + f(oc)
    return o_ctx, o_lat


def retention_branch(p_ctx, p_lat, need_ctx_out):
    def prep(q, k, v):
        return (to_heads(q, N_RET_HEADS).astype(F32),
                to_heads(k, N_RET_HEADS).astype(F32) * RET_DK ** -0.5,
                to_heads(v, N_RET_HEADS).astype(F32))

    def log_decay(q):
        b, h, t, _ = q.shape
        return tuple(jnp.broadcast_to(jnp.asarray(g, F32)[None, :, None, None], (b, h, t, 1))
                     for g in (RET_LOG_DECAY_FWD, RET_LOG_DECAY_BWD))

    qkv_c, qkv_l = prep(*p_ctx[:3]), prep(*p_lat[:3])
    o_c, o_l = bidir_recurrence(qkv_c, qkv_l, log_decay(qkv_c[0]), log_decay(qkv_l[0]), need_ctx_out)
    out = lambda o, g: from_heads(head_norm(o)) * jax.nn.silu(g.astype(F32))
    return (out(o_c, p_ctx[3]) if need_ctx_out else None), out(o_l, p_lat[3])


def gla_branch(p_ctx, p_lat, wa2, ba, need_ctx_out):
    def prep(q, k, v, r, lr):
        qkv = (to_heads(q, N_GLA_HEADS).astype(F32) * GLA_DK ** -0.5,
               to_heads(k, N_GLA_HEADS).astype(F32),
               to_heads(v, N_GLA_HEADS).astype(F32))
        lr_dirs = jnp.split(lr.astype(F32), 2, axis=-1)
        log_a = tuple(to_heads(jax.nn.log_sigmoid(lr_dirs[d] @ wa2[d] + ba[d]) / GLA_GATE_NORM, N_GLA_HEADS)
                      for d in range(2))
        return qkv, log_a

    qkv_c, la_c = prep(*p_ctx)
    qkv_l, la_l = prep(*p_lat)
    o_c, o_l = bidir_recurrence(qkv_c, qkv_l, la_c, la_l, need_ctx_out)
    out = lambda o, r: from_heads(head_norm(o)) * jax.nn.silu(r.astype(F32))
    return (out(o_c, p_ctx[3]) if need_ctx_out else None), out(o_l, p_lat[3])


def hyena_filters(length, w1, b1, w2, b2, w3, freq):
    t = jnp.linspace(0.0, 1.0, length, dtype=F32)[:, None]
    ang = (2.0 * math.pi / length) * jnp.arange(length, dtype=F32)[:, None] \
        * jnp.linspace(1e-4, HY_BANDS - 1.0, HY_BANDS, dtype=F32)[None, :]
    z = jnp.concatenate([t, jnp.cos(ang), -jnp.sin(ang)], axis=-1)
    hdn = jnp.sin(freq * (z @ w1 + b1))
    for i in range(HY_INNER):
        hdn = jnp.sin(freq * (hdn @ w2[i] + b2[i]))
    h = (hdn @ w3).reshape(length, 2, D_HY)
    deltas = jnp.abs(jnp.linspace(HY_MIN_DECAY, HY_MAX_DECAY, D_HY, dtype=F32))
    window = jnp.exp(-t * deltas) + HY_SHIFT
    h = h * window[:, None, :]
    return h[:, 0], h[:, 1]


def fft_long_conv(z, h_fwd, h_bwd):
    L = z.shape[1]
    k = jnp.concatenate([h_fwd, jnp.zeros_like(h_fwd[:1]), jnp.flip(h_bwd[1:], axis=0)], axis=0)
    zf = jnp.fft.rfft(z, n=2 * L, axis=1)
    kf = jnp.fft.rfft(k, axis=0)
    return jnp.fft.irfft(zf * kf[None], n=2 * L, axis=1)[:, :L]


def hyena_branch(p, h_fwd, h_bwd, short_w, short_b, bias):
    u = dwconv1d(p, short_w, short_b)
    x0, x1, v = jnp.split(u, 3, axis=-1)
    z = (x1 * v).astype(F32)
    y = fft_long_conv(z, h_fwd, h_bwd) + z * bias
    return x0.astype(F32) * y


def merge_branches(branches, gate_logits, w_branch, w_out):
    gates = jnp.split(jax.nn.sigmoid(gate_logits.astype(F32)), N_BRANCH, axis=-1)
    mixed = gates[0] * (branches[0] @ w_branch[0])
    for g in range(1, N_BRANCH):
        mixed = mixed + gates[g] * (branches[g] @ w_branch[g])
    return mixed @ w_out


def token_mixer(h_ctx, h_lat, lp, need_ctx_out):
    p_ctx = jnp.split(h_ctx @ lp['w_in'], IN_OFFSETS, axis=-1)
    p_lat = jnp.split(h_lat @ lp['w_in'], IN_OFFSETS, axis=-1)
    ret_c, ret_l = retention_branch(p_ctx[0:4], p_lat[0:4], need_ctx_out)
    gla_c, gla_l = gla_branch(p_ctx[4:9], p_lat[4:9], lp['gla_wa2'], lp['gla_ba'], need_ctx_out)
    filt = (lp['hy_w1'], lp['hy_b1'], lp['hy_w2'], lp['hy_b2'], lp['hy_w3'], lp['hy_freq'])
    hy_args = (lp['hy_short_w'], lp['hy_short_b'], lp['hy_bias'])
    hy_l = hyena_branch(p_lat[9], *hyena_filters(h_lat.shape[1], *filt), *hy_args)
    mix_l = merge_branches((ret_l, gla_l, hy_l), p_lat[10], lp['w_branch'], lp['w_out'])
    if not need_ctx_out:
        return None, mix_l
    hy_c = hyena_branch(p_ctx[9], *hyena_filters(h_ctx.shape[1], *filt), *hy_args)
    mix_c = merge_branches((ret_c, gla_c, hy_c), p_ctx[10], lp['w_branch'], lp['w_out'])
    return mix_c, mix_l


def conv_glu(h, rows, cols, lp):
    b, t, _ = h.shape
    a, v = jnp.split(h @ lp['w_up'], 2, axis=-1)
    a = dwconv2d(a.reshape(b, rows, cols, D_FF), lp['ffn_conv_w'], lp['ffn_conv_b']).reshape(b, t, D_FF)
    return (jax.nn.gelu(a, approximate=False) * v) @ lp['w_down']


def setup_inputs(seed: int = 0) -> dict:
    key = jax.random.key(seed)
    ks = iter(jax.random.split(key, 32))
    nrm = lambda shape, s: s * jax.random.normal(next(ks), shape, F32)
    L = DEPTH
    return {
        'x': nrm((BATCH, SEQ, D_MODEL), 1.0),
        'c': nrm((BATCH, D_MODEL), 1.0),
        'ctx': nrm((BATCH, CTX_LEN, D_MODEL), 1.0),
        'c_ctx': nrm((D_MODEL,), 1.0),
        'ada_w': nrm((L, D_MODEL, 6 * D_MODEL), 0.5 * D_MODEL ** -0.5),
        'ada_b': nrm((L, 6 * D_MODEL), 0.02),
        'norm1_g': 1.0 + nrm((L, D_MODEL), 0.02),
        'w_in': nrm((L, D_MODEL, D_IN), D_MODEL ** -0.5),
        'gla_wa2': nrm((L, 2, GLA_LOWRANK, GLA_QK), GLA_LOWRANK ** -0.5),
        'gla_ba': nrm((L, 2, GLA_QK), 0.1),
        'hy_short_w': nrm((L, 3, 3 * D_HY), 3 ** -0.5),
        'hy_short_b': nrm((L, 3 * D_HY), 0.02),
        'hy_w1': nrm((L, HY_EMB, HY_FILTER_WIDTH), HY_EMB ** -0.5),
        'hy_b1': nrm((L, HY_FILTER_WIDTH), 0.1),
        'hy_w2': nrm((L, HY_INNER, HY_FILTER_WIDTH, HY_FILTER_WIDTH), HY_FILTER_WIDTH ** -0.5),
        'hy_b2': nrm((L, HY_INNER, HY_FILTER_WIDTH), 0.1),
        'hy_w3': nrm((L, HY_FILTER_WIDTH, 2 * D_HY), HY_FILTER_INIT * HY_FILTER_WIDTH ** -0.5),
        'hy_freq': 1.0 + nrm((L, HY_FILTER_WIDTH), 0.1),
        'hy_bias': nrm((L, D_HY), 1.0),
        'w_branch': nrm((L, N_BRANCH, BRANCH_W, D_MODEL), BRANCH_W ** -0.5),
        'w_out': nrm((L, D_MODEL, D_MODEL), D_MODEL ** -0.5),
        'norm2_g': 1.0 + nrm((L, D_MODEL), 0.02),
        'w_up': nrm((L, D_MODEL, 2 * D_FF), D_MODEL ** -0.5),
        'ffn_conv_w': nrm((L, 3, 3, D_FF), 1.0 / 3.0),
        'ffn_conv_b': nrm((L, D_FF), 0.02),
        'w_down': nrm((L, D_FF, D_MODEL), D_FF ** -0.5),
        'final_g': 1.0 + nrm((D_MODEL,), 0.02),
    }


def reference(x, c, ctx, c_ctx, ada_w, ada_b, norm1_g, w_in, gla_wa2, gla_ba, hy_short_w, hy_short_b,
              hy_w1, hy_b1, hy_w2, hy_b2, hy_w3, hy_freq, hy_bias, w_branch, w_out, norm2_g, w_up,
              ffn_conv_w, ffn_conv_b, w_down, final_g):
    rows = x.shape[1] // GRID_W
    x_lat, x_ctx = x, ctx
    s_lat = jax.nn.silu(c)
    s_ctx = jax.nn.silu(c_ctx)
    for l in range(DEPTH):
        last = l == DEPTH - 1
        lp = dict(w_in=w_in[l], gla_wa2=gla_wa2[l], gla_ba=gla_ba[l], hy_short_w=hy_short_w[l],
                  hy_short_b=hy_short_b[l], hy_w1=hy_w1[l], hy_b1=hy_b1[l], hy_w2=hy_w2[l], hy_b2=hy_b2[l],
                  hy_w3=hy_w3[l], hy_freq=hy_freq[l], hy_bias=hy_bias[l], w_branch=w_branch[l],
                  w_out=w_out[l], w_up=w_up[l], ffn_conv_w=ffn_conv_w[l], ffn_conv_b=ffn_conv_b[l],
                  w_down=w_down[l])
        sh1, sc1, g1, sh2, sc2, g2 = jnp.split((s_lat @ ada_w[l] + ada_b[l])[:, None, :], 6, axis=-1)
        csh1, csc1, cg1, csh2, csc2, cg2 = jnp.split(s_ctx @ ada_w[l] + ada_b[l], 6, axis=-1)
        h_lat = modulate(x_lat, norm1_g[l], sh1, sc1)
        h_ctx = modulate(x_ctx, norm1_g[l], csh1, csc1)
        mix_c, mix_l = token_mixer(h_ctx, h_lat, lp, not last)
        x_lat = x_lat + g1 * mix_l
        x_lat = x_lat + g2 * conv_glu(modulate(x_lat, norm2_g[l], sh2, sc2), rows, GRID_W, lp)
        if not last:
            x_ctx = x_ctx + cg1 * mix_c
            x_ctx = x_ctx + cg2 * conv_glu(modulate(x_ctx, norm2_g[l], csh2, csc2), 1, x_ctx.shape[1], lp)
    return rms_norm(x_lat, final_g)
```

```python
import functools
import math

import numpy as np
import jax
import jax.numpy as jnp
from jax import lax
from jax.experimental import pallas as pl
from jax.experimental.pallas import tpu as pltpu

F32 = jnp.float32
BF16 = jnp.bfloat16
HIGHEST = lax.Precision.HIGHEST

D_MODEL = 1024
CTX_LEN = 256
GRID_W = 64
NORM_EPS = 1e-6
HEAD_NORM_EPS = 1e-5

N_HEADS = 4
DK = 64
DV = 128
QK_W = N_HEADS * DK
V_W = N_HEADS * DV
RET_LOG_DECAY = (tuple(math.log1p(-2.0 ** (-5.0 - h)) for h in range(N_HEADS)),
                 tuple(math.log1p(-2.0 ** (-5.5 - h)) for h in range(N_HEADS)))
GLA_LOWRANK = 16
GLA_GATE_NORM = 16.0
GLA_CHUNK = 64

D_HY = 512
HY_BANDS = 16
HY_EMB = 1 + 2 * HY_BANDS
HY_EMB_PAD = 40
HY_FILTER_WIDTH = 64
HY_INNER = 2
HY_MIN_DECAY = math.log(1e-2) / 1.5
HY_MAX_DECAY = math.log(1e-2) / 0.3
HY_SHIFT = 0.05

D_FF = 2816
D_IN_MAIN = 7680
LR_OFF = 3072

TT = 256
LANES = 128
FFT_N2 = 128
FFT_CB = 16
VMEM_LIMIT = 56 * 1024 * 1024


def _dot(a, b, precision=None):
    return jnp.dot(a, b, preferred_element_type=F32, precision=precision)


def _dot_nt(a, b):
    return lax.dot_general(a, b, (((1,), (1,)), ((), ())), preferred_element_type=F32)


def _dot_tn(a, b):
    return lax.dot_general(a, b, (((0,), (0,)), ((), ())), preferred_element_type=F32)


def _params(n_axes):
    return pltpu.CompilerParams(dimension_semantics=("arbitrary",) * n_axes,
                                vmem_limit_bytes=VMEM_LIMIT)


def _const_spec(shape):
    n = len(shape)
    return pl.BlockSpec(shape, lambda *_: (0,) * n, pipeline_mode=pl.Buffered(1))


def _modulate(x, g, shift, scale):
    y = x * lax.rsqrt(jnp.mean(x * x, axis=-1, keepdims=True) + NORM_EPS) * g
    return y * (1.0 + scale) + shift


def _silu(x):
    return x * jax.nn.sigmoid(x)


def _ada_kernel(c_ref, w_ref, b_ref, o_ref):
    o_ref[0] = _dot(_silu(c_ref[...]), w_ref[0], HIGHEST) + b_ref[0]


def _ada_call(cvec, ada_w, ada_b):
    depth, d, n = ada_w.shape
    nb = 1536
    return pl.pallas_call(
        _ada_kernel,
        grid=(depth, n // nb),
        in_specs=[pl.BlockSpec((8, d), lambda l, j: (0, 0)),
                  pl.BlockSpec((1, d, nb), lambda l, j: (l, 0, j)),
                  pl.BlockSpec((1, 1, nb), lambda l, j: (l, 0, j))],
        out_specs=pl.BlockSpec((1, 8, nb), lambda l, j: (l, 0, j)),
        out_shape=jax.ShapeDtypeStruct((depth, 8, n), F32),
        compiler_params=_params(2),
        name="ada_ln",
    )(cvec, ada_w, ada_b.reshape(depth, 1, n))


def _log_sigmoid(x):
    return jnp.minimum(x, 0.0) - jnp.log1p(jnp.exp(-jnp.abs(x)))


def _inproj_kernel(x_ref, mod_ref, g_ref, wm_ref, wlr_ref, wa_ref, ba_ref,
                   ret_ref, gla_ref, hy_ref, mg_ref, lga_ref):
    m = mod_ref[0, 0]
    h = _modulate(x_ref[0], g_ref[...], m[0:1], m[1:2])
    hb = h.astype(BF16)
    ret_ref[0] = _dot(hb, wm_ref[:, 0:1536]).astype(BF16)
    gla_ref[0] = _dot(hb, wm_ref[:, 1536:3072]).astype(BF16)
    hy_ref[0] = _dot(hb, wm_ref[:, 3072:4608]).astype(BF16)
    mg_ref[0] = _dot(hb, wm_ref[:, 4608:7680]).astype(BF16)
    for d in range(2):
        lr = _dot(h, wlr_ref[d], HIGHEST)
        gate = _dot(lr, wa_ref[d], HIGHEST) + ba_ref[d]
        lga_ref[0, :, d * QK_W:(d + 1) * QK_W] = _log_sigmoid(gate) * (1.0 / GLA_GATE_NORM)


def _inproj_call(x_all, modsel, g1, w_main, w_lr, wa2, ba):
    b, s, d = x_all.shape
    nt = s // TT
    tile = lambda w: pl.BlockSpec((1, TT, w), lambda i, t: (i, t, 0))
    bf = lambda w: jax.ShapeDtypeStruct((b, s, w), BF16)
    return pl.pallas_call(
        _inproj_kernel,
        grid=(b, nt),
        in_specs=[tile(d),
                  pl.BlockSpec((1, 1, 6, d), lambda i, t: (i, jnp.minimum(t, 1), 0, 0)),
                  _const_spec((1, d)),
                  _const_spec((d, D_IN_MAIN)),
                  _const_spec((2, d, GLA_LOWRANK)),
                  _const_spec((2, GLA_LOWRANK, QK_W)),
                  _const_spec((2, 1, QK_W))],
        out_specs=[tile(1536), tile(1536), tile(1536), tile(3072), tile(2 * QK_W)],
        out_shape=[bf(1536), bf(1536), bf(1536), bf(3072),
                   jax.ShapeDtypeStruct((b, s, 2 * QK_W), F32)],
        compiler_params=_params(2),
        name="in_proj",
    )(x_all, modsel, g1, w_main, w_lr, wa2, ba)


@functools.lru_cache(maxsize=None)
def _ret_tables():
    i = np.arange(TT, dtype=np.float64)
    scale = DK ** -0.5
    dmat = np.zeros((2, N_HEADS, TT, TT))
    erow = np.zeros((2, N_HEADS, TT, DV))
    kw = np.zeros((2, N_HEADS, TT, DK))
    for d in range(2):
        for h in range(N_HEADS):
            lg = RET_LOG_DECAY[d][h]
            diff = i[:, None] - i[None, :]
            if d == 0:
                dmat[d, h] = np.where(diff >= 0, np.exp(np.maximum(diff, 0) * lg), 0.0) * scale
                erow[d, h] = np.exp((i + 1) * lg)[:, None]
                kw[d, h] = (np.exp((TT - 1 - i) * lg) * scale)[:, None]
            else:
                dmat[d, h] = np.where(diff < 0, np.exp(np.maximum(-diff, 0) * lg), 0.0) * scale
                erow[d, h] = np.exp((TT - i) * lg)[:, None]
                kw[d, h] = (np.exp(i * lg) * scale)[:, None]
    return dmat.astype(np.float32), erow.astype(np.float32), kw.astype(np.float32)


def _scan_kernel(retf_ref, glaf_ref, lgf_ref, retb_ref, glab_ref, lgb_ref,
                 dmat_ref, erow_ref, kw_ref, of_ref, ob_ref, h_ref):
    @pl.when(pl.program_id(1) == 0)
    def _():
        h_ref[...] = jnp.zeros_like(h_ref)

    c = GLA_CHUNK
    row = lax.broadcasted_iota(jnp.int32, (c, c), 0)
    col = lax.broadcasted_iota(jnp.int32, (c, c), 1)
    dirs = ((retf_ref, glaf_ref, lgf_ref, of_ref), (retb_ref, glab_ref, lgb_ref, ob_ref))
    for d, (ret_ref, gla_ref, lg_ref, o_ref) in enumerate(dirs):
        for hd in range(N_HEADS):
            q = ret_ref[0, :, hd * DK:(hd + 1) * DK]
            k = ret_ref[0, :, QK_W + hd * DK:QK_W + (hd + 1) * DK]
            v = ret_ref[0, :, 2 * QK_W + hd * DV:2 * QK_W + (hd + 1) * DV]
            hs = h_ref[d * 8 + hd]
            s = _dot_nt(q, k) * dmat_ref[d, hd]
            o = _dot(s.astype(BF16), v) + erow_ref[d, hd] * _dot(q, hs.astype(BF16))
            kd = (k.astype(F32) * kw_ref[d, hd]).astype(BF16)
            a_tile = math.exp(TT * RET_LOG_DECAY[d][hd])
            h_ref[d * 8 + hd] = a_tile * hs + _dot_tn(kd, v)
            o_ref[0, :, hd * DV:(hd + 1) * DV] = o
        if d == 0:
            tri = (row >= col).astype(F32)
            mask = row >= col
        else:
            tri = (row <= col).astype(F32)
            mask = row < col
        order = range(TT // c) if d == 0 else range(TT // c - 1, -1, -1)
        for sc in order:
            rows = slice(sc * c, (sc + 1) * c)
            cum = _dot(tri, lg_ref[0, rows, :], HIGHEST)
            last = cum[c - 1:c] if d == 0 else cum[0:1]
            q = gla_ref[0, rows, 0:QK_W].astype(F32)
            k = gla_ref[0, rows, QK_W:2 * QK_W].astype(F32)
            q_in = (q * jnp.exp(cum) * DK ** -0.5).astype(BF16)
            k_neg = (k * jnp.exp(-cum)).astype(BF16)
            k_out = (k * jnp.exp(last - cum)).astype(BF16)
            a_col = jnp.broadcast_to(jnp.exp(last), (DV, QK_W)).T
            for hd in range(N_HEADS):
                ks = slice(hd * DK, (hd + 1) * DK)
                v = gla_ref[0, rows, 2 * QK_W + hd * DV:2 * QK_W + (hd + 1) * DV]
                hs = h_ref[d * 8 + 4 + hd]
                s = jnp.where(mask, _dot_nt(q_in[:, ks], k_neg[:, ks]), 0.0)
                o = _dot(s.astype(BF16), v) + _dot(q_in[:, ks], hs.astype(BF16))
                h_ref[d * 8 + 4 + hd] = a_col[ks] * hs + _dot_tn(k_out[:, ks], v)
                o_ref[0, rows, V_W + hd * DV:V_W + (hd + 1) * DV] = o


def _scan_call(ret, gla, lga):
    b, s, _ = ret.shape
    nt = s // TT
    dmat, erow, kw = (jnp.asarray(z) for z in _ret_tables())
    fwd = lambda i, t: (i, t, 0)
    bwd = lambda i, t: (i, jnp.where(t == 0, 0, nt - t), 0)
    bwd_g = lambda i, t: (i, jnp.where(t == 0, 0, nt - t), 1)
    qkv = lambda f: pl.BlockSpec((1, TT, 2 * QK_W + V_W), f)
    return pl.pallas_call(
        _scan_kernel,
        grid=(b, nt),
        in_specs=[qkv(fwd), qkv(fwd), pl.BlockSpec((1, TT, QK_W), fwd),
                  qkv(bwd), qkv(bwd), pl.BlockSpec((1, TT, QK_W), bwd_g),
                  _const_spec(dmat.shape), _const_spec(erow.shape), _const_spec(kw.shape)],
        out_specs=[pl.BlockSpec((1, TT, 2 * V_W), fwd), pl.BlockSpec((1, TT, 2 * V_W), bwd)],
        out_shape=[jax.ShapeDtypeStruct((b, s, 2 * V_W), F32)] * 2,
        scratch_shapes=[pltpu.VMEM((16, DK, DV), F32)],
        compiler_params=_params(2),
        name="bidir_scan",
    )(ret, gla, lga, ret, gla, lga, dmat, erow, kw)


def _hy_prep_kernel(p_ref, prev_ref, next_ref, w_ref, b_ref, x0_ref, zl_ref, zc_ref, *, nt):
    t = pl.program_id(1)
    has_prev = t >= 2
    has_next = jnp.logical_and(t >= 1, t <= nt - 2)
    ridx = lax.broadcasted_iota(jnp.int32, (TT, LANES), 0)
    u = []
    for part in range(3):
        cols = []
        for j in range(D_HY // LANES):
            cs = slice(part * D_HY + j * LANES, part * D_HY + (j + 1) * LANES)
            p = p_ref[0, :, cs].astype(F32)
            pv = jnp.where(has_prev, prev_ref[0, 15:16, cs].astype(F32), 0.0)
            nx = jnp.where(has_next, next_ref[0, 0:1, cs].astype(F32), 0.0)
            up = jnp.where(ridx == 0, pv, pltpu.roll(p, 1, 0))
            dn = jnp.where(ridx == TT - 1, nx, pltpu.roll(p, TT - 1, 0))
            cols.append(b_ref[:, cs] + up * w_ref[0:1, cs] + p * w_ref[1:2, cs] + dn * w_ref[2:3, cs])
        u.append(cols)
    x0_ref[0] = jnp.concatenate(u[0], axis=1).astype(BF16)
    z = jnp.concatenate([a * c for a, c in zip(u[1], u[2])], axis=1)
    zt = z.T

    @pl.when(t == 0)
    def _():
        zc_ref[0] = zt

    @pl.when(t > 0)
    def _():
        zl_ref[0] = zt


def _hy_prep_call(hy, short_w, short_b):
    b, s, w = hy.shape
    nt = s // TT
    hb = 16
    nhb = s // hb
    return pl.pallas_call(
        functools.partial(_hy_prep_kernel, nt=nt),
        grid=(b, nt),
        in_specs=[pl.BlockSpec((1, TT, w), lambda i, t: (i, t, 0)),
                  pl.BlockSpec((1, hb, w), lambda i, t: (i, jnp.maximum(t * (TT // hb) - 1, 0), 0)),
                  pl.BlockSpec((1, hb, w), lambda i, t: (i, jnp.minimum((t + 1) * (TT // hb), nhb - 1), 0)),
                  _const_spec((3, w)), _const_spec((1, w))],
        out_specs=[pl.BlockSpec((1, TT, D_HY), lambda i, t: (i, t, 0)),
                   pl.BlockSpec((1, D_HY, TT), lambda i, t: (i, 0, jnp.maximum(t - 1, 0))),
                   pl.BlockSpec((1, D_HY, CTX_LEN), lambda i, t: (i, 0, 0))],
        out_shape=[jax.ShapeDtypeStruct((b, s, D_HY), BF16),
                   jax.ShapeDtypeStruct((b, D_HY, s - CTX_LEN), F32),
                   jax.ShapeDtypeStruct((b, D_HY, CTX_LEN), F32)],
        compiler_params=_params(2),
        name="hyena_gates",
    )(hy, hy, hy, short_w, short_b)


@functools.lru_cache(maxsize=None)
def _filter_positions(length):
    n = np.arange(2 * length)
    m = np.where(n < length, n, 2 * length - n).astype(np.float64)
    m = np.where(n == length, 0.0, m)
    t = m / (length - 1)
    bands = np.linspace(1e-4, HY_BANDS - 1.0, HY_BANDS)
    ang = (2.0 * math.pi / length) * m[None, :] * bands[:, None]
    z = np.zeros((HY_EMB_PAD, 2 * length))
    z[0] = t
    z[1:1 + HY_BANDS] = np.cos(ang)
    z[1 + HY_BANDS:HY_EMB] = -np.sin(ang)
    return z.astype(np.float32)


def _filter_kernel(z_ref, w1_ref, b1_ref, w2_ref, b2_ref, w3_ref, fr_ref, dl_ref, k_ref, *, length, pb):
    z = z_ref[...]
    fr = fr_ref[...]
    hdn = jnp.sin(fr * (_dot(w1_ref[...], z, HIGHEST) + b1_ref[...]))
    for i in range(HY_INNER):
        hdn = jnp.sin(fr * (_dot(w2_ref[i], hdn, HIGHEST) + b2_ref[i]))
    h = _dot(w3_ref[0], hdn, HIGHEST)
    window = jnp.exp(-dl_ref[...] * z[0:1]) + HY_SHIFT
    pos = pl.program_id(0) * pb + lax.broadcasted_iota(jnp.int32, (1, pb), 1)
    k_ref[...] = jnp.where(pos == length, 0.0, h * window)


def _filter_call(length, w1, b1, w2, b2, w3, freq):
    pb = min(2048, length)
    zf = jnp.asarray(_filter_positions(length))
    w1t = jnp.pad(w1.T, ((0, 0), (0, HY_EMB_PAD - HY_EMB)))
    w2t = jnp.swapaxes(w2, 1, 2)
    w3t = w3.T.reshape(2, D_HY, HY_FILTER_WIDTH)
    col = lambda a: a.reshape(a.shape + (1,))
    deltas = np.abs(np.linspace(HY_MIN_DECAY, HY_MAX_DECAY, D_HY)).astype(np.float32)
    fw = HY_FILTER_WIDTH
    return pl.pallas_call(
        functools.partial(_filter_kernel, length=length, pb=pb),
        grid=(2 * length // pb,),
        in_specs=[pl.BlockSpec((HY_EMB_PAD, pb), lambda j: (0, j)),
                  _const_spec((fw, HY_EMB_PAD)), _const_spec((fw, 1)),
                  _const_spec((HY_INNER, fw, fw)), _const_spec((HY_INNER, fw, 1)),
                  pl.BlockSpec((1, D_HY, fw), lambda j: (jnp.where(j >= length // pb, 1, 0), 0, 0)),
                  _const_spec((fw, 1)), _const_spec((D_HY, 1))],
        out_specs=pl.BlockSpec((D_HY, pb), lambda j: (0, j)),
        out_shape=jax.ShapeDtypeStruct((D_HY, 2 * length), F32),
        compiler_params=_params(1),
        name="hyena_filter",
    )(zf, w1t, col(b1), w2t, col(b2), w3t, col(freq), jnp.asarray(deltas).reshape(D_HY, 1))


def _real_block(w):
    return np.block([[w.real, w.imag], [-w.imag, w.real]])


@functools.lru_cache(maxsize=None)
def _fft_tables(length):
    n = 2 * length
    n2 = FFT_N2
    n1 = n // n2
    f1 = np.arange(n1)
    wa = np.exp(-2j * np.pi * np.outer(f1, np.arange(n1)) / n1)
    stage_a = np.concatenate([wa.real, wa.imag], axis=0)
    tw = np.exp(-2j * np.pi * np.outer(f1, np.arange(n2)) / n)
    wb = np.exp(-2j * np.pi * np.outer(np.arange(n2), np.arange(n2)) / n2)
    inv_a = np.concatenate([wa[:n1 // 2].real, wa[:n1 // 2].imag], axis=1) / n
    f = lambda a: a.astype(np.float32)
    return dict(a_z=f(stage_a[:, :n1 // 2]), a_k=f(stage_a), tw_re=f(tw.real), tw_im=f(tw.imag),
                b_fwd=f(_real_block(wb)), b_inv=f(_real_block(np.conj(wb))), a_inv=f(inv_a))


def _fft_conv_kernel(bias_ref, z_ref, k_ref, az_ref, ak_ref, twr_ref, twi_ref, bf_ref, bi_ref, ai_ref,
                     y_ref, *, n_batch):
    twr = twr_ref[...]
    twi = twi_ref[...]
    n1 = twr.shape[0]

    def spectrum(a):
        ar, ai = a[:n1], a[n1:]
        pr = ar * twr - ai * twi
        pi = ar * twi + ai * twr
        s = _dot(jnp.concatenate([pr, pi], axis=1).astype(BF16), bf_ref[...])
        return s[:, :FFT_N2], s[:, FFT_N2:]

    def per_channel(c, carry):
        kr, ki = spectrum(_dot(ak_ref[...], k_ref[c].astype(BF16)))
        bias = bias_ref[pl.program_id(0) * FFT_CB + c]
        for b in range(n_batch):
            z = z_ref[b, c]
            xr, xi = spectrum(_dot(az_ref[...], z.astype(BF16)))
            yr = xr * kr - xi * ki
            yi = xr * ki + xi * kr
            u = _dot(jnp.concatenate([yr, yi], axis=1).astype(BF16), bi_ref[...])
            ur, ui = u[:, :FFT_N2], u[:, FFT_N2:]
            vr = ur * twr + ui * twi
            vi = ui * twr - ur * twi
            conv = _dot(ai_ref[...], jnp.concatenate([vr, vi], axis=0).astype(BF16))
            y_ref[b, c] = conv + z * bias
        return carry

    lax.fori_loop(0, FFT_CB, per_channel, 0)


def _fft_conv_call(zt, kt, bias):
    b, ch, length = zt.shape
    n1 = 2 * length // FFT_N2
    tb = _fft_tables(length)
    bf = lambda name: jnp.asarray(tb[name]).astype(BF16)
    consts = [bf("a_z"), bf("a_k"), jnp.asarray(tb["tw_re"]), jnp.asarray(tb["tw_im"]),
              bf("b_fwd"), bf("b_inv"), bf("a_inv")]
    y = pl.pallas_call(
        functools.partial(_fft_conv_kernel, n_batch=b),
        grid=(ch // FFT_CB,),
        in_specs=[pl.BlockSpec(memory_space=pltpu.SMEM),
                  pl.BlockSpec((b, FFT_CB, n1 // 2, FFT_N2), lambda j: (0, j, 0, 0)),
                  pl.BlockSpec((FFT_CB, n1, FFT_N2), lambda j: (j, 0, 0))]
                 + [_const_spec(a.shape) for a in consts],
        out_specs=pl.BlockSpec((b, FFT_CB, n1 // 2, FFT_N2), lambda j: (0, j, 0, 0)),
        out_shape=jax.ShapeDtypeStruct((b, ch, n1 // 2, FFT_N2), F32),
        compiler_params=_params(1),
        name="hyena_long_conv",
    )(bias, zt.reshape(b, ch, n1 // 2, FFT_N2), kt.reshape(ch, n1, FFT_N2), *consts)
    return y.reshape(b, ch, length)


@functools.lru_cache(maxsize=None)
def _dft_tables(length):
    n = 2 * length
    w = np.exp(-2j * np.pi * np.outer(np.arange(n), np.arange(n)) / n)
    fwd = np.concatenate([w.real, w.imag], axis=1)
    inv = np.concatenate([w.real[:, :length], w.imag[:, :length]], axis=0) / n
    return fwd[:length].astype(np.float32), fwd.astype(np.float32), inv.astype(np.float32)


def _dft_conv_kernel(z_ref, k_ref, bias_ref, fz_ref, fk_ref, fi_ref, y_ref):
    n = k_ref.shape[1]
    ks = _dot(k_ref[...], fk_ref[...], HIGHEST)
    kr, ki = ks[:, :n], ks[:, n:]
    for b in range(z_ref.shape[0]):
        z = z_ref[b]
        xs = _dot(z, fz_ref[...], HIGHEST)
        xr, xi = xs[:, :n], xs[:, n:]
        ys = jnp.concatenate([xr * kr - xi * ki, xr * ki + xi * kr], axis=1)
        y_ref[b] = _dot(ys, fi_ref[...], HIGHEST) + z * bias_ref[...]


def _dft_conv_call(zt, kt, bias):
    b, ch, length = zt.shape
    fz, fk, fi = (jnp.asarray(a) for a in _dft_tables(length))
    return pl.pallas_call(
        _dft_conv_kernel,
        out_shape=jax.ShapeDtypeStruct((b, ch, length), F32),
        compiler_params=pltpu.CompilerParams(vmem_limit_bytes=VMEM_LIMIT),
        name="hyena_ctx_conv",
    )(zt, kt, bias.reshape(ch, 1), fz, fk, fi)


def _head_norm(o):
    mu = jnp.mean(o, axis=-1, keepdims=True)
    var = jnp.mean(jnp.square(o - mu), axis=-1, keepdims=True)
    return (o - mu) * lax.rsqrt(var + HEAD_NORM_EPS)


def _merge_kernel(*refs, with_ctx):
    if with_ctx:
        (of_ref, ob_ref, rg_ref, gr_ref, x0_ref, yl_ref, yc_ref, mg_ref, x_ref, mod_ref,
         wb_ref, wo_ref, out_ref) = refs
    else:
        (of_ref, ob_ref, rg_ref, gr_ref, x0_ref, yl_ref, mg_ref, x_ref, mod_ref,
         wb_ref, wo_ref, out_ref) = refs
    o = of_ref[0] + ob_ref[0]
    mixed = None
    for m, gate_ref in enumerate((rg_ref, gr_ref)):
        heads = [_head_norm(o[:, m * V_W + hd * DV:m * V_W + (hd + 1) * DV]) for hd in range(N_HEADS)]
        br = jnp.concatenate(heads, axis=1) * _silu(gate_ref[0].astype(F32))
        g = jax.nn.sigmoid(mg_ref[0, :, m * D_MODEL:(m + 1) * D_MODEL].astype(F32))
        term = g * _dot(br.astype(BF16), wb_ref[m])
        mixed = term if mixed is None else mixed + term
    if with_ctx:
        yt = jnp.where(pl.program_id(1) == 0, yc_ref[0], yl_ref[0])
    else:
        yt = yl_ref[0]
    hy = x0_ref[0].astype(F32) * yt.T
    g = jax.nn.sigmoid(mg_ref[0, :, 2 * D_MODEL:3 * D_MODEL].astype(F32))
    mixed = mixed + g * _dot(hy.astype(BF16), wb_ref[2])
    mix = _dot(mixed.astype(BF16), wo_ref[...])
    out_ref[0] = x_ref[0] + mod_ref[0, 0, 2:3] * mix


def _merge_call(o_f, o_b, ret, gla, x0, y_lat, y_ctx, mg, x_all, modsel, w_branch, w_out):
    b, s, d = x_all.shape
    with_ctx = y_ctx is not None
    off = 0 if with_ctx else 1
    tile = lambda w, j=0: pl.BlockSpec((1, TT, w), lambda i, t: (i, t + off, j))
    in_specs = [tile(2 * V_W), tile(2 * V_W), tile(V_W, 2), tile(V_W, 2), tile(D_HY),
                pl.BlockSpec((1, D_HY, TT), lambda i, t: (i, 0, jnp.maximum(t + off - 1, 0)))]
    args = [o_f, o_b, ret, gla, x0, y_lat]
    if with_ctx:
        in_specs.append(pl.BlockSpec((1, D_HY, CTX_LEN), lambda i, t: (i, 0, 0)))
        args.append(y_ctx)
    in_specs += [tile(3 * D_MODEL), tile(d),
                 pl.BlockSpec((1, 1, 6, d), lambda i, t: (i, jnp.minimum(t + off, 1), 0, 0)),
                 _const_spec((3, V_W, d)), _const_spec((d, d))]
    args += [mg, x_all, modsel, w_branch, w_out]
    return pl.pallas_call(
        functools.partial(_merge_kernel, with_ctx=with_ctx),
        grid=(b, s // TT - off),
        in_specs=in_specs,
        out_specs=tile(d),
        out_shape=jax.ShapeDtypeStruct((b, s, d), F32),
        compiler_params=_params(2),
        name="merge_out_proj",
    )(*args)


def _ffn_up_kernel(x_ref, mod_ref, g_ref, w_ref, a_ref, v_ref):
    m = mod_ref[0, 0]
    hb = _modulate(x_ref[0], g_ref[...], m[3:4], m[4:5]).astype(BF16)
    a_ref[0] = _dot(hb, w_ref[:, :D_FF]).astype(BF16)
    v_ref[0] = _dot(hb, w_ref[:, D_FF:]).astype(BF16)


def _ffn_up_call(x_all, modsel, g2, w_up, off):
    b, s, d = x_all.shape
    tile = lambda w: pl.BlockSpec((1, TT, w), lambda i, t: (i, t + off, 0))
    return pl.pallas_call(
        _ffn_up_kernel,
        grid=(b, s // TT - off),
        in_specs=[tile(d),
                  pl.BlockSpec((1, 1, 6, d), lambda i, t: (i, jnp.minimum(t + off, 1), 0, 0)),
                  _const_spec((1, d)), _const_spec((d, 2 * D_FF))],
        out_specs=[tile(D_FF), tile(D_FF)],
        out_shape=[jax.ShapeDtypeStruct((b, s, D_FF), BF16)] * 2,
        compiler_params=_params(2),
        name="ffn_up",
    )(x_all, modsel, g2, w_up)


def _gelu(x):
    return 0.5 * x * (1.0 + lax.erf(x * (2.0 ** -0.5)))


def _ffn_down_kernel(a_ref, ap_ref, an_ref, v_ref, x_ref, mod_ref, cw_ref, cb_ref, wd_ref, fg_ref,
                     out_ref, act_ref, *, off, nt, final):
    t = pl.program_id(1) + off
    is_ctx = t == 0
    gw = GRID_W
    up_ok = t >= 2
    dn_ok = jnp.logical_and(t >= 1, t <= nt - 2)
    p = lax.broadcasted_iota(jnp.int32, (TT + 2 * gw, LANES), 0) - gw
    col = jnp.where(is_ctx, p, jnp.bitwise_and(p, gw - 1))
    width = jnp.where(is_ctx, TT, gw)
    first_col = col == 0
    last_col = col == width - 1
    row_w = jnp.where(is_ctx, 0.0, 1.0)
    zero_halo = jnp.zeros((gw, LANES), BF16)

    def chunk(j, carry):
        cs = pl.ds(pl.multiple_of(j * LANES, LANES), LANES)
        above = jnp.where(up_ok, ap_ref[0, :, cs], zero_halo)
        below = jnp.where(dn_ok, an_ref[0, :, cs], zero_halo)
        ext = jnp.concatenate([above, a_ref[0, :, cs], below], axis=0).astype(F32)
        n_ext = TT + 2 * gw
        left = jnp.where(first_col, 0.0, pltpu.roll(ext, 1, 0))
        right = jnp.where(last_col, 0.0, pltpu.roll(ext, n_ext - 1, 0))
        acc = jnp.broadcast_to(cb_ref[:, cs], (TT, LANES))
        for di in range(3):
            rows = slice(di * gw, di * gw + TT)
            term = (left[rows] * cw_ref[3 * di:3 * di + 1, cs] + ext[rows] * cw_ref[3 * di + 1:3 * di + 2, cs]
                    + right[rows] * cw_ref[3 * di + 2:3 * di + 3, cs])
            acc = acc + (term if di == 1 else row_w * term)
        act_ref[:, cs] = (_gelu(acc) * v_ref[0, :, cs].astype(F32)).astype(BF16)
        return carry

    lax.fori_loop(0, D_FF // LANES, chunk, 0)
    y = x_ref[0] + mod_ref[0, 0, 5:6] * _dot(act_ref[...], wd_ref[...])
    if final:
        y = y * lax.rsqrt(jnp.mean(y * y, axis=-1, keepdims=True) + NORM_EPS) * fg_ref[...]
    out_ref[0] = y


def _ffn_down_call(a, v, x_all, modsel, conv_w, conv_b, w_down, final_g, off, final):
    b, s, d = x_all.shape
    nt = s // TT
    gw = GRID_W
    r = TT // gw
    n_rows = s // gw
    tile = lambda w: pl.BlockSpec((1, TT, w), lambda i, t: (i, t + off, 0))
    if final:
        out_spec = pl.BlockSpec((1, TT, d), lambda i, t: (i, t, 0))
        out_shape = jax.ShapeDtypeStruct((b, s - CTX_LEN, d), F32)
    else:
        out_spec = tile(d)
        out_shape = jax.ShapeDtypeStruct((b, s, d), F32)
    return pl.pallas_call(
        functools.partial(_ffn_down_kernel, off=off, nt=nt, final=final),
        grid=(b, nt - off),
        in_specs=[tile(D_FF),
                  pl.BlockSpec((1, gw, D_FF), lambda i, t: (i, jnp.maximum((t + off) * r - 1, 0), 0)),
                  pl.BlockSpec((1, gw, D_FF), lambda i, t: (i, jnp.minimum((t + off + 1) * r, n_rows - 1), 0)),
                  tile(D_FF), tile(d),
                  pl.BlockSpec((1, 1, 6, d), lambda i, t: (i, jnp.minimum(t + off, 1), 0, 0)),
                  _const_spec((9, D_FF)), _const_spec((1, D_FF)), _const_spec((D_FF, d)),
                  _const_spec((1, d))],
        out_specs=out_spec,
        out_shape=out_shape,
        scratch_shapes=[pltpu.VMEM((TT, D_FF), BF16)],
        compiler_params=_params(2),
        name="ffn_down",
    )(a, a, a, v, x_all, modsel, conv_w, conv_b, w_down, final_g)


def kernel(x, c, ctx, c_ctx, ada_w, ada_b, norm1_g, w_in, gla_wa2, gla_ba, hy_short_w, hy_short_b,
           hy_w1, hy_b1, hy_w2, hy_b2, hy_w3, hy_freq, hy_bias, w_branch, w_out, norm2_g, w_up,
           ffn_conv_w, ffn_conv_b, w_down, final_g):
    bsz, seq, d = x.shape
    depth = ada_w.shape[0]
    assert d == D_MODEL and ctx.shape[1] == CTX_LEN == TT and seq % TT == 0
    assert (2 * seq) % (2 * FFT_N2) == 0 and bsz + 1 <= 8

    x_all = jnp.concatenate([ctx, x], axis=1)
    cvec = jnp.zeros((8, d), F32).at[:bsz].set(c).at[bsz].set(c_ctx)
    mod_all = _ada_call(cvec, ada_w, ada_b).reshape(depth, 8, 6, d)

    out = None
    for l in range(depth):
        last = l == depth - 1
        m = mod_all[l]
        modsel = jnp.stack([jnp.broadcast_to(m[bsz], (bsz, 6, d)), m[:bsz]], axis=1)
        w = w_in[l]
        w_main = jnp.concatenate([w[:, :LR_OFF], w[:, LR_OFF + 2 * GLA_LOWRANK:]], axis=1).astype(BF16)
        w_lr = w[:, LR_OFF:LR_OFF + 2 * GLA_LOWRANK].reshape(d, 2, GLA_LOWRANK).transpose(1, 0, 2)
        ret, gla, hy, mg, lga = _inproj_call(x_all, modsel, norm1_g[l].reshape(1, d), w_main, w_lr,
                                             gla_wa2[l], gla_ba[l].reshape(2, 1, QK_W))
        o_f, o_b = _scan_call(ret, gla, lga)

        x0, z_lat, z_ctx = _hy_prep_call(hy, hy_short_w[l], hy_short_b[l].reshape(1, 3 * D_HY))
        filt = (hy_w1[l], hy_b1[l], hy_w2[l], hy_b2[l], hy_w3[l], hy_freq[l])
        y_lat = _fft_conv_call(z_lat, _filter_call(seq, *filt), hy_bias[l])
        y_ctx = None if last else _dft_conv_call(z_ctx, _filter_call(CTX_LEN, *filt), hy_bias[l])

        x_all = _merge_call(o_f, o_b, ret, gla, x0, y_lat, y_ctx, mg, x_all, modsel,
                            w_branch[l].astype(BF16), w_out[l].astype(BF16))
        off = 1 if last else 0
        a, v = _ffn_up_call(x_all, modsel, norm2_g[l].reshape(1, d), w_up[l].astype(BF16), off)
        out = _ffn_down_call(a, v, x_all, modsel, ffn_conv_w[l].reshape(9, D_FF),
                             ffn_conv_b[l].reshape(1, D_FF), w_down[l].astype(BF16),
                             final_g.reshape(1, d), off, last)
        x_all = out
    return out
```

```python
import functools
import math

import numpy as np
import jax
import jax.numpy as jnp
from jax import lax
from jax.experimental import pallas as pl
from jax.experimental.pallas import tpu as pltpu

F32 = jnp.float32
BF16 = jnp.bfloat16
HIGHEST = lax.Precision.HIGHEST

D_MODEL = 1024
CTX_LEN = 256
GRID_W = 64
NORM_EPS = 1e-6
HEAD_NORM_EPS = 1e-5

N_HEADS = 4
DK = 64
DV = 128
QK_W = N_HEADS * DK
V_W = N_HEADS * DV
RET_LOG_DECAY = (tuple(math.log1p(-2.0 ** (-5.0 - h)) for h in range(N_HEADS)),
                 tuple(math.log1p(-2.0 ** (-5.5 - h)) for h in range(N_HEADS)))
GLA_LOWRANK = 16
GLA_GATE_NORM = 16.0
GLA_CHUNK = 64

D_HY = 512
HY_BANDS = 16
HY_EMB = 1 + 2 * HY_BANDS
HY_EMB_PAD = 40
HY_FILTER_WIDTH = 64
HY_INNER = 2
HY_MIN_DECAY = math.log(1e-2) / 1.5
HY_MAX_DECAY = math.log(1e-2) / 0.3
HY_SHIFT = 0.05

D_FF = 2816
D_IN_MAIN = 7680
LR_OFF = 3072

TT = 256
LANES = 128
FFT_N2 = 128
FFT_CB = 16
VMEM_LIMIT = 56 * 1024 * 1024


def _dot(a, b, precision=None):
    return jnp.dot(a, b, preferred_element_type=F32, precision=precision)


def _dot_nt(a, b):
    return lax.dot_general(a, b, (((1,), (1,)), ((), ())), preferred_element_type=F32)


def _dot_tn(a, b):
    return lax.dot_general(a, b, (((0,), (0,)), ((), ())), preferred_element_type=F32)


def _params(n_axes):
    return pltpu.CompilerParams(dimension_semantics=("arbitrary",) * n_axes,
                                vmem_limit_bytes=VMEM_LIMIT)


def _const_spec(shape):
    n = len(shape)
    return pl.BlockSpec(shape, lambda *_: (0,) * n, pipeline_mode=pl.Buffered(1))


def _modulate(x, g, shift, scale):
    y = x * lax.rsqrt(jnp.mean(x * x, axis=-1, keepdims=True) + NORM_EPS) * g
    return y * (1.0 + scale) + shift


def _sigmoid(x):
    return 0.5 * jnp.tanh(0.5 * x) + 0.5


def _silu(x):
    return x * _sigmoid(x)


def _ada_kernel(c_ref, w_ref, b_ref, o_ref):
    o_ref[0] = _dot(_silu(c_ref[...]), w_ref[0], HIGHEST) + b_ref[0]


def _ada_call(cvec, ada_w, ada_b):
    depth, d, n = ada_w.shape
    nb = 1536
    return pl.pallas_call(
        _ada_kernel,
        grid=(depth, n // nb),
        in_specs=[pl.BlockSpec((8, d), lambda l, j: (0, 0)),
                  pl.BlockSpec((1, d, nb), lambda l, j: (l, 0, j)),
                  pl.BlockSpec((1, 1, nb), lambda l, j: (l, 0, j))],
        out_specs=pl.BlockSpec((1, 8, nb), lambda l, j: (l, 0, j)),
        out_shape=jax.ShapeDtypeStruct((depth, 8, n), F32),
        compiler_params=_params(2),
        name="ada_ln",
    )(cvec, ada_w, ada_b.reshape(depth, 1, n))


def _log_sigmoid(x):
    return jnp.minimum(x, 0.0) - jnp.log1p(jnp.exp(-jnp.abs(x)))


def _split_bf16(a):
    hi = a.astype(BF16)
    return hi, (a - hi.astype(F32)).astype(BF16)


def _inproj_kernel(x_ref, mod_ref, g_ref, wm_ref, w2_ref, ba_ref,
                   ret_ref, gla_ref, hy_ref, mg_ref, lga_ref):
    m = mod_ref[0, 0]
    h = _modulate(x_ref[0], g_ref[...], m[0:1], m[1:2])
    hb = h.astype(BF16)
    ret_ref[0] = _dot(hb, wm_ref[:, 0:1536]).astype(BF16)
    gla_ref[0] = _dot(hb, wm_ref[:, 1536:3072]).astype(BF16)
    hy_ref[0] = _dot(hb, wm_ref[:, 3072:4608]).astype(BF16)
    mg_ref[0] = _dot(hb, wm_ref[:, 4608:D_IN_MAIN]).astype(BF16)
    lr_hi, lr_lo = _split_bf16(_dot(hb, wm_ref[:, D_IN_MAIN:]))
    gate = _dot(jnp.concatenate([lr_hi, lr_lo, lr_hi], axis=1), w2_ref[...]) + ba_ref[...]
    lga_ref[0] = _log_sigmoid(gate) * (1.0 / GLA_GATE_NORM)


def _gate_weights(w_lr, wa2):
    d = w_lr.shape[0]
    hi, lo = _split_bf16(w_lr)
    w1 = jnp.concatenate([hi, lo, jnp.zeros((d, LANES - 4 * GLA_LOWRANK), BF16)], axis=1)
    bd = jnp.zeros((2 * GLA_LOWRANK, 2 * QK_W), F32)
    bd = bd.at[:GLA_LOWRANK, :QK_W].set(wa2[0]).at[GLA_LOWRANK:, QK_W:].set(wa2[1])
    w2 = jnp.concatenate([bd, bd, jnp.zeros((LANES - 4 * GLA_LOWRANK, 2 * QK_W), F32)], axis=0)
    w2_hi, w2_lo = _split_bf16(w2)
    return w1, jnp.concatenate([w2_hi, w2_hi, w2_lo], axis=0)


def _inproj_call(x_all, modsel, g1, w_main, w2, ba):
    b, s, d = x_all.shape
    nt = s // TT
    tile = lambda w: pl.BlockSpec((1, TT, w), lambda i, t: (i, t, 0))
    bf = lambda w: jax.ShapeDtypeStruct((b, s, w), BF16)
    return pl.pallas_call(
        _inproj_kernel,
        grid=(b, nt),
        in_specs=[tile(d),
                  pl.BlockSpec((1, 1, 6, d), lambda i, t: (i, jnp.minimum(t, 1), 0, 0)),
                  _const_spec((1, d)),
                  _const_spec((d, D_IN_MAIN + LANES)),
                  _const_spec((3 * LANES, 2 * QK_W)),
                  _const_spec((1, 2 * QK_W))],
        out_specs=[tile(1536), tile(1536), tile(1536), tile(3072), tile(2 * QK_W)],
        out_shape=[bf(1536), bf(1536), bf(1536), bf(3072),
                   jax.ShapeDtypeStruct((b, s, 2 * QK_W), F32)],
        compiler_params=_params(2),
        name="in_proj",
    )(x_all, modsel, g1, w_main, w2, ba)


@functools.lru_cache(maxsize=None)
def _ret_tables():
    i = np.arange(TT, dtype=np.float64)
    scale = DK ** -0.5
    dmat = np.zeros((2, N_HEADS, TT, TT))
    erow = np.zeros((2, N_HEADS, TT, DV))
    kw = np.zeros((2, N_HEADS, TT, DK))
    for d in range(2):
        for h in range(N_HEADS):
            lg = RET_LOG_DECAY[d][h]
            diff = i[:, None] - i[None, :]
            if d == 0:
                dmat[d, h] = np.where(diff >= 0, np.exp(np.maximum(diff, 0) * lg), 0.0) * scale
                erow[d, h] = np.exp((i + 1) * lg)[:, None]
                kw[d, h] = (np.exp((TT - 1 - i) * lg) * scale)[:, None]
            else:
                dmat[d, h] = np.where(diff < 0, np.exp(np.maximum(-diff, 0) * lg), 0.0) * scale
                erow[d, h] = np.exp((TT - i) * lg)[:, None]
                kw[d, h] = (np.exp(i * lg) * scale)[:, None]
    return dmat.astype(np.float32), erow.astype(np.float32), kw.astype(np.float32)


def _scan_kernel(retf_ref, glaf_ref, lgf_ref, retb_ref, glab_ref, lgb_ref,
                 dmat_ref, erow_ref, kw_ref, of_ref, ob_ref, h_ref):
    @pl.when(pl.program_id(1) == 0)
    def _():
        h_ref[...] = jnp.zeros_like(h_ref)

    c = GLA_CHUNK
    row = lax.broadcasted_iota(jnp.int32, (c, c), 0)
    col = lax.broadcasted_iota(jnp.int32, (c, c), 1)
    dirs = ((retf_ref, glaf_ref, lgf_ref, of_ref), (retb_ref, glab_ref, lgb_ref, ob_ref))
    for d, (ret_ref, gla_ref, lg_ref, o_ref) in enumerate(dirs):
        for hd in range(N_HEADS):
            q = ret_ref[0, :, hd * DK:(hd + 1) * DK]
            k = ret_ref[0, :, QK_W + hd * DK:QK_W + (hd + 1) * DK]
            v = ret_ref[0, :, 2 * QK_W + hd * DV:2 * QK_W + (hd + 1) * DV]
            hs = h_ref[d * 8 + hd]
            s = _dot_nt(q, k) * dmat_ref[d, hd]
            o = _dot(s.astype(BF16), v) + erow_ref[d, hd] * _dot(q, hs.astype(BF16))
            kd = (k.astype(F32) * kw_ref[d, hd]).astype(BF16)
            a_tile = math.exp(TT * RET_LOG_DECAY[d][hd])
            h_ref[d * 8 + hd] = a_tile * hs + _dot_tn(kd, v)
            o_ref[0, :, hd * DV:(hd + 1) * DV] = o
        if d == 0:
            tri = (row >= col).astype(F32)
            mask = row >= col
        else:
            tri = (row <= col).astype(F32)
            mask = row < col
        order = range(TT // c) if d == 0 else range(TT // c - 1, -1, -1)
        for sc in order:
            rows = slice(sc * c, (sc + 1) * c)
            cum = _dot(tri, lg_ref[0, rows, :], HIGHEST)
            last = cum[c - 1:c] if d == 0 else cum[0:1]
            q = gla_ref[0, rows, 0:QK_W].astype(F32)
            k = gla_ref[0, rows, QK_W:2 * QK_W].astype(F32)
            q_in = (q * jnp.exp(cum) * DK ** -0.5).astype(BF16)
            k_neg = (k * jnp.exp(-cum)).astype(BF16)
            k_out = (k * jnp.exp(last - cum)).astype(BF16)
            a_col = jnp.broadcast_to(jnp.exp(last), (DV, QK_W)).T
            for hd in range(N_HEADS):
                ks = slice(hd * DK, (hd + 1) * DK)
                v = gla_ref[0, rows, 2 * QK_W + hd * DV:2 * QK_W + (hd + 1) * DV]
                hs = h_ref[d * 8 + 4 + hd]
                s = jnp.where(mask, _dot_nt(q_in[:, ks], k_neg[:, ks]), 0.0)
                o = _dot(s.astype(BF16), v) + _dot(q_in[:, ks], hs.astype(BF16))
                h_ref[d * 8 + 4 + hd] = a_col[ks] * hs + _dot_tn(k_out[:, ks], v)
                o_ref[0, rows, V_W + hd * DV:V_W + (hd + 1) * DV] = o


def _scan_call(ret, gla, lga):
    b, s, _ = ret.shape
    nt = s // TT
    dmat, erow, kw = (jnp.asarray(z) for z in _ret_tables())
    fwd = lambda i, t: (i, t, 0)
    bwd = lambda i, t: (i, jnp.where(t == 0, 0, nt - t), 0)
    bwd_g = lambda i, t: (i, jnp.where(t == 0, 0, nt - t), 1)
    qkv = lambda f: pl.BlockSpec((1, TT, 2 * QK_W + V_W), f)
    return pl.pallas_call(
        _scan_kernel,
        grid=(b, nt),
        in_specs=[qkv(fwd), qkv(fwd), pl.BlockSpec((1, TT, QK_W), fwd),
                  qkv(bwd), qkv(bwd), pl.BlockSpec((1, TT, QK_W), bwd_g),
                  _const_spec(dmat.shape), _const_spec(erow.shape), _const_spec(kw.shape)],
        out_specs=[pl.BlockSpec((1, TT, 2 * V_W), fwd), pl.BlockSpec((1, TT, 2 * V_W), bwd)],
        out_shape=[jax.ShapeDtypeStruct((b, s, 2 * V_W), F32)] * 2,
        scratch_shapes=[pltpu.VMEM((16, DK, DV), F32)],
        compiler_params=_params(2),
        name="bidir_scan",
    )(ret, gla, lga, ret, gla, lga, dmat, erow, kw)


def _hy_prep_kernel(p_ref, prev_ref, next_ref, w_ref, b_ref, x0_ref, zl_ref, zc_ref, *, nt):
    t = pl.program_id(1)
    has_prev = t >= 2
    has_next = jnp.logical_and(t >= 1, t <= nt - 2)
    ridx = lax.broadcasted_iota(jnp.int32, (TT, LANES), 0)
    u = []
    for part in range(3):
        cols = []
        for j in range(D_HY // LANES):
            cs = slice(part * D_HY + j * LANES, part * D_HY + (j + 1) * LANES)
            p = p_ref[0, :, cs].astype(F32)
            pv = jnp.where(has_prev, prev_ref[0, 15:16, cs].astype(F32), 0.0)
            nx = jnp.where(has_next, next_ref[0, 0:1, cs].astype(F32), 0.0)
            up = jnp.where(ridx == 0, pv, pltpu.roll(p, 1, 0))
            dn = jnp.where(ridx == TT - 1, nx, pltpu.roll(p, TT - 1, 0))
            cols.append(b_ref[:, cs] + up * w_ref[0:1, cs] + p * w_ref[1:2, cs] + dn * w_ref[2:3, cs])
        u.append(cols)
    x0_ref[0] = jnp.concatenate(u[0], axis=1).astype(BF16)
    z = jnp.concatenate([a * c for a, c in zip(u[1], u[2])], axis=1)
    zt = z.T

    @pl.when(t == 0)
    def _():
        zc_ref[0] = zt

    @pl.when(t > 0)
    def _():
        for r in range(TT // FFT_N2):
            zl_ref[0, :, r * FFT_CB:(r + 1) * FFT_CB, :] = _to_conv_rows(zt[:, r * FFT_N2:(r + 1) * FFT_N2])


def _hy_prep_call(hy, short_w, short_b):
    b, s, w = hy.shape
    nt = s // TT
    hb = 16
    nhb = s // hb
    return pl.pallas_call(
        functools.partial(_hy_prep_kernel, nt=nt),
        grid=(b, nt),
        in_specs=[pl.BlockSpec((1, TT, w), lambda i, t: (i, t, 0)),
                  pl.BlockSpec((1, hb, w), lambda i, t: (i, jnp.maximum(t * (TT // hb) - 1, 0), 0)),
                  pl.BlockSpec((1, hb, w), lambda i, t: (i, jnp.minimum((t + 1) * (TT // hb), nhb - 1), 0)),
                  _const_spec((3, w)), _const_spec((1, w))],
        out_specs=[pl.BlockSpec((1, TT, D_HY), lambda i, t: (i, t, 0)),
                   pl.BlockSpec((1, D_HY // FFT_CB, TT // FFT_N2 * FFT_CB, FFT_N2),
                                lambda i, t: (i, 0, jnp.maximum(t - 1, 0), 0)),
                   pl.BlockSpec((1, D_HY, CTX_LEN), lambda i, t: (i, 0, 0))],
        out_shape=[jax.ShapeDtypeStruct((b, s, D_HY), BF16),
                   jax.ShapeDtypeStruct((b, D_HY // FFT_CB, (s - CTX_LEN) // FFT_N2 * FFT_CB, FFT_N2), F32),
                   jax.ShapeDtypeStruct((b, D_HY, CTX_LEN), F32)],
        compiler_params=_params(2),
        name="hyena_gates",
    )(hy, hy, hy, short_w, short_b)


@functools.lru_cache(maxsize=None)
def _filter_positions(length):
    n = np.arange(2 * length)
    m = np.where(n < length, n, 2 * length - n).astype(np.float64)
    m = np.where(n == length, 0.0, m)
    t = m / (length - 1)
    bands = np.linspace(1e-4, HY_BANDS - 1.0, HY_BANDS)
    ang = (2.0 * math.pi / length) * m[None, :] * bands[:, None]
    z = np.zeros((HY_EMB_PAD, 2 * length))
    z[0] = t
    z[1:1 + HY_BANDS] = np.cos(ang)
    z[1 + HY_BANDS:HY_EMB] = -np.sin(ang)
    return z.astype(np.float32)


def _filter_kernel(z_ref, w1_ref, b1_ref, w2_ref, b2_ref, w3_ref, fr_ref, dl_ref, k_ref, *, length, pb, split):
    z = z_ref[...]
    fr = fr_ref[...]
    hdn = jnp.sin(fr * (_dot(w1_ref[...], z, HIGHEST) + b1_ref[...]))
    for i in range(HY_INNER):
        hdn = jnp.sin(fr * (_dot(w2_ref[i], hdn, HIGHEST) + b2_ref[i]))
    h = _dot(w3_ref[0], hdn, HIGHEST)
    window = jnp.exp(-dl_ref[...] * z[0:1]) + HY_SHIFT
    pos = pl.program_id(0) * pb + lax.broadcasted_iota(jnp.int32, (1, pb), 1)
    k = jnp.where(pos == length, 0.0, h * window)
    if split:
        for r in range(pb // FFT_N2):
            k_ref[:, r * FFT_CB:(r + 1) * FFT_CB, :] = _to_conv_rows(k[:, r * FFT_N2:(r + 1) * FFT_N2])
    else:
        k_ref[...] = k


def _filter_call(length, w1, b1, w2, b2, w3, freq, split):
    pb = min(2048, length)
    if split:
        out_spec = pl.BlockSpec((D_HY // FFT_CB, pb // FFT_N2 * FFT_CB, FFT_N2), lambda j: (0, j, 0))
        out_shape = jax.ShapeDtypeStruct((D_HY // FFT_CB, 2 * length // FFT_N2 * FFT_CB, FFT_N2), F32)
    else:
        out_spec = pl.BlockSpec((D_HY, pb), lambda j: (0, j))
        out_shape = jax.ShapeDtypeStruct((D_HY, 2 * length), F32)
    zf = jnp.asarray(_filter_positions(length))
    w1t = jnp.pad(w1.T, ((0, 0), (0, HY_EMB_PAD - HY_EMB)))
    w2t = jnp.swapaxes(w2, 1, 2)
    w3t = w3.T.reshape(2, D_HY, HY_FILTER_WIDTH)
    col = lambda a: a.reshape(a.shape + (1,))
    deltas = np.abs(np.linspace(HY_MIN_DECAY, HY_MAX_DECAY, D_HY)).astype(np.float32)
    fw = HY_FILTER_WIDTH
    return pl.pallas_call(
        functools.partial(_filter_kernel, length=length, pb=pb, split=split),
        grid=(2 * length // pb,),
        in_specs=[pl.BlockSpec((HY_EMB_PAD, pb), lambda j: (0, j)),
                  _const_spec((fw, HY_EMB_PAD)), _const_spec((fw, 1)),
                  _const_spec((HY_INNER, fw, fw)), _const_spec((HY_INNER, fw, 1)),
                  pl.BlockSpec((1, D_HY, fw), lambda j: (jnp.where(j >= length // pb, 1, 0), 0, 0)),
                  _const_spec((fw, 1)), _const_spec((D_HY, 1))],
        out_specs=out_spec,
        out_shape=out_shape,
        compiler_params=_params(1),
        name="hyena_filter",
    )(zf, w1t, col(b1), w2t, col(b2), w3t, col(freq), jnp.asarray(deltas).reshape(D_HY, 1))


def _real_block(w):
    return np.block([[w.real, w.imag], [-w.imag, w.real]])


@functools.lru_cache(maxsize=None)
def _fft_tables(length):
    n = 2 * length
    n2 = FFT_N2
    n1 = n // n2
    f1 = np.arange(n1)
    wa = np.exp(-2j * np.pi * np.outer(f1, np.arange(n1)) / n1)
    stage_a = np.concatenate([wa.real, wa.imag], axis=0)
    tw = np.exp(-2j * np.pi * np.outer(f1, np.arange(n2)) / n)
    wb = np.exp(-2j * np.pi * np.outer(np.arange(n2), np.arange(n2)) / n2)
    wh = wa[:, :n1 // 2]
    a_pair = np.block([[wh.real, -wh.imag], [wh.imag, wh.real]])
    wi = np.conj(wa[:n1 // 2]) / n
    a_inv = np.block([[wi.real, -wi.imag], [wi.imag, wi.real]])
    f = lambda a: a.astype(np.float32)
    return dict(a_pair=f(a_pair), a_k=f(stage_a), tw_re=f(tw.real), tw_im=f(tw.imag),
                b_fwd=f(_real_block(wb)), b_inv=f(_real_block(np.conj(wb))), a_inv=f(a_inv))


def _fft_conv_kernel(bias_ref, z_ref, k_ref, ap_ref, ak_ref, twr_ref, twi_ref, bf_ref, bi_ref, ai_ref,
                     y_ref, sz_ref, sk_ref, sv_ref):
    n1, n2 = twr_ref.shape
    h, cb = n1 // 2, FFT_CB
    z0, z1 = (z_ref.at[b, 0] for b in range(2))
    y0, y1 = (y_ref.at[b, 0] for b in range(2))
    kk = k_ref.at[0]
    twr = twr_ref[...]
    twi = twi_ref[...]

    def twiddled(a):
        ar, ai = a[:n1], a[n1:]
        return jnp.concatenate([ar * twr - ai * twi, ar * twi + ai * twr], axis=1).astype(BF16)

    def stage_a(p, carry):
        pair = (2 * p, 2 * p + 1)
        zc = jnp.concatenate([jnp.concatenate([z0[pl.ds(c, h, stride=cb), :], z1[pl.ds(c, h, stride=cb), :]], axis=0)
                              for c in pair], axis=1)
        kc = jnp.concatenate([kk[pl.ds(c, n1, stride=cb), :] for c in pair], axis=1)
        az = _dot(ap_ref[...], zc.astype(BF16))
        ak = _dot(ak_ref[...], kc.astype(BF16))
        for i, c in enumerate(pair):
            rows = pl.ds(pl.multiple_of(c * n1, n1), n1)
            sz_ref[rows, :] = twiddled(az[:, i * n2:(i + 1) * n2])
            sk_ref[rows, :] = twiddled(ak[:, i * n2:(i + 1) * n2])
        return carry

    lax.fori_loop(0, cb // 2, stage_a, 0, unroll=4)

    gb = 4

    def stage_b(g, carry):
        rows = pl.ds(pl.multiple_of(g * (gb * n1), gb * n1), gb * n1)
        xs = _dot(sz_ref[rows, :], bf_ref[...])
        ks = _dot(sk_ref[rows, :], bf_ref[...])
        xr, xi, kr, ki = xs[:, :n2], xs[:, n2:], ks[:, :n2], ks[:, n2:]
        ys = jnp.concatenate([xr * kr - xi * ki, xr * ki + xi * kr], axis=1).astype(BF16)
        u = _dot(ys, bi_ref[...])
        for i in range(gb):
            ur, ui = u[i * n1:(i + 1) * n1, :n2], u[i * n1:(i + 1) * n1, n2:]
            lanes = slice((i % 2) * n2, (i % 2 + 1) * n2)
            sv_ref[g * (gb // 2) + i // 2, :n1, lanes] = (ur * twr + ui * twi).astype(BF16)
            sv_ref[g * (gb // 2) + i // 2, n1:, lanes] = (ui * twr - ur * twi).astype(BF16)
        return carry

    lax.fori_loop(0, cb // gb, stage_b, 0, unroll=2)

    def stage_c(p, carry):
        y = _dot(ai_ref[...], sv_ref[p])
        for i in range(2):
            c = 2 * p + i
            bias = bias_ref[pl.program_id(0) * cb + c]
            sel = pl.ds(c, h, stride=cb)
            y0[sel, :] = y[:h, i * n2:(i + 1) * n2] + z0[sel, :] * bias
            y1[sel, :] = y[h:, i * n2:(i + 1) * n2] + z1[sel, :] * bias
        return carry

    lax.fori_loop(0, cb // 2, stage_c, 0, unroll=4)


def _to_conv_rows(a):
    return a.reshape(a.shape[0] // FFT_CB, FFT_CB, a.shape[1])


def _fft_conv_call(zq, kq, bias):
    cb = FFT_CB
    b, nblk, rows, n2 = zq.shape
    assert b == 2 and n2 == FFT_N2
    h, ch = rows // cb, nblk * cb
    n1 = 2 * h
    tb = _fft_tables(h * n2)
    bf = lambda name: jnp.asarray(tb[name]).astype(BF16)
    consts = [bf("a_pair"), bf("a_k"), jnp.asarray(tb["tw_re"]), jnp.asarray(tb["tw_im"]),
              bf("b_fwd"), bf("b_inv"), bf("a_inv")]
    return pl.pallas_call(
        _fft_conv_kernel,
        grid=(nblk,),
        in_specs=[pl.BlockSpec(memory_space=pltpu.SMEM),
                  pl.BlockSpec((b, 1, h * cb, n2), lambda j: (0, j, 0, 0)),
                  pl.BlockSpec((1, n1 * cb, n2), lambda j: (j, 0, 0))]
                 + [_const_spec(a.shape) for a in consts],
        out_specs=pl.BlockSpec((b, 1, h * cb, n2), lambda j: (0, j, 0, 0)),
        out_shape=jax.ShapeDtypeStruct(zq.shape, F32),
        scratch_shapes=[pltpu.VMEM((cb * n1, 2 * n2), BF16), pltpu.VMEM((cb * n1, 2 * n2), BF16),
                        pltpu.VMEM((cb // 2, 2 * n1, 2 * n2), BF16)],
        compiler_params=_params(1),
        name="hyena_long_conv",
    )(bias, zq, kq, *consts)


@functools.lru_cache(maxsize=None)
def _dft_tables(length):
    n = 2 * length
    w = np.exp(-2j * np.pi * np.outer(np.arange(n), np.arange(n)) / n)
    fwd = np.concatenate([w.real, w.imag], axis=1)
    inv = np.concatenate([w.real[:, :length], w.imag[:, :length]], axis=0) / n
    return fwd[:length].astype(np.float32), fwd.astype(np.float32), inv.astype(np.float32)


def _dft_conv_kernel(z_ref, k_ref, bias_ref, fz_ref, fk_ref, fi_ref, y_ref):
    n = k_ref.shape[1]
    ks = _dot(k_ref[...], fk_ref[...], HIGHEST)
    kr, ki = ks[:, :n], ks[:, n:]
    for b in range(z_ref.shape[0]):
        z = z_ref[b]
        xs = _dot(z, fz_ref[...], HIGHEST)
        xr, xi = xs[:, :n], xs[:, n:]
        ys = jnp.concatenate([xr * kr - xi * ki, xr * ki + xi * kr], axis=1)
        y_ref[b] = _dot(ys, fi_ref[...], HIGHEST) + z * bias_ref[...]


def _dft_conv_call(zt, kt, bias):
    b, ch, length = zt.shape
    fz, fk, fi = (jnp.asarray(a) for a in _dft_tables(length))
    return pl.pallas_call(
        _dft_conv_kernel,
        out_shape=jax.ShapeDtypeStruct((b, ch, length), F32),
        compiler_params=pltpu.CompilerParams(vmem_limit_bytes=VMEM_LIMIT),
        name="hyena_ctx_conv",
    )(zt, kt, bias.reshape(ch, 1), fz, fk, fi)


def _head_norm(o):
    mu = jnp.mean(o, axis=-1, keepdims=True)
    var = jnp.mean(jnp.square(o - mu), axis=-1, keepdims=True)
    return (o - mu) * lax.rsqrt(var + HEAD_NORM_EPS)


def _merge_kernel(*refs, with_ctx):
    if with_ctx:
        (of_ref, ob_ref, rg_ref, gr_ref, x0_ref, yl_ref, yc_ref, mg_ref, x_ref, mod_ref,
         wb_ref, wo_ref, out_ref) = refs
    else:
        (of_ref, ob_ref, rg_ref, gr_ref, x0_ref, yl_ref, mg_ref, x_ref, mod_ref,
         wb_ref, wo_ref, out_ref) = refs
    o = of_ref[0] + ob_ref[0]
    mixed = None
    for m, gate_ref in enumerate((rg_ref, gr_ref)):
        heads = [_head_norm(o[:, m * V_W + hd * DV:m * V_W + (hd + 1) * DV]) for hd in range(N_HEADS)]
        br = jnp.concatenate(heads, axis=1) * _silu(gate_ref[0].astype(F32))
        g = _sigmoid(mg_ref[0, :, m * D_MODEL:(m + 1) * D_MODEL].astype(F32))
        term = g * _dot(br.astype(BF16), wb_ref[m])
        mixed = term if mixed is None else mixed + term
    yt = jnp.concatenate([yl_ref[0, :, r * FFT_CB:(r + 1) * FFT_CB, :].reshape(D_HY, FFT_N2)
                          for r in range(TT // FFT_N2)], axis=1)
    if with_ctx:
        yt = jnp.where(pl.program_id(1) == 0, yc_ref[0], yt)
    hy = x0_ref[0].astype(F32) * yt.T
    g = _sigmoid(mg_ref[0, :, 2 * D_MODEL:3 * D_MODEL].astype(F32))
    mixed = mixed + g * _dot(hy.astype(BF16), wb_ref[2])
    mix = _dot(mixed.astype(BF16), wo_ref[...])
    out_ref[0] = x_ref[0] + mod_ref[0, 0, 2:3] * mix


def _merge_call(o_f, o_b, ret, gla, x0, y_lat, y_ctx, mg, x_all, modsel, w_branch, w_out):
    b, s, d = x_all.shape
    with_ctx = y_ctx is not None
    off = 0 if with_ctx else 1
    tile = lambda w, j=0: pl.BlockSpec((1, TT, w), lambda i, t: (i, t + off, j))
    in_specs = [tile(2 * V_W), tile(2 * V_W), tile(V_W, 2), tile(V_W, 2), tile(D_HY),
                pl.BlockSpec((1, D_HY // FFT_CB, TT // FFT_N2 * FFT_CB, FFT_N2),
                             lambda i, t: (i, 0, jnp.maximum(t + off - 1, 0), 0))]
    args = [o_f, o_b, ret, gla, x0, y_lat]
    if with_ctx:
        in_specs.append(pl.BlockSpec((1, D_HY, CTX_LEN), lambda i, t: (i, 0, 0)))
        args.append(y_ctx)
    in_specs += [tile(3 * D_MODEL), tile(d),
                 pl.BlockSpec((1, 1, 6, d), lambda i, t: (i, jnp.minimum(t + off, 1), 0, 0)),
                 _const_spec((3, V_W, d)), _const_spec((d, d))]
    args += [mg, x_all, modsel, w_branch, w_out]
    return pl.pallas_call(
        functools.partial(_merge_kernel, with_ctx=with_ctx),
        grid=(b, s // TT - off),
        in_specs=in_specs,
        out_specs=tile(d),
        out_shape=jax.ShapeDtypeStruct((b, s, d), F32),
        compiler_params=_params(2),
        name="merge_out_proj",
    )(*args)


def _ffn_up_kernel(x_ref, mod_ref, g_ref, w_ref, a_ref, v_ref):
    m = mod_ref[0, 0]
    hb = _modulate(x_ref[0], g_ref[...], m[3:4], m[4:5]).astype(BF16)
    a_ref[0] = _dot(hb, w_ref[:, :D_FF]).astype(BF16)
    v_ref[0] = _dot(hb, w_ref[:, D_FF:]).astype(BF16)


def _ffn_up_call(x_all, modsel, g2, w_up, off):
    b, s, d = x_all.shape
    tile = lambda w: pl.BlockSpec((1, TT, w), lambda i, t: (i, t + off, 0))
    return pl.pallas_call(
        _ffn_up_kernel,
        grid=(b, s // TT - off),
        in_specs=[tile(d),
                  pl.BlockSpec((1, 1, 6, d), lambda i, t: (i, jnp.minimum(t + off, 1), 0, 0)),
                  _const_spec((1, d)), _const_spec((d, 2 * D_FF))],
        out_specs=[tile(D_FF), tile(D_FF)],
        out_shape=[jax.ShapeDtypeStruct((b, s, D_FF), BF16)] * 2,
        compiler_params=_params(2),
        name="ffn_up",
    )(x_all, modsel, g2, w_up)


def _gelu(x):
    return 0.5 * x * (1.0 + lax.erf(x * (2.0 ** -0.5)))


def _ffn_down_kernel(a_ref, ap_ref, an_ref, v_ref, x_ref, mod_ref, cw_ref, cb_ref, wd_ref, fg_ref,
                     out_ref, acc_ref, *, off, nt, final):
    t = pl.program_id(1) + off
    is_ctx = t == 0
    gw = GRID_W
    up_ok = t >= 2
    dn_ok = jnp.logical_and(t >= 1, t <= nt - 2)
    row_w = jnp.where(is_ctx, 0.0, 1.0)
    sub = lax.broadcasted_iota(jnp.int32, (8, LANES), 0)
    n_rows = TT // gw
    first_col = [jnp.logical_and(sub == 0, jnp.logical_or(k == 0, jnp.logical_not(is_ctx))) for k in range(n_rows)]
    last_col = [jnp.logical_and(sub == 7, jnp.logical_or(k == n_rows - 1, jnp.logical_not(is_ctx)))
                for k in range(n_rows)]

    def from_left(x):
        r = pltpu.roll(x, 1, 0)
        parts = []
        for k in range(n_rows):
            parts += [jnp.where(first_col[k], 0.0, r[k * gw:k * gw + 8]), r[k * gw + 8:(k + 1) * gw]]
        return jnp.concatenate(parts, axis=0)

    def from_right(x):
        r = pltpu.roll(x, TT - 1, 0)
        parts = []
        for k in range(n_rows):
            parts += [r[k * gw:(k + 1) * gw - 8], jnp.where(last_col[k], 0.0, r[(k + 1) * gw - 8:(k + 1) * gw])]
        return jnp.concatenate(parts, axis=0)

    def conv_chunk(cs):
        above = jnp.where(up_ok, ap_ref[0, :, cs].astype(F32), 0.0)
        below = jnp.where(dn_ok, an_ref[0, :, cs].astype(F32), 0.0)
        mid = a_ref[0, :, cs].astype(F32)
        rows = (jnp.concatenate([above, mid[:TT - gw]], axis=0), mid, jnp.concatenate([mid[gw:], below], axis=0))
        w = [cw_ref[k:k + 1, cs] * (1.0 if k // 3 == 1 else row_w) for k in range(9)]
        taps = [rows[0] * w[dj] + rows[1] * w[3 + dj] + rows[2] * w[6 + dj] for dj in range(3)]
        acc = cb_ref[:, cs] + taps[1] + from_left(taps[0]) + from_right(taps[2])
        return _gelu(acc).astype(BF16) * v_ref[0, :, cs]

    kb = 2 * LANES
    for j in range(D_FF // kb):
        act = jnp.concatenate([conv_chunk(slice(j * kb + i * LANES, j * kb + (i + 1) * LANES)) for i in range(2)],
                              axis=1)
        part = _dot(act, wd_ref[j * kb:(j + 1) * kb, :])
        if j == 0:
            acc_ref[...] = part
        else:
            acc_ref[...] += part
    y = x_ref[0] + mod_ref[0, 0, 5:6] * acc_ref[...]
    if final:
        y = y * lax.rsqrt(jnp.mean(y * y, axis=-1, keepdims=True) + NORM_EPS) * fg_ref[...]
    out_ref[0] = y


def _ffn_down_call(a, v, x_all, modsel, conv_w, conv_b, w_down, final_g, off, final):
    b, s, d = x_all.shape
    nt = s // TT
    gw = GRID_W
    r = TT // gw
    n_rows = s // gw
    tile = lambda w: pl.BlockSpec((1, TT, w), lambda i, t: (i, t + off, 0))
    if final:
        out_spec = pl.BlockSpec((1, TT, d), lambda i, t: (i, t, 0))
        out_shape = jax.ShapeDtypeStruct((b, s - CTX_LEN, d), F32)
    else:
        out_spec = tile(d)
        out_shape = jax.ShapeDtypeStruct((b, s, d), F32)
    return pl.pallas_call(
        functools.partial(_ffn_down_kernel, off=off, nt=nt, final=final),
        grid=(b, nt - off),
        in_specs=[tile(D_FF),
                  pl.BlockSpec((1, gw, D_FF), lambda i, t: (i, jnp.maximum((t + off) * r - 1, 0), 0)),
                  pl.BlockSpec((1, gw, D_FF), lambda i, t: (i, jnp.minimum((t + off + 1) * r, n_rows - 1), 0)),
                  tile(D_FF), tile(d),
                  pl.BlockSpec((1, 1, 6, d), lambda i, t: (i, jnp.minimum(t + off, 1), 0, 0)),
                  _const_spec((9, D_FF)), _const_spec((1, D_FF)), _const_spec((D_FF, d)),
                  _const_spec((1, d))],
        out_specs=out_spec,
        out_shape=out_shape,
        scratch_shapes=[pltpu.VMEM((TT, d), F32)],
        compiler_params=_params(2),
        name="ffn_down",
    )(a, a, a, v, x_all, modsel, conv_w, conv_b, w_down, final_g)


def kernel(x, c, ctx, c_ctx, ada_w, ada_b, norm1_g, w_in, gla_wa2, gla_ba, hy_short_w, hy_short_b,
           hy_w1, hy_b1, hy_w2, hy_b2, hy_w3, hy_freq, hy_bias, w_branch, w_out, norm2_g, w_up,
           ffn_conv_w, ffn_conv_b, w_down, final_g):
    bsz, seq, d = x.shape
    depth = ada_w.shape[0]
    assert d == D_MODEL and ctx.shape[1] == CTX_LEN == TT and seq % TT == 0
    assert (2 * seq) % (2 * FFT_N2) == 0 and bsz == 2

    x_all = jnp.concatenate([ctx, x], axis=1)
    cvec = jnp.zeros((8, d), F32).at[:bsz].set(c).at[bsz].set(c_ctx)
    mod_all = _ada_call(cvec, ada_w, ada_b).reshape(depth, 8, 6, d)

    out = None
    for l in range(depth):
        last = l == depth - 1
        m = mod_all[l]
        modsel = jnp.stack([jnp.broadcast_to(m[bsz], (bsz, 6, d)), m[:bsz]], axis=1)
        w = w_in[l]
        w_lr, w2 = _gate_weights(w[:, LR_OFF:LR_OFF + 2 * GLA_LOWRANK], gla_wa2[l])
        w_main = jnp.concatenate([w[:, :LR_OFF].astype(BF16), w[:, LR_OFF + 2 * GLA_LOWRANK:].astype(BF16),
                                  w_lr], axis=1)
        ret, gla, hy, mg, lga = _inproj_call(x_all, modsel, norm1_g[l].reshape(1, d), w_main, w2,
                                             gla_ba[l].reshape(1, 2 * QK_W))
        o_f, o_b = _scan_call(ret, gla, lga)

        x0, z_lat, z_ctx = _hy_prep_call(hy, hy_short_w[l], hy_short_b[l].reshape(1, 3 * D_HY))
        filt = (hy_w1[l], hy_b1[l], hy_w2[l], hy_b2[l], hy_w3[l], hy_freq[l])
        y_lat = _fft_conv_call(z_lat, _filter_call(seq, *filt, split=True), hy_bias[l])
        y_ctx = None if last else _dft_conv_call(z_ctx, _filter_call(CTX_LEN, *filt, split=False), hy_bias[l])

        x_all = _merge_call(o_f, o_b, ret, gla, x0, y_lat, y_ctx, mg, x_all, modsel,
                            w_branch[l].astype(BF16), w_out[l].astype(BF16))
        off = 1 if last else 0
        a, v = _ffn_up_call(x_all, modsel, norm2_g[l].reshape(1, d), w_up[l].astype(BF16), off)
        out = _ffn_down_call(a, v, x_all, modsel, ffn_conv_w[l].reshape(9, D_FF),
                             ffn_conv_b[l].reshape(1, D_FF), w_down[l].astype(BF16),
                             final_g.reshape(1, d), off, last)
        x_all = out
    return out
```

```python
import functools
import math

import numpy as np
import jax
import jax.numpy as jnp
from jax import lax
from jax.experimental import pallas as pl
from jax.experimental.pallas import tpu as pltpu

F32 = jnp.float32
BF16 = jnp.bfloat16
HIGHEST = lax.Precision.HIGHEST

D_MODEL = 1024
CTX_LEN = 256
GRID_W = 64
NORM_EPS = 1e-6
HEAD_NORM_EPS = 1e-5

N_HEADS = 4
DK = 64
DV = 128
QK_W = N_HEADS * DK
V_W = N_HEADS * DV
RET_LOG_DECAY = (tuple(math.log1p(-2.0 ** (-5.0 - h)) for h in range(N_HEADS)),
                 tuple(math.log1p(-2.0 ** (-5.5 - h)) for h in range(N_HEADS)))
GLA_LOWRANK = 16
GLA_GATE_NORM = 16.0
GLA_CHUNK = 64

D_HY = 512
HY_BANDS = 16
HY_EMB = 1 + 2 * HY_BANDS
HY_EMB_PAD = 40
HY_FILTER_WIDTH = 64
HY_INNER = 2
HY_MIN_DECAY = math.log(1e-2) / 1.5
HY_MAX_DECAY = math.log(1e-2) / 0.3
HY_SHIFT = 0.05

D_FF = 2816
D_IN_MAIN = 7680
LR_OFF = 3072

TT = 256
LANES = 128
FFT_N2 = 128
FFT_CB = 16
VMEM_LIMIT = 56 * 1024 * 1024


def _dot(a, b, precision=None):
    return jnp.dot(a, b, preferred_element_type=F32, precision=precision)


def _dot_nt(a, b):
    return lax.dot_general(a, b, (((1,), (1,)), ((), ())), preferred_element_type=F32)


def _dot_tn(a, b):
    return lax.dot_general(a, b, (((0,), (0,)), ((), ())), preferred_element_type=F32)


def _params(n_axes):
    return pltpu.CompilerParams(dimension_semantics=("arbitrary",) * n_axes,
                                vmem_limit_bytes=VMEM_LIMIT)


def _const_spec(shape):
    n = len(shape)
    return pl.BlockSpec(shape, lambda *_: (0,) * n, pipeline_mode=pl.Buffered(1))


def _modulate(x, g, shift, scale):
    y = x * lax.rsqrt(jnp.mean(x * x, axis=-1, keepdims=True) + NORM_EPS) * g
    return y * (1.0 + scale) + shift


def _sigmoid(x):
    return 0.5 * jnp.tanh(0.5 * x) + 0.5


def _silu(x):
    return x * _sigmoid(x)


def _ada_kernel(c_ref, w_ref, b_ref, o_ref):
    o_ref[0] = _dot(_silu(c_ref[...]), w_ref[0], HIGHEST) + b_ref[0]


def _ada_call(cvec, ada_w, ada_b):
    depth, d, n = ada_w.shape
    nb = 1536
    return pl.pallas_call(
        _ada_kernel,
        grid=(depth, n // nb),
        in_specs=[pl.BlockSpec((8, d), lambda l, j: (0, 0)),
                  pl.BlockSpec((1, d, nb), lambda l, j: (l, 0, j)),
                  pl.BlockSpec((1, 1, nb), lambda l, j: (l, 0, j))],
        out_specs=pl.BlockSpec((1, 8, nb), lambda l, j: (l, 0, j)),
        out_shape=jax.ShapeDtypeStruct((depth, 8, n), F32),
        compiler_params=_params(2),
        name="ada_ln",
    )(cvec, ada_w, ada_b.reshape(depth, 1, n))


def _log_sigmoid(x):
    return jnp.minimum(x, 0.0) - jnp.log1p(jnp.exp(-jnp.abs(x)))


def _split_bf16(a):
    hi = a.astype(BF16)
    return hi, (a - hi.astype(F32)).astype(BF16)


def _inproj_kernel(x_ref, mod_ref, g_ref, wm_ref, w2_ref, ba_ref,
                   ret_ref, gla_ref, hy_ref, mg_ref, lga_ref):
    m = mod_ref[0, 0]
    h = _modulate(x_ref[0], g_ref[...], m[0:1], m[1:2])
    hb = h.astype(BF16)
    ret_ref[0] = _dot(hb, wm_ref[:, 0:1536]).astype(BF16)
    gla_ref[0] = _dot(hb, wm_ref[:, 1536:3072]).astype(BF16)
    hy_ref[0] = _dot(hb, wm_ref[:, 3072:4608]).astype(BF16)
    mg_ref[0] = _dot(hb, wm_ref[:, 4608:D_IN_MAIN]).astype(BF16)
    lr_hi, lr_lo = _split_bf16(_dot(hb, wm_ref[:, D_IN_MAIN:]))
    gate = _dot(jnp.concatenate([lr_hi, lr_lo, lr_hi], axis=1), w2_ref[...]) + ba_ref[...]
    lga_ref[0] = _log_sigmoid(gate) * (1.0 / GLA_GATE_NORM)


def _gate_weights(w_lr, wa2):
    d = w_lr.shape[0]
    hi, lo = _split_bf16(w_lr)
    w1 = jnp.concatenate([hi, lo, jnp.zeros((d, LANES - 4 * GLA_LOWRANK), BF16)], axis=1)
    bd = jnp.zeros((2 * GLA_LOWRANK, 2 * QK_W), F32)
    bd = bd.at[:GLA_LOWRANK, :QK_W].set(wa2[0]).at[GLA_LOWRANK:, QK_W:].set(wa2[1])
    w2 = jnp.concatenate([bd, bd, jnp.zeros((LANES - 4 * GLA_LOWRANK, 2 * QK_W), F32)], axis=0)
    w2_hi, w2_lo = _split_bf16(w2)
    return w1, jnp.concatenate([w2_hi, w2_hi, w2_lo], axis=0)


def _inproj_call(x_all, modsel, g1, w_main, w2, ba):
    b, s, d = x_all.shape
    nt = s // TT
    tile = lambda w: pl.BlockSpec((1, TT, w), lambda i, t: (i, t, 0))
    bf = lambda w: jax.ShapeDtypeStruct((b, s, w), BF16)
    return pl.pallas_call(
        _inproj_kernel,
        grid=(b, nt),
        in_specs=[tile(d),
                  pl.BlockSpec((1, 1, 6, d), lambda i, t: (i, jnp.minimum(t, 1), 0, 0)),
                  _const_spec((1, d)),
                  _const_spec((d, D_IN_MAIN + LANES)),
                  _const_spec((3 * LANES, 2 * QK_W)),
                  _const_spec((1, 2 * QK_W))],
        out_specs=[tile(1536), tile(1536), tile(1536), tile(3072), tile(2 * QK_W)],
        out_shape=[bf(1536), bf(1536), bf(1536), bf(3072),
                   jax.ShapeDtypeStruct((b, s, 2 * QK_W), F32)],
        compiler_params=_params(2),
        name="in_proj",
    )(x_all, modsel, g1, w_main, w2, ba)


@functools.lru_cache(maxsize=None)
def _scan_tables():
    i = np.arange(TT, dtype=np.float64)
    scale = DK ** -0.5
    diff = i[:, None] - i[None, :]
    same = (i[:, None] // GLA_CHUNK) == (i[None, :] // GLA_CHUNK)
    dmat = np.zeros((2, N_HEADS, TT, TT))
    erow = np.zeros((2, TT, V_W))
    kw = np.zeros((2, TT, QK_W))
    tri = np.zeros((2, TT, 2 * TT))
    for d in range(2):
        for h in range(N_HEADS):
            lg = RET_LOG_DECAY[d][h]
            if d == 0:
                dmat[d, h] = np.where(diff >= 0, np.exp(np.maximum(diff, 0) * lg), 0.0) * scale
                erow[d, :, h * DV:(h + 1) * DV] = np.exp((i + 1) * lg)[:, None]
                kw[d, :, h * DK:(h + 1) * DK] = (np.exp((TT - 1 - i) * lg) * scale)[:, None]
            else:
                dmat[d, h] = np.where(diff < 0, np.exp(np.maximum(-diff, 0) * lg), 0.0) * scale
                erow[d, :, h * DV:(h + 1) * DV] = np.exp((TT - i) * lg)[:, None]
                kw[d, :, h * DK:(h + 1) * DK] = (np.exp(i * lg) * scale)[:, None]
        block = same & ((diff >= 0) if d == 0 else (diff <= 0))
        tri[d] = np.concatenate([block, block], axis=1)
    f = lambda a: a.astype(np.float32)
    return f(dmat), f(erow), f(kw), f(tri)


def _scan_kernel(retf_ref, glaf_ref, lgf_ref, retb_ref, glab_ref, lgb_ref,
                 dmat_ref, erow_ref, kw_ref, tri_ref, of_ref, ob_ref, h_ref, hbd_ref):
    @pl.when(pl.program_id(1) == 0)
    def _():
        h_ref[...] = jnp.zeros_like(h_ref)
        hbd_ref[...] = jnp.zeros_like(hbd_ref)

    c = GLA_CHUNK
    n_chunks = TT // c
    row = lax.broadcasted_iota(jnp.int32, (TT, TT), 0)
    col = lax.broadcasted_iota(jnp.int32, (TT, TT), 1)
    same_chunk = (row // c) == (col // c)
    heads = [(slice(hd * DK, (hd + 1) * DK), slice(hd * DV, (hd + 1) * DV)) for hd in range(N_HEADS)]
    dirs = ((retf_ref, glaf_ref, lgf_ref, of_ref), (retb_ref, glab_ref, lgb_ref, ob_ref))
    for d, (ret_ref, gla_ref, lg_ref, o_ref) in enumerate(dirs):
        q = ret_ref[0, :, 0:QK_W]
        k = ret_ref[0, :, QK_W:2 * QK_W]
        v = ret_ref[0, :, 2 * QK_W:]
        o_inter = _dot(q, hbd_ref[2 * d]) * erow_ref[d]
        s_new = _dot_tn((k.astype(F32) * kw_ref[d]).astype(BF16), v)
        for hd, (ks, vs) in enumerate(heads):
            s = (_dot_nt(q[:, ks], k[:, ks]) * dmat_ref[d, hd]).astype(BF16)
            o_ref[0, :, vs] = _dot(s, v[:, vs]) + o_inter[:, vs]
            h_new = math.exp(TT * RET_LOG_DECAY[d][hd]) * h_ref[d * 8 + hd] + s_new[ks, vs]
            h_ref[d * 8 + hd] = h_new
            hbd_ref[2 * d, ks, vs] = h_new.astype(BF16)

        hi, lo = _split_bf16(lg_ref[0])
        cum = _dot(tri_ref[d], jnp.concatenate([hi, lo], axis=0))
        edge = c - 1 if d == 0 else 0
        tot = jnp.concatenate([jnp.broadcast_to(cum[j * c + edge:j * c + edge + 1], (c, QK_W))
                               for j in range(n_chunks)], axis=0)
        a_tot = jnp.exp(tot)
        k_neg = gla_ref[0, :, QK_W:2 * QK_W].astype(F32) * jnp.exp(-cum)
        q_in = (gla_ref[0, :, 0:QK_W].astype(F32) * jnp.exp(cum) * DK ** -0.5).astype(BF16)
        k_out = (k_neg * a_tot).astype(BF16)
        k_neg = k_neg.astype(BF16)
        v = gla_ref[0, :, 2 * QK_W:]
        mask = jnp.logical_and(same_chunk, row >= col if d == 0 else row < col)
        for hd, (ks, vs) in enumerate(heads):
            s = jnp.where(mask, _dot_nt(q_in[:, ks], k_neg[:, ks]), 0.0).astype(BF16)
            o_ref[0, :, V_W + hd * DV:V_W + (hd + 1) * DV] = _dot(s, v[:, vs])
        for j in (range(n_chunks) if d == 0 else range(n_chunks - 1, -1, -1)):
            rows = slice(j * c, (j + 1) * c)
            o_ref[0, rows, V_W:] += _dot(q_in[rows], hbd_ref[2 * d + 1])
            s_j = _dot_tn(k_out[rows], v[rows])
            a_col = jnp.broadcast_to(a_tot[j * c:j * c + 1], (DV, QK_W)).T
            for hd, (ks, vs) in enumerate(heads):
                h_new = a_col[ks] * h_ref[d * 8 + 4 + hd] + s_j[ks, vs]
                h_ref[d * 8 + 4 + hd] = h_new
                hbd_ref[2 * d + 1, ks, vs] = h_new.astype(BF16)


def _scan_call(ret, gla, lga):
    b, s, _ = ret.shape
    nt = s // TT
    dmat, erow, kw, tri = (jnp.asarray(z) for z in _scan_tables())
    tri = tri.astype(BF16)
    fwd = lambda i, t: (i, t, 0)
    bwd = lambda i, t: (i, jnp.where(t == 0, 0, nt - t), 0)
    bwd_g = lambda i, t: (i, jnp.where(t == 0, 0, nt - t), 1)
    qkv = lambda f: pl.BlockSpec((1, TT, 2 * QK_W + V_W), f)
    return pl.pallas_call(
        _scan_kernel,
        grid=(b, nt),
        in_specs=[qkv(fwd), qkv(fwd), pl.BlockSpec((1, TT, QK_W), fwd),
                  qkv(bwd), qkv(bwd), pl.BlockSpec((1, TT, QK_W), bwd_g),
                  _const_spec(dmat.shape), _const_spec(erow.shape), _const_spec(kw.shape),
                  _const_spec(tri.shape)],
        out_specs=[pl.BlockSpec((1, TT, 2 * V_W), fwd), pl.BlockSpec((1, TT, 2 * V_W), bwd)],
        out_shape=[jax.ShapeDtypeStruct((b, s, 2 * V_W), F32)] * 2,
        scratch_shapes=[pltpu.VMEM((16, DK, DV), F32), pltpu.VMEM((4, QK_W, V_W), BF16)],
        compiler_params=_params(2),
        name="bidir_scan",
    )(ret, gla, lga, ret, gla, lga, dmat, erow, kw, tri)


def _hy_prep_kernel(p_ref, prev_ref, next_ref, w_ref, b_ref, x0_ref, zl_ref, zc_ref, *, nt):
    t = pl.program_id(1)
    has_prev = t >= 2
    has_next = jnp.logical_and(t >= 1, t <= nt - 2)
    ridx = lax.broadcasted_iota(jnp.int32, (TT, LANES), 0)
    u = []
    for part in range(3):
        cols = []
        for j in range(D_HY // LANES):
            cs = slice(part * D_HY + j * LANES, part * D_HY + (j + 1) * LANES)
            p = p_ref[0, :, cs].astype(F32)
            pv = jnp.where(has_prev, prev_ref[0, 15:16, cs].astype(F32), 0.0)
            nx = jnp.where(has_next, next_ref[0, 0:1, cs].astype(F32), 0.0)
            up = jnp.where(ridx == 0, pv, pltpu.roll(p, 1, 0))
            dn = jnp.where(ridx == TT - 1, nx, pltpu.roll(p, TT - 1, 0))
            cols.append(b_ref[:, cs] + up * w_ref[0:1, cs] + p * w_ref[1:2, cs] + dn * w_ref[2:3, cs])
        u.append(cols)
    x0_ref[0] = jnp.concatenate(u[0], axis=1).astype(BF16)
    z = jnp.concatenate([a * c for a, c in zip(u[1], u[2])], axis=1)
    zt = z.T

    @pl.when(t == 0)
    def _():
        zc_ref[0] = zt

    @pl.when(t > 0)
    def _():
        for r in range(TT // FFT_N2):
            zl_ref[0, :, r * FFT_CB:(r + 1) * FFT_CB, :] = _to_conv_rows(zt[:, r * FFT_N2:(r + 1) * FFT_N2])


def _hy_prep_call(hy, short_w, short_b):
    b, s, w = hy.shape
    nt = s // TT
    hb = 16
    nhb = s // hb
    return pl.pallas_call(
        functools.partial(_hy_prep_kernel, nt=nt),
        grid=(b, nt),
        in_specs=[pl.BlockSpec((1, TT, w), lambda i, t: (i, t, 0)),
                  pl.BlockSpec((1, hb, w), lambda i, t: (i, jnp.maximum(t * (TT // hb) - 1, 0), 0)),
                  pl.BlockSpec((1, hb, w), lambda i, t: (i, jnp.minimum((t + 1) * (TT // hb), nhb - 1), 0)),
                  _const_spec((3, w)), _const_spec((1, w))],
        out_specs=[pl.BlockSpec((1, TT, D_HY), lambda i, t: (i, t, 0)),
                   pl.BlockSpec((1, D_HY // FFT_CB, TT // FFT_N2 * FFT_CB, FFT_N2),
                                lambda i, t: (i, 0, jnp.maximum(t - 1, 0), 0)),
                   pl.BlockSpec((1, D_HY, CTX_LEN), lambda i, t: (i, 0, 0))],
        out_shape=[jax.ShapeDtypeStruct((b, s, D_HY), BF16),
                   jax.ShapeDtypeStruct((b, D_HY // FFT_CB, (s - CTX_LEN) // FFT_N2 * FFT_CB, FFT_N2), F32),
                   jax.ShapeDtypeStruct((b, D_HY, CTX_LEN), F32)],
        compiler_params=_params(2),
        name="hyena_gates",
    )(hy, hy, hy, short_w, short_b)


@functools.lru_cache(maxsize=None)
def _filter_positions(length):
    n = np.arange(2 * length)
    m = np.where(n < length, n, 2 * length - n).astype(np.float64)
    m = np.where(n == length, 0.0, m)
    t = m / (length - 1)
    bands = np.linspace(1e-4, HY_BANDS - 1.0, HY_BANDS)
    ang = (2.0 * math.pi / length) * m[None, :] * bands[:, None]
    z = np.zeros((HY_EMB_PAD, 2 * length))
    z[0] = t
    z[1:1 + HY_BANDS] = np.cos(ang)
    z[1 + HY_BANDS:HY_EMB] = -np.sin(ang)
    return z.astype(np.float32)


def _filter_kernel(z_ref, w1_ref, b1_ref, w2_ref, b2_ref, w3_ref, fr_ref, dl_ref, k_ref, *, length, pb, split):
    z = z_ref[...]
    fr = fr_ref[...]
    hdn = jnp.sin(fr * (_dot(w1_ref[...], z, HIGHEST) + b1_ref[...]))
    for i in range(HY_INNER):
        hdn = jnp.sin(fr * (_dot(w2_ref[i], hdn, HIGHEST) + b2_ref[i]))
    h = _dot(w3_ref[0], hdn, HIGHEST)
    window = jnp.exp(-dl_ref[...] * z[0:1]) + HY_SHIFT
    pos = pl.program_id(0) * pb + lax.broadcasted_iota(jnp.int32, (1, pb), 1)
    k = jnp.where(pos == length, 0.0, h * window)
    if split:
        for r in range(pb // FFT_N2):
            k_ref[:, r * FFT_CB:(r + 1) * FFT_CB, :] = _to_conv_rows(k[:, r * FFT_N2:(r + 1) * FFT_N2])
    else:
        k_ref[...] = k


def _filter_call(length, w1, b1, w2, b2, w3, freq, split):
    pb = min(2048, length)
    if split:
        out_spec = pl.BlockSpec((D_HY // FFT_CB, pb // FFT_N2 * FFT_CB, FFT_N2), lambda j: (0, j, 0))
        out_shape = jax.ShapeDtypeStruct((D_HY // FFT_CB, 2 * length // FFT_N2 * FFT_CB, FFT_N2), F32)
    else:
        out_spec = pl.BlockSpec((D_HY, pb), lambda j: (0, j))
        out_shape = jax.ShapeDtypeStruct((D_HY, 2 * length), F32)
    zf = jnp.asarray(_filter_positions(length))
    w1t = jnp.pad(w1.T, ((0, 0), (0, HY_EMB_PAD - HY_EMB)))
    w2t = jnp.swapaxes(w2, 1, 2)
    w3t = w3.T.reshape(2, D_HY, HY_FILTER_WIDTH)
    col = lambda a: a.reshape(a.shape + (1,))
    deltas = np.abs(np.linspace(HY_MIN_DECAY, HY_MAX_DECAY, D_HY)).astype(np.float32)
    fw = HY_FILTER_WIDTH
    return pl.pallas_call(
        functools.partial(_filter_kernel, length=length, pb=pb, split=split),
        grid=(2 * length // pb,),
        in_specs=[pl.BlockSpec((HY_EMB_PAD, pb), lambda j: (0, j)),
                  _const_spec((fw, HY_EMB_PAD)), _const_spec((fw, 1)),
                  _const_spec((HY_INNER, fw, fw)), _const_spec((HY_INNER, fw, 1)),
                  pl.BlockSpec((1, D_HY, fw), lambda j: (jnp.where(j >= length // pb, 1, 0), 0, 0)),
                  _const_spec((fw, 1)), _const_spec((D_HY, 1))],
        out_specs=out_spec,
        out_shape=out_shape,
        compiler_params=_params(1),
        name="hyena_filter",
    )(zf, w1t, col(b1), w2t, col(b2), w3t, col(freq), jnp.asarray(deltas).reshape(D_HY, 1))


def _real_block(w):
    return np.block([[w.real, w.imag], [-w.imag, w.real]])


@functools.lru_cache(maxsize=None)
def _fft_tables(length):
    n = 2 * length
    n2 = FFT_N2
    n1 = n // n2
    f1 = np.arange(n1)
    wa = np.exp(-2j * np.pi * np.outer(f1, np.arange(n1)) / n1)
    stage_a = np.concatenate([wa.real, wa.imag], axis=0)
    tw = np.exp(-2j * np.pi * np.outer(f1, np.arange(n2)) / n)
    wb = np.exp(-2j * np.pi * np.outer(np.arange(n2), np.arange(n2)) / n2)
    wh = wa[:, :n1 // 2]
    a_pair = np.block([[wh.real, -wh.imag], [wh.imag, wh.real]])
    wi = np.conj(wa[:n1 // 2]) / n
    a_inv = np.block([[wi.real, -wi.imag], [wi.imag, wi.real]])
    f = lambda a: a.astype(np.float32)
    return dict(a_pair=f(a_pair), a_k=f(stage_a), tw_re=f(tw.real), tw_im=f(tw.imag),
                b_fwd=f(_real_block(wb)), b_inv=f(_real_block(np.conj(wb))), a_inv=f(a_inv))


def _fft_conv_kernel(bias_ref, z_ref, k_ref, ap_ref, ak_ref, twr_ref, twi_ref, bf_ref, bi_ref, ai_ref,
                     y_ref, sz_ref, sk_ref, sv_ref):
    n1, n2 = twr_ref.shape
    h, cb = n1 // 2, FFT_CB
    z0, z1 = (z_ref.at[b, 0] for b in range(2))
    y0, y1 = (y_ref.at[b, 0] for b in range(2))
    kk = k_ref.at[0]
    twr = twr_ref[...]
    twi = twi_ref[...]

    def twiddled(a):
        ar, ai = a[:n1], a[n1:]
        return jnp.concatenate([ar * twr - ai * twi, ar * twi + ai * twr], axis=1).astype(BF16)

    def stage_a(p, carry):
        pair = (2 * p, 2 * p + 1)
        zc = jnp.concatenate([jnp.concatenate([z0[pl.ds(c, h, stride=cb), :], z1[pl.ds(c, h, stride=cb), :]], axis=0)
                              for c in pair], axis=1)
        kc = jnp.concatenate([kk[pl.ds(c, n1, stride=cb), :] for c in pair], axis=1)
        az = _dot(ap_ref[...], zc.astype(BF16))
        ak = _dot(ak_ref[...], kc.astype(BF16))
        for i, c in enumerate(pair):
            rows = pl.ds(pl.multiple_of(c * n1, n1), n1)
            sz_ref[rows, :] = twiddled(az[:, i * n2:(i + 1) * n2])
            sk_ref[rows, :] = twiddled(ak[:, i * n2:(i + 1) * n2])
        return carry

    lax.fori_loop(0, cb // 2, stage_a, 0, unroll=4)

    gb = 4

    def stage_b(g, carry):
        rows = pl.ds(pl.multiple_of(g * (gb * n1), gb * n1), gb * n1)
        xs = _dot(sz_ref[rows, :], bf_ref[...])
        ks = _dot(sk_ref[rows, :], bf_ref[...])
        xr, xi, kr, ki = xs[:, :n2], xs[:, n2:], ks[:, :n2], ks[:, n2:]
        ys = jnp.concatenate([xr * kr - xi * ki, xr * ki + xi * kr], axis=1).astype(BF16)
        u = _dot(ys, bi_ref[...])
        for i in range(gb):
            ur, ui = u[i * n1:(i + 1) * n1, :n2], u[i * n1:(i + 1) * n1, n2:]
            lanes = slice((i % 2) * n2, (i % 2 + 1) * n2)
            sv_ref[g * (gb // 2) + i // 2, :n1, lanes] = (ur * twr + ui * twi).astype(BF16)
            sv_ref[g * (gb // 2) + i // 2, n1:, lanes] = (ui * twr - ur * twi).astype(BF16)
        return carry

    lax.fori_loop(0, cb // gb, stage_b, 0, unroll=2)

    def stage_c(p, carry):
        y = _dot(ai_ref[...], sv_ref[p])
        for i in range(2):
            c = 2 * p + i
            bias = bias_ref[pl.program_id(0) * cb + c]
            sel = pl.ds(c, h, stride=cb)
            y0[sel, :] = y[:h, i * n2:(i + 1) * n2] + z0[sel, :] * bias
            y1[sel, :] = y[h:, i * n2:(i + 1) * n2] + z1[sel, :] * bias
        return carry

    lax.fori_loop(0, cb // 2, stage_c, 0, unroll=4)


def _to_conv_rows(a):
    return a.reshape(a.shape[0] // FFT_CB, FFT_CB, a.shape[1])


def _fft_conv_call(zq, kq, bias):
    cb = FFT_CB
    b, nblk, rows, n2 = zq.shape
    assert b == 2 and n2 == FFT_N2
    h, ch = rows // cb, nblk * cb
    n1 = 2 * h
    tb = _fft_tables(h * n2)
    bf = lambda name: jnp.asarray(tb[name]).astype(BF16)
    consts = [bf("a_pair"), bf("a_k"), jnp.asarray(tb["tw_re"]), jnp.asarray(tb["tw_im"]),
              bf("b_fwd"), bf("b_inv"), bf("a_inv")]
    return pl.pallas_call(
        _fft_conv_kernel,
        grid=(nblk,),
        in_specs=[pl.BlockSpec(memory_space=pltpu.SMEM),
                  pl.BlockSpec((b, 1, h * cb, n2), lambda j: (0, j, 0, 0)),
                  pl.BlockSpec((1, n1 * cb, n2), lambda j: (j, 0, 0))]
                 + [_const_spec(a.shape) for a in consts],
        out_specs=pl.BlockSpec((b, 1, h * cb, n2), lambda j: (0, j, 0, 0)),
        out_shape=jax.ShapeDtypeStruct(zq.shape, F32),
        scratch_shapes=[pltpu.VMEM((cb * n1, 2 * n2), BF16), pltpu.VMEM((cb * n1, 2 * n2), BF16),
                        pltpu.VMEM((cb // 2, 2 * n1, 2 * n2), BF16)],
        compiler_params=_params(1),
        name="hyena_long_conv",
    )(bias, zq, kq, *consts)


@functools.lru_cache(maxsize=None)
def _dft_tables(length):
    n = 2 * length
    w = np.exp(-2j * np.pi * np.outer(np.arange(n), np.arange(n)) / n)
    fwd = np.concatenate([w.real, w.imag], axis=1)
    inv = np.concatenate([w.real[:, :length], w.imag[:, :length]], axis=0) / n
    return fwd[:length].astype(np.float32), fwd.astype(np.float32), inv.astype(np.float32)


def _dft_conv_kernel(z_ref, k_ref, bias_ref, fz_ref, fk_ref, fi_ref, y_ref):
    n = k_ref.shape[1]
    ks = _dot(k_ref[...], fk_ref[...], HIGHEST)
    kr, ki = ks[:, :n], ks[:, n:]
    for b in range(z_ref.shape[0]):
        z = z_ref[b]
        xs = _dot(z, fz_ref[...], HIGHEST)
        xr, xi = xs[:, :n], xs[:, n:]
        ys = jnp.concatenate([xr * kr - xi * ki, xr * ki + xi * kr], axis=1)
        y_ref[b] = _dot(ys, fi_ref[...], HIGHEST) + z * bias_ref[...]


def _dft_conv_call(zt, kt, bias):
    b, ch, length = zt.shape
    fz, fk, fi = (jnp.asarray(a) for a in _dft_tables(length))
    return pl.pallas_call(
        _dft_conv_kernel,
        out_shape=jax.ShapeDtypeStruct((b, ch, length), F32),
        compiler_params=pltpu.CompilerParams(vmem_limit_bytes=VMEM_LIMIT),
        name="hyena_ctx_conv",
    )(zt, kt, bias.reshape(ch, 1), fz, fk, fi)


def _head_norm(o):
    mu = jnp.mean(o, axis=-1, keepdims=True)
    var = jnp.mean(jnp.square(o - mu), axis=-1, keepdims=True)
    return (o - mu) * lax.rsqrt(var + HEAD_NORM_EPS)


def _merge_kernel(*refs, with_ctx):
    if with_ctx:
        (of_ref, ob_ref, rg_ref, gr_ref, x0_ref, yl_ref, yc_ref, mg_ref, x_ref, mod_ref,
         wb_ref, wo_ref, out_ref) = refs
    else:
        (of_ref, ob_ref, rg_ref, gr_ref, x0_ref, yl_ref, mg_ref, x_ref, mod_ref,
         wb_ref, wo_ref, out_ref) = refs
    o = of_ref[0] + ob_ref[0]
    mixed = None
    for m, gate_ref in enumerate((rg_ref, gr_ref)):
        heads = [_head_norm(o[:, m * V_W + hd * DV:m * V_W + (hd + 1) * DV]) for hd in range(N_HEADS)]
        br = jnp.concatenate(heads, axis=1) * _silu(gate_ref[0].astype(F32))
        g = _sigmoid(mg_ref[0, :, m * D_MODEL:(m + 1) * D_MODEL].astype(F32))
        term = g * _dot(br.astype(BF16), wb_ref[m])
        mixed = term if mixed is None else mixed + term
    yt = jnp.concatenate([yl_ref[0, :, r * FFT_CB:(r + 1) * FFT_CB, :].reshape(D_HY, FFT_N2)
                          for r in range(TT // FFT_N2)], axis=1)
    if with_ctx:
        yt = jnp.where(pl.program_id(1) == 0, yc_ref[0], yt)
    hy = x0_ref[0].astype(F32) * yt.T
    g = _sigmoid(mg_ref[0, :, 2 * D_MODEL:3 * D_MODEL].astype(F32))
    mixed = mixed + g * _dot(hy.astype(BF16), wb_ref[2])
    mix = _dot(mixed.astype(BF16), wo_ref[...])
    out_ref[0] = x_ref[0] + mod_ref[0, 0, 2:3] * mix


def _merge_call(o_f, o_b, ret, gla, x0, y_lat, y_ctx, mg, x_all, modsel, w_branch, w_out):
    b, s, d = x_all.shape
    with_ctx = y_ctx is not None
    off = 0 if with_ctx else 1
    tile = lambda w, j=0: pl.BlockSpec((1, TT, w), lambda i, t: (i, t + off, j))
    in_specs = [tile(2 * V_W), tile(2 * V_W), tile(V_W, 2), tile(V_W, 2), tile(D_HY),
                pl.BlockSpec((1, D_HY // FFT_CB, TT // FFT_N2 * FFT_CB, FFT_N2),
                             lambda i, t: (i, 0, jnp.maximum(t + off - 1, 0), 0))]
    args = [o_f, o_b, ret, gla, x0, y_lat]
    if with_ctx:
        in_specs.append(pl.BlockSpec((1, D_HY, CTX_LEN), lambda i, t: (i, 0, 0)))
        args.append(y_ctx)
    in_specs += [tile(3 * D_MODEL), tile(d),
                 pl.BlockSpec((1, 1, 6, d), lambda i, t: (i, jnp.minimum(t + off, 1), 0, 0)),
                 _const_spec((3, V_W, d)), _const_spec((d, d))]
    args += [mg, x_all, modsel, w_branch, w_out]
    return pl.pallas_call(
        functools.partial(_merge_kernel, with_ctx=with_ctx),
        grid=(b, s // TT - off),
        in_specs=in_specs,
        out_specs=tile(d),
        out_shape=jax.ShapeDtypeStruct((b, s, d), F32),
        compiler_params=_params(2),
        name="merge_out_proj",
    )(*args)


def _gelu(x):
    return 0.5 * x * (1.0 + lax.erf(x * (2.0 ** -0.5)))


def _ffn_kernel(xu_ref, xr_ref, modu_ref, modr_ref, g_ref, wu_ref, cw_ref, cb_ref, wd_ref, fg_ref,
                out_ref, a_ref, v_ref, tail_ref, acc_ref, *, off, nt, final):
    s = pl.program_id(1)
    p = lax.rem(s, 2)
    gw = GRID_W

    @pl.when(s == 0)
    def _():
        a_ref[...] = jnp.zeros_like(a_ref)
        v_ref[...] = jnp.zeros_like(v_ref)

    tail_ref[...] = a_ref[p, TT - gw:, :]
    mu = modu_ref[0, 0]
    hb = _modulate(xu_ref[0], g_ref[...], mu[3:4], mu[4:5]).astype(BF16)
    a_ref[p] = _dot(hb, wu_ref[:, :D_FF]).astype(BF16)
    v_ref[p] = _dot(hb, wu_ref[:, D_FF:]).astype(BF16)

    t = s - 1 + off
    is_ctx = t == 0
    up_ok = t >= 2
    dn_ok = jnp.logical_and(t >= 1, t <= nt - 2)
    row_w = jnp.where(is_ctx, 0.0, 1.0)
    sub = lax.broadcasted_iota(jnp.int32, (8, LANES), 0)
    n_rows = TT // gw
    first_col = [jnp.logical_and(sub == 0, jnp.logical_or(k == 0, jnp.logical_not(is_ctx))) for k in range(n_rows)]
    last_col = [jnp.logical_and(sub == 7, jnp.logical_or(k == n_rows - 1, jnp.logical_not(is_ctx)))
                for k in range(n_rows)]

    def from_left(x):
        r = pltpu.roll(x, 1, 0)
        parts = []
        for k in range(n_rows):
            parts += [jnp.where(first_col[k], 0.0, r[k * gw:k * gw + 8]), r[k * gw + 8:(k + 1) * gw]]
        return jnp.concatenate(parts, axis=0)

    def from_right(x):
        r = pltpu.roll(x, TT - 1, 0)
        parts = []
        for k in range(n_rows):
            parts += [r[k * gw:(k + 1) * gw - 8], jnp.where(last_col[k], 0.0, r[(k + 1) * gw - 8:(k + 1) * gw])]
        return jnp.concatenate(parts, axis=0)

    def conv_chunk(cs):
        above = jnp.where(up_ok, tail_ref[:, cs].astype(F32), 0.0)
        below = jnp.where(dn_ok, a_ref[p, :gw, cs].astype(F32), 0.0)
        mid = a_ref[1 - p, :, cs].astype(F32)
        rows = (jnp.concatenate([above, mid[:TT - gw]], axis=0), mid, jnp.concatenate([mid[gw:], below], axis=0))
        w = [cw_ref[k:k + 1, cs] * (1.0 if k // 3 == 1 else row_w) for k in range(9)]
        taps = [rows[0] * w[dj] + rows[1] * w[3 + dj] + rows[2] * w[6 + dj] for dj in range(3)]
        acc = cb_ref[:, cs] + taps[1] + from_left(taps[0]) + from_right(taps[2])
        return _gelu(acc).astype(BF16) * v_ref[1 - p, :, cs]

    kb = 2 * LANES
    for j in range(D_FF // kb):
        act = jnp.concatenate([conv_chunk(slice(j * kb + i * LANES, j * kb + (i + 1) * LANES)) for i in range(2)],
                              axis=1)
        part = _dot(act, wd_ref[j * kb:(j + 1) * kb, :])
        if j == 0:
            acc_ref[...] = part
        else:
            acc_ref[...] += part
    y = xr_ref[0] + modr_ref[0, 0, 5:6] * acc_ref[...]
    if final:
        y = y * lax.rsqrt(jnp.mean(y * y, axis=-1, keepdims=True) + NORM_EPS) * fg_ref[...]
    out_ref[0] = y


def _ffn_call(x_all, modsel, g2, w_up, conv_w, conv_b, w_down, final_g, off, final):
    b, s, d = x_all.shape
    nt = s // TT
    n = nt - off
    up_tile = lambda i, t: jnp.minimum(t, n - 1) + off
    fin_tile = lambda i, t: jnp.maximum(t - 1, 0) + off
    x_spec = lambda f: pl.BlockSpec((1, TT, d), lambda i, t: (i, f(i, t), 0))
    mod_spec = lambda f: pl.BlockSpec((1, 1, 6, d), lambda i, t: (i, jnp.minimum(f(i, t), 1), 0, 0))
    if final:
        out_spec = pl.BlockSpec((1, TT, d), lambda i, t: (i, jnp.maximum(t - 1, 0), 0))
        out_shape = jax.ShapeDtypeStruct((b, s - off * TT, d), F32)
    else:
        out_spec = x_spec(fin_tile)
        out_shape = jax.ShapeDtypeStruct((b, s, d), F32)
    return pl.pallas_call(
        functools.partial(_ffn_kernel, off=off, nt=nt, final=final),
        grid=(b, n + 1),
        in_specs=[x_spec(up_tile), x_spec(fin_tile), mod_spec(up_tile), mod_spec(fin_tile),
                  _const_spec((1, d)), _const_spec((d, 2 * D_FF)),
                  _const_spec((9, D_FF)), _const_spec((1, D_FF)), _const_spec((D_FF, d)),
                  _const_spec((1, d))],
        out_specs=out_spec,
        out_shape=out_shape,
        scratch_shapes=[pltpu.VMEM((2, TT, D_FF), BF16), pltpu.VMEM((2, TT, D_FF), BF16),
                        pltpu.VMEM((GRID_W, D_FF), BF16), pltpu.VMEM((TT, d), F32)],
        compiler_params=_params(2),
        name="conv_glu",
    )(x_all, x_all, modsel, modsel, g2, w_up, conv_w, conv_b, w_down, final_g)


def kernel(x, c, ctx, c_ctx, ada_w, ada_b, norm1_g, w_in, gla_wa2, gla_ba, hy_short_w, hy_short_b,
           hy_w1, hy_b1, hy_w2, hy_b2, hy_w3, hy_freq, hy_bias, w_branch, w_out, norm2_g, w_up,
           ffn_conv_w, ffn_conv_b, w_down, final_g):
    bsz, seq, d = x.shape
    depth = ada_w.shape[0]
    assert d == D_MODEL and ctx.shape[1] == CTX_LEN == TT and seq % TT == 0
    assert (2 * seq) % (2 * FFT_N2) == 0 and bsz == 2

    x_all = jnp.concatenate([ctx, x], axis=1)
    cvec = jnp.zeros((8, d), F32).at[:bsz].set(c).at[bsz].set(c_ctx)
    mod_all = _ada_call(cvec, ada_w, ada_b).reshape(depth, 8, 6, d)

    out = None
    for l in range(depth):
        last = l == depth - 1
        m = mod_all[l]
        modsel = jnp.stack([jnp.broadcast_to(m[bsz], (bsz, 6, d)), m[:bsz]], axis=1)
        w = w_in[l]
        w_lr, w2 = _gate_weights(w[:, LR_OFF:LR_OFF + 2 * GLA_LOWRANK], gla_wa2[l])
        w_main = jnp.concatenate([w[:, :LR_OFF].astype(BF16), w[:, LR_OFF + 2 * GLA_LOWRANK:].astype(BF16),
                                  w_lr], axis=1)
        ret, gla, hy, mg, lga = _inproj_call(x_all, modsel, norm1_g[l].reshape(1, d), w_main, w2,
                                             gla_ba[l].reshape(1, 2 * QK_W))
        o_f, o_b = _scan_call(ret, gla, lga)

        x0, z_lat, z_ctx = _hy_prep_call(hy, hy_short_w[l], hy_short_b[l].reshape(1, 3 * D_HY))
        filt = (hy_w1[l], hy_b1[l], hy_w2[l], hy_b2[l], hy_w3[l], hy_freq[l])
        y_lat = _fft_conv_call(z_lat, _filter_call(seq, *filt, split=True), hy_bias[l])
        y_ctx = None if last else _dft_conv_call(z_ctx, _filter_call(CTX_LEN, *filt, split=False), hy_bias[l])

        x_all = _merge_call(o_f, o_b, ret, gla, x0, y_lat, y_ctx, mg, x_all, modsel,
                            w_branch[l].astype(BF16), w_out[l].astype(BF16))
        off = 1 if last else 0
        out = _ffn_call(x_all, modsel, norm2_g[l].reshape(1, d), w_up[l].astype(BF16),
                        ffn_conv_w[l].reshape(9, D_FF), ffn_conv_b[l].reshape(1, D_FF),
                        w_down[l].astype(BF16), final_g.reshape(1, d), off, last)
        x_all = out
    return out
```

```python
import functools
import math

import numpy as np
import jax
import jax.numpy as jnp
from jax import lax
from jax.experimental import pallas as pl
from jax.experimental.pallas import tpu as pltpu

F32 = jnp.float32
BF16 = jnp.bfloat16
HIGHEST = lax.Precision.HIGHEST

D_MODEL = 1024
CTX_LEN = 256
GRID_W = 64
NORM_EPS = 1e-6
HEAD_NORM_EPS = 1e-5

N_HEADS = 4
DK = 64
DV = 128
QK_W = N_HEADS * DK
V_W = N_HEADS * DV
RET_LOG_DECAY = (tuple(math.log1p(-2.0 ** (-5.0 - h)) for h in range(N_HEADS)),
                 tuple(math.log1p(-2.0 ** (-5.5 - h)) for h in range(N_HEADS)))
GLA_LOWRANK = 16
GLA_GATE_NORM = 16.0
GLA_CHUNK = 64

D_HY = 512
HY_BANDS = 16
HY_EMB = 1 + 2 * HY_BANDS
HY_EMB_PAD = 40
HY_FILTER_WIDTH = 64
HY_INNER = 2
HY_MIN_DECAY = math.log(1e-2) / 1.5
HY_MAX_DECAY = math.log(1e-2) / 0.3
HY_SHIFT = 0.05

D_FF = 2816
D_IN_MAIN = 7680
LR_OFF = 3072

TT = 256
LANES = 128
FFT_N2 = 128
FFT_CB = 16
VMEM_LIMIT = 56 * 1024 * 1024


def _dot(a, b, precision=None):
    return jnp.dot(a, b, preferred_element_type=F32, precision=precision)


def _dot_nt(a, b):
    return lax.dot_general(a, b, (((1,), (1,)), ((), ())), preferred_element_type=F32)


def _dot_tn(a, b):
    return lax.dot_general(a, b, (((0,), (0,)), ((), ())), preferred_element_type=F32)


def _params(n_axes):
    return pltpu.CompilerParams(dimension_semantics=("arbitrary",) * n_axes,
                                vmem_limit_bytes=VMEM_LIMIT)


def _const_spec(shape):
    n = len(shape)
    return pl.BlockSpec(shape, lambda *_: (0,) * n, pipeline_mode=pl.Buffered(1))


def _modulate(x, g, shift, scale):
    y = x * lax.rsqrt(jnp.mean(x * x, axis=-1, keepdims=True) + NORM_EPS) * g
    return y * (1.0 + scale) + shift


def _sigmoid(x):
    return 0.5 * jnp.tanh(0.5 * x) + 0.5


def _silu(x):
    return x * _sigmoid(x)


def _ada_kernel(c_ref, w_ref, b_ref, o_ref):
    o_ref[0] = _dot(_silu(c_ref[...]), w_ref[0], HIGHEST) + b_ref[0]


def _ada_call(cvec, ada_w, ada_b):
    depth, d, n = ada_w.shape
    nb = 1536
    return pl.pallas_call(
        _ada_kernel,
        grid=(depth, n // nb),
        in_specs=[pl.BlockSpec((8, d), lambda l, j: (0, 0)),
                  pl.BlockSpec((1, d, nb), lambda l, j: (l, 0, j)),
                  pl.BlockSpec((1, 1, nb), lambda l, j: (l, 0, j))],
        out_specs=pl.BlockSpec((1, 8, nb), lambda l, j: (l, 0, j)),
        out_shape=jax.ShapeDtypeStruct((depth, 8, n), F32),
        compiler_params=_params(2),
        name="ada_ln",
    )(cvec, ada_w, ada_b.reshape(depth, 1, n))


def _log_sigmoid(x):
    return jnp.minimum(x, 0.0) - jnp.log1p(jnp.exp(-jnp.abs(x)))


def _split_bf16(a):
    hi = a.astype(BF16)
    return hi, (a - hi.astype(F32)).astype(BF16)


def _inproj_kernel(*refs, nt, split):
    if split:
        ctx_ref, refs = refs[0], refs[1:]
    (x_ref, mod_ref, g_ref, wa_ref, wb_ref, wl_ref, w2_ref, ba_ref, sw_ref, sb_ref,
     ret_ref, gla_ref, mg_ref, lga_ref, x0_ref, zl_ref, zc_ref, cur_ref, prev_ref) = refs
    s = pl.program_id(1)

    @pl.when(s == 0)
    def _():
        cur_ref[...] = jnp.zeros_like(cur_ref)
        prev_ref[...] = jnp.zeros_like(prev_ref)

    x = x_ref[0]
    if split:
        x = jnp.where(s == 0, ctx_ref[0], x)
    m = mod_ref[0, 0]
    hb = _modulate(x, g_ref[...], m[0:1], m[1:2]).astype(BF16)
    hy_new = _dot(hb, wb_ref[:, :3 * D_HY])
    zt = _hyena_gates(s - 1, nt, cur_ref, prev_ref, hy_new[0:1], sw_ref, sb_ref, x0_ref)
    prev_ref[...] = cur_ref[TT - 8:, :]
    cur_ref[...] = hy_new
    ret_ref[0] = _dot(hb, wa_ref[:, :1536]).astype(BF16)
    gla_ref[0] = _dot(hb, wa_ref[:, 1536:]).astype(BF16)
    mg_ref[0] = _dot(hb, wb_ref[:, 3 * D_HY:]).astype(BF16)
    lr_hi, lr_lo = _split_bf16(_dot(hb, wl_ref[...]))
    gate = _dot(jnp.concatenate([lr_hi, lr_lo, lr_hi], axis=1), w2_ref[...]) + ba_ref[...]
    lga_ref[0] = _log_sigmoid(gate) * (1.0 / GLA_GATE_NORM)
    _store_z(s - 1, zt, zl_ref, zc_ref)


def _gate_weights(w_lr, wa2):
    d = w_lr.shape[0]
    hi, lo = _split_bf16(w_lr)
    w1 = jnp.concatenate([hi, lo, jnp.zeros((d, LANES - 4 * GLA_LOWRANK), BF16)], axis=1)
    bd = jnp.zeros((2 * GLA_LOWRANK, 2 * QK_W), F32)
    bd = bd.at[:GLA_LOWRANK, :QK_W].set(wa2[0]).at[GLA_LOWRANK:, QK_W:].set(wa2[1])
    w2 = jnp.concatenate([bd, bd, jnp.zeros((LANES - 4 * GLA_LOWRANK, 2 * QK_W), F32)], axis=0)
    w2_hi, w2_lo = _split_bf16(w2)
    return w1, jnp.concatenate([w2_hi, w2_hi, w2_lo], axis=0)


def _inproj_call(seq_in, modsel, g1, w_a, w_b, w_l, w2, ba, short_w, short_b):
    split = isinstance(seq_in, tuple)
    if split:
        ctx, x = seq_in
        b, d = x.shape[0], x.shape[2]
        s = ctx.shape[1] + x.shape[1]
    else:
        b, s, d = seq_in.shape
    nt = s // TT
    cur = lambda t: jnp.minimum(t, nt - 1)
    tile = lambda w: pl.BlockSpec((1, TT, w), lambda i, t: (i, cur(t), 0))
    bf = lambda w: jax.ShapeDtypeStruct((b, s, w), BF16)
    if split:
        x_specs = [pl.BlockSpec((1, TT, d), lambda i, t: (i, 0, 0)),
                   pl.BlockSpec((1, TT, d), lambda i, t: (i, jnp.maximum(cur(t) - 1, 0), 0))]
        x_args = [ctx, x]
    else:
        x_specs, x_args = [tile(d)], [seq_in]
    return pl.pallas_call(
        functools.partial(_inproj_kernel, nt=nt, split=split),
        grid=(b, nt + 1),
        in_specs=x_specs + [
            pl.BlockSpec((1, 1, 6, d), lambda i, t: (i, jnp.minimum(cur(t), 1), 0, 0)),
            _const_spec((1, d)),
            _const_spec(w_a.shape), _const_spec(w_b.shape), _const_spec(w_l.shape),
            _const_spec((3 * LANES, 2 * QK_W)), _const_spec((1, 2 * QK_W)),
            _const_spec((3, 3 * D_HY)), _const_spec((1, 3 * D_HY))],
        out_specs=[tile(1536), tile(1536), tile(3072), tile(2 * QK_W),
                   pl.BlockSpec((1, TT, D_HY), lambda i, t: (i, jnp.maximum(t - 1, 0), 0)),
                   pl.BlockSpec((1, D_HY // FFT_CB, TT // FFT_N2 * FFT_CB, FFT_N2),
                                lambda i, t: (i, 0, jnp.maximum(t - 2, 0), 0)),
                   pl.BlockSpec((1, D_HY, CTX_LEN), lambda i, t: (i, 0, 0))],
        out_shape=[bf(1536), bf(1536), bf(3072), jax.ShapeDtypeStruct((b, s, 2 * QK_W), F32),
                   bf(D_HY),
                   jax.ShapeDtypeStruct((b, D_HY // FFT_CB, (s - CTX_LEN) // FFT_N2 * FFT_CB, FFT_N2), F32),
                   jax.ShapeDtypeStruct((b, D_HY, CTX_LEN), F32)],
        scratch_shapes=[pltpu.VMEM((TT, 3 * D_HY), F32), pltpu.VMEM((8, 3 * D_HY), F32)],
        compiler_params=_params(2),
        name="in_proj",
    )(*x_args, modsel, g1, w_a, w_b, w_l, w2, ba, short_w, short_b)


@functools.lru_cache(maxsize=None)
def _scan_tables():
    i = np.arange(TT, dtype=np.float64)
    scale = DK ** -0.5
    diff = i[:, None] - i[None, :]
    same = (i[:, None] // GLA_CHUNK) == (i[None, :] // GLA_CHUNK)
    dmat = np.zeros((2, N_HEADS, TT, TT))
    erow = np.zeros((2, TT, V_W))
    kw = np.zeros((2, TT, QK_W))
    tri = np.zeros((2, TT, 2 * TT))
    for d in range(2):
        for h in range(N_HEADS):
            lg = RET_LOG_DECAY[d][h]
            if d == 0:
                dmat[d, h] = np.where(diff >= 0, np.exp(np.maximum(diff, 0) * lg), 0.0) * scale
                erow[d, :, h * DV:(h + 1) * DV] = np.exp((i + 1) * lg)[:, None]
                kw[d, :, h * DK:(h + 1) * DK] = (np.exp((TT - 1 - i) * lg) * scale)[:, None]
            else:
                dmat[d, h] = np.where(diff < 0, np.exp(np.maximum(-diff, 0) * lg), 0.0) * scale
                erow[d, :, h * DV:(h + 1) * DV] = np.exp((TT - i) * lg)[:, None]
                kw[d, :, h * DK:(h + 1) * DK] = (np.exp(i * lg) * scale)[:, None]
        block = same & ((diff >= 0) if d == 0 else (diff <= 0))
        tri[d] = np.concatenate([block, block], axis=1)
    f = lambda a: a.astype(np.float32)
    return f(dmat), f(erow), f(kw), f(tri)


def _scan_kernel(retf_ref, glaf_ref, lgf_ref, retb_ref, glab_ref, lgb_ref,
                 dmat_ref, erow_ref, kw_ref, tri_ref, of_ref, ob_ref, h_ref, hbd_ref, og_ref):
    @pl.when(pl.program_id(1) == 0)
    def _():
        h_ref[...] = jnp.zeros_like(h_ref)
        hbd_ref[...] = jnp.zeros_like(hbd_ref)

    c = GLA_CHUNK
    n_chunks = TT // c
    row = lax.broadcasted_iota(jnp.int32, (TT, TT), 0)
    col = lax.broadcasted_iota(jnp.int32, (TT, TT), 1)
    same_chunk = (row // c) == (col // c)
    heads = [(slice(hd * DK, (hd + 1) * DK), slice(hd * DV, (hd + 1) * DV)) for hd in range(N_HEADS)]
    dirs = ((retf_ref, glaf_ref, lgf_ref, of_ref), (retb_ref, glab_ref, lgb_ref, ob_ref))
    for d, (ret_ref, gla_ref, lg_ref, o_ref) in enumerate(dirs):
        q = ret_ref[0, :, 0:QK_W]
        k = ret_ref[0, :, QK_W:2 * QK_W]
        v = ret_ref[0, :, 2 * QK_W:]
        o_inter = _dot(q, hbd_ref[2 * d]) * erow_ref[d]
        s_new = _dot_tn((k.astype(F32) * kw_ref[d]).astype(BF16), v)
        for hd, (ks, vs) in enumerate(heads):
            s = (_dot_nt(q[:, ks], k[:, ks]) * dmat_ref[d, hd]).astype(BF16)
            o_ref[0, :, vs] = (_dot(s, v[:, vs]) + o_inter[:, vs]).astype(BF16)
            h_new = math.exp(TT * RET_LOG_DECAY[d][hd]) * h_ref[d * 8 + hd] + s_new[ks, vs]
            h_ref[d * 8 + hd] = h_new
            hbd_ref[2 * d, ks, vs] = h_new.astype(BF16)

        hi, lo = _split_bf16(lg_ref[0])
        cum = _dot(tri_ref[d], jnp.concatenate([hi, lo], axis=0))
        edge = c - 1 if d == 0 else 0
        tot = jnp.concatenate([jnp.broadcast_to(cum[j * c + edge:j * c + edge + 1], (c, QK_W))
                               for j in range(n_chunks)], axis=0)
        a_tot = jnp.exp(tot)
        k_neg = gla_ref[0, :, QK_W:2 * QK_W].astype(F32) * jnp.exp(-cum)
        q_in = (gla_ref[0, :, 0:QK_W].astype(F32) * jnp.exp(cum) * DK ** -0.5).astype(BF16)
        k_out = (k_neg * a_tot).astype(BF16)
        k_neg = k_neg.astype(BF16)
        v = gla_ref[0, :, 2 * QK_W:]
        mask = jnp.logical_and(same_chunk, row >= col if d == 0 else row < col)
        for hd, (ks, vs) in enumerate(heads):
            s = jnp.where(mask, _dot_nt(q_in[:, ks], k_neg[:, ks]), 0.0).astype(BF16)
            og_ref[:, vs] = _dot(s, v[:, vs])
        for j in (range(n_chunks) if d == 0 else range(n_chunks - 1, -1, -1)):
            rows = slice(j * c, (j + 1) * c)
            inter = _dot(q_in[rows], hbd_ref[2 * d + 1])
            o_ref[0, rows, V_W:] = (og_ref[rows, :] + inter).astype(BF16)
            s_j = _dot_tn(k_out[rows], v[rows])
            a_col = jnp.broadcast_to(a_tot[j * c:j * c + 1], (DV, QK_W)).T
            for hd, (ks, vs) in enumerate(heads):
                h_new = a_col[ks] * h_ref[d * 8 + 4 + hd] + s_j[ks, vs]
                h_ref[d * 8 + 4 + hd] = h_new
                hbd_ref[2 * d + 1, ks, vs] = h_new.astype(BF16)


def _scan_call(ret, gla, lga):
    b, s, _ = ret.shape
    nt = s // TT
    dmat, erow, kw, tri = (jnp.asarray(z) for z in _scan_tables())
    tri = tri.astype(BF16)
    fwd = lambda i, t: (i, t, 0)
    bwd = lambda i, t: (i, jnp.where(t == 0, 0, nt - t), 0)
    bwd_g = lambda i, t: (i, jnp.where(t == 0, 0, nt - t), 1)
    qkv = lambda f: pl.BlockSpec((1, TT, 2 * QK_W + V_W), f)
    return pl.pallas_call(
        _scan_kernel,
        grid=(b, nt),
        in_specs=[qkv(fwd), qkv(fwd), pl.BlockSpec((1, TT, QK_W), fwd),
                  qkv(bwd), qkv(bwd), pl.BlockSpec((1, TT, QK_W), bwd_g),
                  _const_spec(dmat.shape), _const_spec(erow.shape), _const_spec(kw.shape),
                  _const_spec(tri.shape)],
        out_specs=[pl.BlockSpec((1, TT, 2 * V_W), fwd), pl.BlockSpec((1, TT, 2 * V_W), bwd)],
        out_shape=[jax.ShapeDtypeStruct((b, s, 2 * V_W), BF16)] * 2,
        scratch_shapes=[pltpu.VMEM((16, DK, DV), F32), pltpu.VMEM((4, QK_W, V_W), BF16),
                        pltpu.VMEM((TT, V_W), F32)],
        compiler_params=_params(2),
        name="bidir_scan",
    )(ret, gla, lga, ret, gla, lga, dmat, erow, kw, tri)


def _hyena_gates(t, nt, cur_ref, prev_ref, next_row, w_ref, b_ref, x0_ref):
    has_prev = t >= 2
    has_next = jnp.logical_and(t >= 1, t <= nt - 2)
    ridx = lax.broadcasted_iota(jnp.int32, (TT, LANES), 0)
    u = []
    for part in range(3):
        cols = []
        for j in range(D_HY // LANES):
            cs = slice(part * D_HY + j * LANES, part * D_HY + (j + 1) * LANES)
            p = cur_ref[:, cs]
            pv = jnp.where(has_prev, prev_ref[7:8, cs], 0.0)
            nx = jnp.where(has_next, next_row[:, cs], 0.0)
            up = jnp.where(ridx == 0, pv, pltpu.roll(p, 1, 0))
            dn = jnp.where(ridx == TT - 1, nx, pltpu.roll(p, TT - 1, 0))
            cols.append(b_ref[:, cs] + up * w_ref[0:1, cs] + p * w_ref[1:2, cs] + dn * w_ref[2:3, cs])
        u.append(cols)
    x0_ref[0] = jnp.concatenate(u[0], axis=1).astype(BF16)
    z = jnp.concatenate([a * c for a, c in zip(u[1], u[2])], axis=1)
    return z.T


def _store_z(t, zt, zl_ref, zc_ref):
    @pl.when(t == 0)
    def _():
        zc_ref[0] = zt

    @pl.when(t > 0)
    def _():
        for r in range(TT // FFT_N2):
            zl_ref[0, :, r * FFT_CB:(r + 1) * FFT_CB, :] = _to_conv_rows(zt[:, r * FFT_N2:(r + 1) * FFT_N2])


@functools.lru_cache(maxsize=None)
def _filter_positions(length):
    n = np.arange(2 * length)
    m = np.where(n < length, n, 2 * length - n).astype(np.float64)
    m = np.where(n == length, 0.0, m)
    t = m / (length - 1)
    bands = np.linspace(1e-4, HY_BANDS - 1.0, HY_BANDS)
    ang = (2.0 * math.pi / length) * m[None, :] * bands[:, None]
    z = np.zeros((HY_EMB_PAD, 2 * length))
    z[0] = t
    z[1:1 + HY_BANDS] = np.cos(ang)
    z[1 + HY_BANDS:HY_EMB] = -np.sin(ang)
    return z.astype(np.float32)


def _filter_kernel(z_ref, w1_ref, b1_ref, w2_ref, b2_ref, w3_ref, fr_ref, dl_ref, k_ref, *, length, pb, split):
    z = z_ref[...]
    fr = fr_ref[...]
    hdn = jnp.sin(fr * (_dot(w1_ref[...], z, HIGHEST) + b1_ref[...]))
    for i in range(HY_INNER):
        hdn = jnp.sin(fr * (_dot(w2_ref[i], hdn, HIGHEST) + b2_ref[i]))
    h = _dot(w3_ref[0], hdn, HIGHEST)
    window = jnp.exp(-dl_ref[...] * z[0:1]) + HY_SHIFT
    pos = pl.program_id(0) * pb + lax.broadcasted_iota(jnp.int32, (1, pb), 1)
    k = jnp.where(pos == length, 0.0, h * window)
    if split:
        for r in range(pb // FFT_N2):
            k_ref[:, r * FFT_CB:(r + 1) * FFT_CB, :] = _to_conv_rows(k[:, r * FFT_N2:(r + 1) * FFT_N2])
    else:
        k_ref[...] = k


def _filter_call(length, w1, b1, w2, b2, w3, freq, split):
    pb = min(2048, length)
    if split:
        out_spec = pl.BlockSpec((D_HY // FFT_CB, pb // FFT_N2 * FFT_CB, FFT_N2), lambda j: (0, j, 0))
        out_shape = jax.ShapeDtypeStruct((D_HY // FFT_CB, 2 * length // FFT_N2 * FFT_CB, FFT_N2), F32)
    else:
        out_spec = pl.BlockSpec((D_HY, pb), lambda j: (0, j))
        out_shape = jax.ShapeDtypeStruct((D_HY, 2 * length), F32)
    zf = jnp.asarray(_filter_positions(length))
    w1t = jnp.pad(w1.T, ((0, 0), (0, HY_EMB_PAD - HY_EMB)))
    w2t = jnp.swapaxes(w2, 1, 2)
    w3t = w3.T.reshape(2, D_HY, HY_FILTER_WIDTH)
    col = lambda a: a.reshape(a.shape + (1,))
    deltas = np.abs(np.linspace(HY_MIN_DECAY, HY_MAX_DECAY, D_HY)).astype(np.float32)
    fw = HY_FILTER_WIDTH
    return pl.pallas_call(
        functools.partial(_filter_kernel, length=length, pb=pb, split=split),
        grid=(2 * length // pb,),
        in_specs=[pl.BlockSpec((HY_EMB_PAD, pb), lambda j: (0, j)),
                  _const_spec((fw, HY_EMB_PAD)), _const_spec((fw, 1)),
                  _const_spec((HY_INNER, fw, fw)), _const_spec((HY_INNER, fw, 1)),
                  pl.BlockSpec((1, D_HY, fw), lambda j: (jnp.where(j >= length // pb, 1, 0), 0, 0)),
                  _const_spec((fw, 1)), _const_spec((D_HY, 1))],
        out_specs=out_spec,
        out_shape=out_shape,
        compiler_params=_params(1),
        name="hyena_filter",
    )(zf, w1t, col(b1), w2t, col(b2), w3t, col(freq), jnp.asarray(deltas).reshape(D_HY, 1))


def _real_block(w):
    return np.block([[w.real, w.imag], [-w.imag, w.real]])


@functools.lru_cache(maxsize=None)
def _fft_tables(length):
    n = 2 * length
    n2 = FFT_N2
    n1 = n // n2
    f1 = np.arange(n1)
    wa = np.exp(-2j * np.pi * np.outer(f1, np.arange(n1)) / n1)
    stage_a = np.concatenate([wa.real, wa.imag], axis=0)
    tw = np.exp(-2j * np.pi * np.outer(f1, np.arange(n2)) / n)
    wb = np.exp(-2j * np.pi * np.outer(np.arange(n2), np.arange(n2)) / n2)
    wh = wa[:, :n1 // 2]
    a_pair = np.block([[wh.real, -wh.imag], [wh.imag, wh.real]])
    wi = np.conj(wa[:n1 // 2]) / n
    a_inv = np.block([[wi.real, -wi.imag], [wi.imag, wi.real]])
    f = lambda a: a.astype(np.float32)
    return dict(a_pair=f(a_pair), a_k=f(stage_a), tw_re=f(tw.real), tw_im=f(tw.imag),
                b_fwd=f(_real_block(wb)), b_inv=f(_real_block(np.conj(wb))), a_inv=f(a_inv))


def _fft_conv_kernel(bias_ref, z_ref, k_ref, ap_ref, ak_ref, twr_ref, twi_ref, bf_ref, bi_ref, ai_ref,
                     y_ref, sz_ref, sk_ref, sv_ref):
    n1, n2 = twr_ref.shape
    h, cb = n1 // 2, FFT_CB
    z0, z1 = (z_ref.at[b, 0] for b in range(2))
    y0, y1 = (y_ref.at[b, 0] for b in range(2))
    kk = k_ref.at[0]
    twr = twr_ref[...]
    twi = twi_ref[...]

    def twiddled(a):
        ar, ai = a[:n1], a[n1:]
        return jnp.concatenate([ar * twr - ai * twi, ar * twi + ai * twr], axis=1).astype(BF16)

    def stage_a(p, carry):
        pair = (2 * p, 2 * p + 1)
        zc = jnp.concatenate([jnp.concatenate([z0[pl.ds(c, h, stride=cb), :], z1[pl.ds(c, h, stride=cb), :]], axis=0)
                              for c in pair], axis=1)
        kc = jnp.concatenate([kk[pl.ds(c, n1, stride=cb), :] for c in pair], axis=1)
        az = _dot(ap_ref[...], zc.astype(BF16))
        ak = _dot(ak_ref[...], kc.astype(BF16))
        for i, c in enumerate(pair):
            rows = pl.ds(pl.multiple_of(c * n1, n1), n1)
            sz_ref[rows, :] = twiddled(az[:, i * n2:(i + 1) * n2])
            sk_ref[rows, :] = twiddled(ak[:, i * n2:(i + 1) * n2])
        return carry

    lax.fori_loop(0, cb // 2, stage_a, 0, unroll=4)

    gb = 4

    def stage_b(g, carry):
        rows = pl.ds(pl.multiple_of(g * (gb * n1), gb * n1), gb * n1)
        xs = _dot(sz_ref[rows, :], bf_ref[...])
        ks = _dot(sk_ref[rows, :], bf_ref[...])
        xr, xi, kr, ki = xs[:, :n2], xs[:, n2:], ks[:, :n2], ks[:, n2:]
        ys = jnp.concatenate([xr * kr - xi * ki, xr * ki + xi * kr], axis=1).astype(BF16)
        u = _dot(ys, bi_ref[...])
        for i in range(gb):
            ur, ui = u[i * n1:(i + 1) * n1, :n2], u[i * n1:(i + 1) * n1, n2:]
            lanes = slice((i % 2) * n2, (i % 2 + 1) * n2)
            sv_ref[g * (gb // 2) + i // 2, :n1, lanes] = (ur * twr + ui * twi).astype(BF16)
            sv_ref[g * (gb // 2) + i // 2, n1:, lanes] = (ui * twr - ur * twi).astype(BF16)
        return carry

    lax.fori_loop(0, cb // gb, stage_b, 0, unroll=2)

    def stage_c(p, carry):
        y = _dot(ai_ref[...], sv_ref[p])
        for i in range(2):
            c = 2 * p + i
            bias = bias_ref[pl.program_id(0) * cb + c]
            sel = pl.ds(c, h, stride=cb)
            y0[sel, :] = y[:h, i * n2:(i + 1) * n2] + z0[sel, :] * bias
            y1[sel, :] = y[h:, i * n2:(i + 1) * n2] + z1[sel, :] * bias
        return carry

    lax.fori_loop(0, cb // 2, stage_c, 0, unroll=4)


def _to_conv_rows(a):
    return a.reshape(a.shape[0] // FFT_CB, FFT_CB, a.shape[1])


def _fft_conv_call(zq, kq, bias):
    cb = FFT_CB
    b, nblk, rows, n2 = zq.shape
    assert b == 2 and n2 == FFT_N2
    h, ch = rows // cb, nblk * cb
    n1 = 2 * h
    tb = _fft_tables(h * n2)
    bf = lambda name: jnp.asarray(tb[name]).astype(BF16)
    consts = [bf("a_pair"), bf("a_k"), jnp.asarray(tb["tw_re"]), jnp.asarray(tb["tw_im"]),
              bf("b_fwd"), bf("b_inv"), bf("a_inv")]
    return pl.pallas_call(
        _fft_conv_kernel,
        grid=(nblk,),
        in_specs=[pl.BlockSpec(memory_space=pltpu.SMEM),
                  pl.BlockSpec((b, 1, h * cb, n2), lambda j: (0, j, 0, 0)),
                  pl.BlockSpec((1, n1 * cb, n2), lambda j: (j, 0, 0))]
                 + [_const_spec(a.shape) for a in consts],
        out_specs=pl.BlockSpec((b, 1, h * cb, n2), lambda j: (0, j, 0, 0)),
        out_shape=jax.ShapeDtypeStruct(zq.shape, F32),
        scratch_shapes=[pltpu.VMEM((cb * n1, 2 * n2), BF16), pltpu.VMEM((cb * n1, 2 * n2), BF16),
                        pltpu.VMEM((cb // 2, 2 * n1, 2 * n2), BF16)],
        compiler_params=_params(1),
        name="hyena_long_conv",
    )(bias, zq, kq, *consts)


@functools.lru_cache(maxsize=None)
def _dft_tables(length):
    n = 2 * length
    w = np.exp(-2j * np.pi * np.outer(np.arange(n), np.arange(n)) / n)
    fwd = np.concatenate([w.real, w.imag], axis=1)
    inv = np.concatenate([w.real[:, :length], w.imag[:, :length]], axis=0) / n
    return fwd[:length].astype(np.float32), fwd.astype(np.float32), inv.astype(np.float32)


def _dft_conv_kernel(z_ref, k_ref, bias_ref, fz_ref, fk_ref, fi_ref, y_ref):
    n = k_ref.shape[1]
    ks = _dot(k_ref[...], fk_ref[...], HIGHEST)
    kr, ki = ks[:, :n], ks[:, n:]
    for b in range(z_ref.shape[0]):
        z = z_ref[b]
        xs = _dot(z, fz_ref[...], HIGHEST)
        xr, xi = xs[:, :n], xs[:, n:]
        ys = jnp.concatenate([xr * kr - xi * ki, xr * ki + xi * kr], axis=1)
        y_ref[b] = _dot(ys, fi_ref[...], HIGHEST) + z * bias_ref[...]


def _dft_conv_call(zt, kt, bias):
    b, ch, length = zt.shape
    fz, fk, fi = (jnp.asarray(a) for a in _dft_tables(length))
    return pl.pallas_call(
        _dft_conv_kernel,
        out_shape=jax.ShapeDtypeStruct((b, ch, length), F32),
        compiler_params=pltpu.CompilerParams(vmem_limit_bytes=VMEM_LIMIT),
        name="hyena_ctx_conv",
    )(zt, kt, bias.reshape(ch, 1), fz, fk, fi)


def _head_norm(o):
    mu = jnp.mean(o, axis=-1, keepdims=True)
    var = jnp.mean(jnp.square(o - mu), axis=-1, keepdims=True)
    return (o - mu) * lax.rsqrt(var + HEAD_NORM_EPS)


def _merge_kernel(*refs, with_ctx, split):
    refs = list(refs)
    of_ref, ob_ref, rg_ref, gr_ref, x0_ref, yl_ref = refs[:6]
    del refs[:6]
    yc_ref = refs.pop(0) if with_ctx else None
    mg_ref = refs.pop(0)
    ctx_ref = refs.pop(0) if split else None
    x_ref, mod_ref, wb_ref, wo_ref, out_ref = refs
    o = of_ref[0].astype(F32) + ob_ref[0].astype(F32)
    mixed = None
    for m, gate_ref in enumerate((rg_ref, gr_ref)):
        heads = [_head_norm(o[:, m * V_W + hd * DV:m * V_W + (hd + 1) * DV]) for hd in range(N_HEADS)]
        br = jnp.concatenate(heads, axis=1) * _silu(gate_ref[0].astype(F32))
        g = _sigmoid(mg_ref[0, :, m * D_MODEL:(m + 1) * D_MODEL].astype(F32))
        term = g * _dot(br.astype(BF16), wb_ref[m])
        mixed = term if mixed is None else mixed + term
    yt = jnp.concatenate([yl_ref[0, :, r * FFT_CB:(r + 1) * FFT_CB, :].reshape(D_HY, FFT_N2)
                          for r in range(TT // FFT_N2)], axis=1)
    if with_ctx:
        yt = jnp.where(pl.program_id(1) == 0, yc_ref[0], yt)
    hy = x0_ref[0].astype(F32) * yt.T
    g = _sigmoid(mg_ref[0, :, 2 * D_MODEL:3 * D_MODEL].astype(F32))
    mixed = mixed + g * _dot(hy.astype(BF16), wb_ref[2])
    mix = _dot(mixed.astype(BF16), wo_ref[...])
    x = x_ref[0]
    if split:
        x = jnp.where(pl.program_id(1) == 0, ctx_ref[0], x)
    out_ref[0] = x + mod_ref[0, 0, 2:3] * mix


def _merge_call(o_f, o_b, ret, gla, x0, y_lat, y_ctx, mg, seq_in, modsel, w_branch, w_out):
    split = isinstance(seq_in, tuple)
    with_ctx = y_ctx is not None
    assert with_ctx or not split
    b, s, _ = o_f.shape
    d = D_MODEL
    off = 0 if with_ctx else 1
    tile = lambda w, j=0: pl.BlockSpec((1, TT, w), lambda i, t: (i, t + off, j))
    in_specs = [tile(2 * V_W), tile(2 * V_W), tile(V_W, 2), tile(V_W, 2), tile(D_HY),
                pl.BlockSpec((1, D_HY // FFT_CB, TT // FFT_N2 * FFT_CB, FFT_N2),
                             lambda i, t: (i, 0, jnp.maximum(t + off - 1, 0), 0))]
    args = [o_f, o_b, ret, gla, x0, y_lat]
    if with_ctx:
        in_specs.append(pl.BlockSpec((1, D_HY, CTX_LEN), lambda i, t: (i, 0, 0)))
        args.append(y_ctx)
    in_specs.append(tile(3 * D_MODEL))
    args.append(mg)
    if split:
        in_specs += [pl.BlockSpec((1, TT, d), lambda i, t: (i, 0, 0)),
                     pl.BlockSpec((1, TT, d), lambda i, t: (i, jnp.maximum(t - 1, 0), 0))]
        args += list(seq_in)
    else:
        in_specs.append(tile(d))
        args.append(seq_in)
    in_specs += [pl.BlockSpec((1, 1, 6, d), lambda i, t: (i, jnp.minimum(t + off, 1), 0, 0)),
                 _const_spec((3, V_W, d)), _const_spec((d, d))]
    args += [modsel, w_branch, w_out]
    return pl.pallas_call(
        functools.partial(_merge_kernel, with_ctx=with_ctx, split=split),
        grid=(b, s // TT - off),
        in_specs=in_specs,
        out_specs=pl.BlockSpec((1, TT, d), lambda i, t: (i, t, 0)),
        out_shape=jax.ShapeDtypeStruct((b, s - off * TT, d), F32),
        compiler_params=_params(2),
        name="merge_out_proj",
    )(*args)


def _gelu(x):
    return 0.5 * x * (1.0 + lax.erf(x * (2.0 ** -0.5)))


def _ffn_kernel(xu_ref, xr_ref, modu_ref, modr_ref, g_ref, wu_ref, cw_ref, cb_ref, wd_ref, fg_ref,
                out_ref, a_ref, v_ref, tail_ref, acc_ref, *, has_ctx, nt, final):
    s = pl.program_id(1)
    p = lax.rem(s, 2)
    gw = GRID_W

    @pl.when(s == 0)
    def _():
        a_ref[...] = jnp.zeros_like(a_ref)
        v_ref[...] = jnp.zeros_like(v_ref)

    tail_ref[...] = a_ref[p, TT - gw:, :]
    mu = modu_ref[0, 0]
    hb = _modulate(xu_ref[0], g_ref[...], mu[3:4], mu[4:5]).astype(BF16)
    a_ref[p] = _dot(hb, wu_ref[:, :D_FF]).astype(BF16)
    v_ref[p] = _dot(hb, wu_ref[:, D_FF:]).astype(BF16)

    t = s - 1
    first_lat = 1 if has_ctx else 0
    is_ctx = t < first_lat
    up_ok = t >= first_lat + 1
    dn_ok = jnp.logical_and(t >= first_lat, t <= nt - 2)
    row_w = jnp.where(is_ctx, 0.0, 1.0)
    sub = lax.broadcasted_iota(jnp.int32, (8, LANES), 0)
    n_rows = TT // gw
    first_col = [jnp.logical_and(sub == 0, jnp.logical_or(k == 0, jnp.logical_not(is_ctx))) for k in range(n_rows)]
    last_col = [jnp.logical_and(sub == 7, jnp.logical_or(k == n_rows - 1, jnp.logical_not(is_ctx)))
                for k in range(n_rows)]

    def from_left(x):
        r = pltpu.roll(x, 1, 0)
        parts = []
        for k in range(n_rows):
            parts += [jnp.where(first_col[k], 0.0, r[k * gw:k * gw + 8]), r[k * gw + 8:(k + 1) * gw]]
        return jnp.concatenate(parts, axis=0)

    def from_right(x):
        r = pltpu.roll(x, TT - 1, 0)
        parts = []
        for k in range(n_rows):
            parts += [r[k * gw:(k + 1) * gw - 8], jnp.where(last_col[k], 0.0, r[(k + 1) * gw - 8:(k + 1) * gw])]
        return jnp.concatenate(parts, axis=0)

    def conv_chunk(cs):
        above = jnp.where(up_ok, tail_ref[:, cs].astype(F32), 0.0)
        below = jnp.where(dn_ok, a_ref[p, :gw, cs].astype(F32), 0.0)
        mid = a_ref[1 - p, :, cs].astype(F32)
        rows = (jnp.concatenate([above, mid[:TT - gw]], axis=0), mid, jnp.concatenate([mid[gw:], below], axis=0))
        w = [cw_ref[k:k + 1, cs] * (1.0 if k // 3 == 1 else row_w) for k in range(9)]
        taps = [rows[0] * w[dj] + rows[1] * w[3 + dj] + rows[2] * w[6 + dj] for dj in range(3)]
        acc = cb_ref[:, cs] + taps[1] + from_left(taps[0]) + from_right(taps[2])
        return _gelu(acc).astype(BF16) * v_ref[1 - p, :, cs]

    kb = 2 * LANES
    for j in range(D_FF // kb):
        act = jnp.concatenate([conv_chunk(slice(j * kb + i * LANES, j * kb + (i + 1) * LANES)) for i in range(2)],
                              axis=1)
        part = _dot(act, wd_ref[j * kb:(j + 1) * kb, :])
        if j == 0:
            acc_ref[...] = part
        else:
            acc_ref[...] += part
    y = xr_ref[0] + modr_ref[0, 0, 5:6] * acc_ref[...]
    if final:
        y = y * lax.rsqrt(jnp.mean(y * y, axis=-1, keepdims=True) + NORM_EPS) * fg_ref[...]
    out_ref[0] = y


def _ffn_call(x_in, modsel, g2, w_up, conv_w, conv_b, w_down, final_g, has_ctx, final):
    b, s, d = x_in.shape
    nt = s // TT
    up_tile = lambda i, t: jnp.minimum(t, nt - 1)
    fin_tile = lambda i, t: jnp.maximum(t - 1, 0)
    x_spec = lambda f: pl.BlockSpec((1, TT, d), lambda i, t: (i, f(i, t), 0))
    mod_row = (lambda tile: jnp.minimum(tile, 1)) if has_ctx else (lambda tile: 1)
    mod_spec = lambda f: pl.BlockSpec((1, 1, 6, d), lambda i, t: (i, mod_row(f(i, t)), 0, 0))
    return pl.pallas_call(
        functools.partial(_ffn_kernel, has_ctx=has_ctx, nt=nt, final=final),
        grid=(b, nt + 1),
        in_specs=[x_spec(up_tile), x_spec(fin_tile), mod_spec(up_tile), mod_spec(fin_tile),
                  _const_spec((1, d)), _const_spec((d, 2 * D_FF)),
                  _const_spec((9, D_FF)), _const_spec((1, D_FF)), _const_spec((D_FF, d)),
                  _const_spec((1, d))],
        out_specs=x_spec(fin_tile),
        out_shape=jax.ShapeDtypeStruct((b, s, d), F32),
        scratch_shapes=[pltpu.VMEM((2, TT, D_FF), BF16), pltpu.VMEM((2, TT, D_FF), BF16),
                        pltpu.VMEM((GRID_W, D_FF), BF16), pltpu.VMEM((TT, d), F32)],
        compiler_params=_params(2),
        name="conv_glu",
    )(x_in, x_in, modsel, modsel, g2, w_up, conv_w, conv_b, w_down, final_g)


def kernel(x, c, ctx, c_ctx, ada_w, ada_b, norm1_g, w_in, gla_wa2, gla_ba, hy_short_w, hy_short_b,
           hy_w1, hy_b1, hy_w2, hy_b2, hy_w3, hy_freq, hy_bias, w_branch, w_out, norm2_g, w_up,
           ffn_conv_w, ffn_conv_b, w_down, final_g):
    bsz, seq, d = x.shape
    depth = ada_w.shape[0]
    assert d == D_MODEL and ctx.shape[1] == CTX_LEN == TT and seq % TT == 0
    assert (2 * seq) % (2 * FFT_N2) == 0 and bsz == 2

    cvec = jnp.zeros((8, d), F32).at[:bsz].set(c).at[bsz].set(c_ctx)
    mod_all = _ada_call(cvec, ada_w, ada_b).reshape(depth, 8, 6, d)

    stream = (ctx, x)
    for l in range(depth):
        last = l == depth - 1
        m = mod_all[l]
        modsel = jnp.stack([jnp.broadcast_to(m[bsz], (bsz, 6, d)), m[:bsz]], axis=1)
        w = w_in[l]
        w_l, w2 = _gate_weights(w[:, LR_OFF:LR_OFF + 2 * GLA_LOWRANK], gla_wa2[l])
        ret, gla, mg, lga, x0, z_lat, z_ctx = _inproj_call(
            stream, modsel, norm1_g[l].reshape(1, d), w[:, :LR_OFF].astype(BF16),
            w[:, LR_OFF + 2 * GLA_LOWRANK:].astype(BF16), w_l, w2, gla_ba[l].reshape(1, 2 * QK_W),
            hy_short_w[l], hy_short_b[l].reshape(1, 3 * D_HY))
        o_f, o_b = _scan_call(ret, gla, lga)

        filt = (hy_w1[l], hy_b1[l], hy_w2[l], hy_b2[l], hy_w3[l], hy_freq[l])
        y_lat = _fft_conv_call(z_lat, _filter_call(seq, *filt, split=True), hy_bias[l])
        y_ctx = None if last else _dft_conv_call(z_ctx, _filter_call(CTX_LEN, *filt, split=False), hy_bias[l])

        mixed = _merge_call(o_f, o_b, ret, gla, x0, y_lat, y_ctx, mg, stream, modsel,
                            w_branch[l].astype(BF16), w_out[l].astype(BF16))
        stream = _ffn_call(mixed, modsel, norm2_g[l].reshape(1, d), w_up[l].astype(BF16),
                           ffn_conv_w[l].reshape(9, D_FF), ffn_conv_b[l].reshape(1, D_FF),
                           w_down[l].astype(BF16), final_g.reshape(1, d), has_ctx=not last, final=last)
    return stream
```

```python
import functools
import math

import numpy as np
import jax
import jax.numpy as jnp
from jax import lax
from jax.experimental import pallas as pl
from jax.experimental.pallas import tpu as pltpu

F32 = jnp.float32
BF16 = jnp.bfloat16
HIGHEST = lax.Precision.HIGHEST

D_MODEL = 1024
CTX_LEN = 256
GRID_W = 64
NORM_EPS = 1e-6
HEAD_NORM_EPS = 1e-5

N_HEADS = 4
DK = 64
DV = 128
QK_W = N_HEADS * DK
V_W = N_HEADS * DV
RET_LOG_DECAY = (tuple(math.log1p(-2.0 ** (-5.0 - h)) for h in range(N_HEADS)),
                 tuple(math.log1p(-2.0 ** (-5.5 - h)) for h in range(N_HEADS)))
GLA_LOWRANK = 16
GLA_GATE_NORM = 16.0
GLA_CHUNK = 64

D_HY = 512
HY_BANDS = 16
HY_EMB = 1 + 2 * HY_BANDS
HY_EMB_PAD = 40
HY_FILTER_WIDTH = 64
HY_INNER = 2
HY_MIN_DECAY = math.log(1e-2) / 1.5
HY_MAX_DECAY = math.log(1e-2) / 0.3
HY_SHIFT = 0.05

D_FF = 2816
D_IN_MAIN = 7680
LR_OFF = 3072

TT = 256
LANES = 128
FFT_N2 = 128
FFT_CB = 16
VMEM_LIMIT = 56 * 1024 * 1024


def _dot(a, b, precision=None):
    return jnp.dot(a, b, preferred_element_type=F32, precision=precision)


def _dot_nt(a, b):
    return lax.dot_general(a, b, (((1,), (1,)), ((), ())), preferred_element_type=F32)


def _dot_tn(a, b):
    return lax.dot_general(a, b, (((0,), (0,)), ((), ())), preferred_element_type=F32)


def _params(n_axes):
    return pltpu.CompilerParams(dimension_semantics=("arbitrary",) * n_axes,
                                vmem_limit_bytes=VMEM_LIMIT)


def _const_spec(shape):
    n = len(shape)
    return pl.BlockSpec(shape, lambda *_: (0,) * n, pipeline_mode=pl.Buffered(1))


def _modulate(x, g, shift, scale):
    y = x * lax.rsqrt(jnp.mean(x * x, axis=-1, keepdims=True) + NORM_EPS) * g
    return y * (1.0 + scale) + shift


def _sigmoid(x):
    return 0.5 * jnp.tanh(0.5 * x) + 0.5


def _silu(x):
    return x * _sigmoid(x)


def _ada_kernel(c_ref, w_ref, b_ref, o_ref):
    o_ref[0] = _dot(_silu(c_ref[...]), w_ref[0], HIGHEST) + b_ref[0]


def _ada_call(cvec, ada_w, ada_b):
    depth, d, n = ada_w.shape
    nb = 1536
    return pl.pallas_call(
        _ada_kernel,
        grid=(depth, n // nb),
        in_specs=[pl.BlockSpec((8, d), lambda l, j: (0, 0)),
                  pl.BlockSpec((1, d, nb), lambda l, j: (l, 0, j)),
                  pl.BlockSpec((1, 1, nb), lambda l, j: (l, 0, j))],
        out_specs=pl.BlockSpec((1, 8, nb), lambda l, j: (l, 0, j)),
        out_shape=jax.ShapeDtypeStruct((depth, 8, n), F32),
        compiler_params=_params(2),
        name="ada_ln",
    )(cvec, ada_w, ada_b.reshape(depth, 1, n))


def _log_sigmoid(x):
    return jnp.minimum(x, 0.0) - jnp.log1p(jnp.exp(-jnp.abs(x)))


def _split_bf16(a):
    hi = a.astype(BF16)
    return hi, (a - hi.astype(F32)).astype(BF16)


def _inproj_kernel(first_ref, next_ref, modf_ref, modn_ref, g_ref, wa_ref, wb_ref, wl_ref, w2_ref, ba_ref,
                   sw_ref, sb_ref, ret_ref, gla_ref, mg_ref, lga_ref, x0_ref, zl_ref, zc_ref,
                   hb_ref, cur_ref, prev_ref, *, nt):
    s = pl.program_id(1)
    g = g_ref[...]

    @pl.when(s == 0)
    def _():
        cur_ref[...] = jnp.zeros_like(cur_ref)
        prev_ref[...] = jnp.zeros_like(prev_ref)
        mf = modf_ref[0, 0]
        hb_ref[...] = _modulate(first_ref[0], g, mf[0:1], mf[1:2]).astype(BF16)

    hb = hb_ref[...]
    lr_hi, lr_lo = _split_bf16(_dot(hb, wl_ref[...]))
    gate = _dot(jnp.concatenate([lr_hi, lr_lo, lr_hi], axis=1), w2_ref[...]) + ba_ref[...]
    lga_ref[0] = _log_sigmoid(gate) * (1.0 / GLA_GATE_NORM)
    hy_new = _dot(hb, wb_ref[:, :3 * D_HY])
    zt = _hyena_gates(s - 1, nt, cur_ref, prev_ref, hy_new[0:1], sw_ref, sb_ref, x0_ref)
    prev_ref[...] = cur_ref[TT - 8:, :]
    cur_ref[...] = hy_new
    mg_ref[0] = _dot(hb, wb_ref[:, 3 * D_HY:]).astype(BF16)
    ret_ref[0] = _dot(hb, wa_ref[:, :1536]).astype(BF16)
    gla_ref[0] = _dot(hb, wa_ref[:, 1536:]).astype(BF16)
    mn = modn_ref[0, 0]
    hb_ref[...] = _modulate(next_ref[0], g, mn[0:1], mn[1:2]).astype(BF16)
    _store_z(s - 1, zt, zl_ref, zc_ref)


def _gate_weights(w_lr, wa2):
    d = w_lr.shape[0]
    hi, lo = _split_bf16(w_lr)
    w1 = jnp.concatenate([hi, lo, jnp.zeros((d, LANES - 4 * GLA_LOWRANK), BF16)], axis=1)
    bd = jnp.zeros((2 * GLA_LOWRANK, 2 * QK_W), F32)
    bd = bd.at[:GLA_LOWRANK, :QK_W].set(wa2[0]).at[GLA_LOWRANK:, QK_W:].set(wa2[1])
    w2 = jnp.concatenate([bd, bd, jnp.zeros((LANES - 4 * GLA_LOWRANK, 2 * QK_W), F32)], axis=0)
    w2_hi, w2_lo = _split_bf16(w2)
    return w1, jnp.concatenate([w2_hi, w2_hi, w2_lo], axis=0)


def _inproj_call(seq_in, modsel, g1, w_a, w_b, w_l, w2, ba, short_w, short_b):
    split = isinstance(seq_in, tuple)
    if split:
        ctx, x = seq_in
        b, d = x.shape[0], x.shape[2]
        s = ctx.shape[1] + x.shape[1]
    else:
        b, s, d = seq_in.shape
    nt = s // TT
    assert nt >= 2
    cur = lambda t: jnp.minimum(t, nt - 1)
    nxt = lambda t: jnp.minimum(t + 1, nt - 1)
    tile = lambda w: pl.BlockSpec((1, TT, w), lambda i, t: (i, cur(t), 0))
    bf = lambda w: jax.ShapeDtypeStruct((b, s, w), BF16)
    first_spec = pl.BlockSpec((1, TT, d), lambda i, t: (i, 0, 0))
    if split:
        x_specs = [first_spec, pl.BlockSpec((1, TT, d), lambda i, t: (i, nxt(t) - 1, 0))]
        x_args = [ctx, x]
    else:
        x_specs = [first_spec, pl.BlockSpec((1, TT, d), lambda i, t: (i, nxt(t), 0))]
        x_args = [seq_in, seq_in]
    return pl.pallas_call(
        functools.partial(_inproj_kernel, nt=nt),
        grid=(b, nt + 1),
        in_specs=x_specs + [
            pl.BlockSpec((1, 1, 6, d), lambda i, t: (i, 0, 0, 0)),
            pl.BlockSpec((1, 1, 6, d), lambda i, t: (i, 1, 0, 0)),
            _const_spec((1, d)),
            _const_spec(w_a.shape), _const_spec(w_b.shape), _const_spec(w_l.shape),
            _const_spec((3 * LANES, 2 * QK_W)), _const_spec((1, 2 * QK_W)),
            _const_spec((3, 3 * D_HY)), _const_spec((1, 3 * D_HY))],
        out_specs=[tile(1536), tile(1536), tile(3072), tile(2 * QK_W),
                   pl.BlockSpec((1, TT, D_HY), lambda i, t: (i, jnp.maximum(t - 1, 0), 0)),
                   pl.BlockSpec((1, D_HY // FFT_CB, TT // FFT_N2 * FFT_CB, FFT_N2),
                                lambda i, t: (i, 0, jnp.maximum(t - 2, 0), 0)),
                   pl.BlockSpec((1, D_HY, CTX_LEN), lambda i, t: (i, 0, 0))],
        out_shape=[bf(1536), bf(1536), bf(3072), jax.ShapeDtypeStruct((b, s, 2 * QK_W), F32),
                   bf(D_HY),
                   jax.ShapeDtypeStruct((b, D_HY // FFT_CB, (s - CTX_LEN) // FFT_N2 * FFT_CB, FFT_N2), F32),
                   jax.ShapeDtypeStruct((b, D_HY, CTX_LEN), F32)],
        scratch_shapes=[pltpu.VMEM((TT, d), BF16), pltpu.VMEM((TT, 3 * D_HY), F32),
                        pltpu.VMEM((8, 3 * D_HY), F32)],
        compiler_params=_params(2),
        name="in_proj",
    )(*x_args, modsel, modsel, g1, w_a, w_b, w_l, w2, ba, short_w, short_b)


@functools.lru_cache(maxsize=None)
def _scan_tables():
    i = np.arange(TT, dtype=np.float64)
    scale = DK ** -0.5
    diff = i[:, None] - i[None, :]
    same = (i[:, None] // GLA_CHUNK) == (i[None, :] // GLA_CHUNK)
    dmat = np.zeros((2, N_HEADS, TT, TT))
    erow = np.zeros((2, TT, V_W))
    kw = np.zeros((2, TT, QK_W))
    tri = np.zeros((2, TT, 2 * TT))
    for d in range(2):
        for h in range(N_HEADS):
            lg = RET_LOG_DECAY[d][h]
            if d == 0:
                dmat[d, h] = np.where(diff >= 0, np.exp(np.maximum(diff, 0) * lg), 0.0) * scale
                erow[d, :, h * DV:(h + 1) * DV] = np.exp((i + 1) * lg)[:, None]
                kw[d, :, h * DK:(h + 1) * DK] = (np.exp((TT - 1 - i) * lg) * scale)[:, None]
            else:
                dmat[d, h] = np.where(diff < 0, np.exp(np.maximum(-diff, 0) * lg), 0.0) * scale
                erow[d, :, h * DV:(h + 1) * DV] = np.exp((TT - i) * lg)[:, None]
                kw[d, :, h * DK:(h + 1) * DK] = (np.exp(i * lg) * scale)[:, None]
        block = same & ((diff >= 0) if d == 0 else (diff <= 0))
        tri[d] = np.concatenate([block, block], axis=1)
    f = lambda a: a.astype(np.float32)
    return f(dmat), f(erow), f(kw), f(tri)


def _scan_kernel(retf_ref, glaf_ref, lgf_ref, retb_ref, glab_ref, lgb_ref,
                 dmat_ref, erow_ref, kw_ref, tri_ref, of_ref, ob_ref, h_ref, hbd_ref, og_ref):
    @pl.when(pl.program_id(1) == 0)
    def _():
        h_ref[...] = jnp.zeros_like(h_ref)
        hbd_ref[...] = jnp.zeros_like(hbd_ref)

    c = GLA_CHUNK
    n_chunks = TT // c
    row = lax.broadcasted_iota(jnp.int32, (TT, TT), 0)
    col = lax.broadcasted_iota(jnp.int32, (TT, TT), 1)
    same_chunk = (row // c) == (col // c)
    heads = [(slice(hd * DK, (hd + 1) * DK), slice(hd * DV, (hd + 1) * DV)) for hd in range(N_HEADS)]
    dirs = ((retf_ref, glaf_ref, lgf_ref, of_ref), (retb_ref, glab_ref, lgb_ref, ob_ref))
    for d, (ret_ref, gla_ref, lg_ref, o_ref) in enumerate(dirs):
        q = ret_ref[0, :, 0:QK_W]
        k = ret_ref[0, :, QK_W:2 * QK_W]
        v = ret_ref[0, :, 2 * QK_W:]
        o_inter = _dot(q, hbd_ref[2 * d]) * erow_ref[d]
        s_new = _dot_tn((k.astype(F32) * kw_ref[d]).astype(BF16), v)
        for hd, (ks, vs) in enumerate(heads):
            s = (_dot_nt(q[:, ks], k[:, ks]) * dmat_ref[d, hd]).astype(BF16)
            o_ref[0, :, vs] = (_dot(s, v[:, vs]) + o_inter[:, vs]).astype(BF16)
            h_new = math.exp(TT * RET_LOG_DECAY[d][hd]) * h_ref[d * 8 + hd] + s_new[ks, vs]
            h_ref[d * 8 + hd] = h_new
            hbd_ref[2 * d, ks, vs] = h_new.astype(BF16)

        hi, lo = _split_bf16(lg_ref[0])
        cum = _dot(tri_ref[d], jnp.concatenate([hi, lo], axis=0))
        edge = c - 1 if d == 0 else 0
        tot = jnp.concatenate([jnp.broadcast_to(cum[j * c + edge:j * c + edge + 1], (c, QK_W))
                               for j in range(n_chunks)], axis=0)
        a_tot = jnp.exp(tot)
        k_neg = gla_ref[0, :, QK_W:2 * QK_W].astype(F32) * jnp.exp(-cum)
        q_in = (gla_ref[0, :, 0:QK_W].astype(F32) * jnp.exp(cum) * DK ** -0.5).astype(BF16)
        k_out = (k_neg * a_tot).astype(BF16)
        k_neg = k_neg.astype(BF16)
        v = gla_ref[0, :, 2 * QK_W:]
        mask = jnp.logical_and(same_chunk, row >= col if d == 0 else row < col)
        for hd, (ks, vs) in enumerate(heads):
            s = jnp.where(mask, _dot_nt(q_in[:, ks], k_neg[:, ks]), 0.0).astype(BF16)
            og_ref[:, vs] = _dot(s, v[:, vs])
        for j in (range(n_chunks) if d == 0 else range(n_chunks - 1, -1, -1)):
            rows = slice(j * c, (j + 1) * c)
            inter = _dot(q_in[rows], hbd_ref[2 * d + 1])
            o_ref[0, rows, V_W:] = (og_ref[rows, :] + inter).astype(BF16)
            s_j = _dot_tn(k_out[rows], v[rows])
            a_col = jnp.broadcast_to(a_tot[j * c:j * c + 1], (DV, QK_W)).T
            for hd, (ks, vs) in enumerate(heads):
                h_new = a_col[ks] * h_ref[d * 8 + 4 + hd] + s_j[ks, vs]
                h_ref[d * 8 + 4 + hd] = h_new
                hbd_ref[2 * d + 1, ks, vs] = h_new.astype(BF16)


def _scan_call(ret, gla, lga):
    b, s, _ = ret.shape
    nt = s // TT
    dmat, erow, kw, tri = (jnp.asarray(z) for z in _scan_tables())
    tri = tri.astype(BF16)
    fwd = lambda i, t: (i, t, 0)
    bwd = lambda i, t: (i, jnp.where(t == 0, 0, nt - t), 0)
    bwd_g = lambda i, t: (i, jnp.where(t == 0, 0, nt - t), 1)
    qkv = lambda f: pl.BlockSpec((1, TT, 2 * QK_W + V_W), f)
    return pl.pallas_call(
        _scan_kernel,
        grid=(b, nt),
        in_specs=[qkv(fwd), qkv(fwd), pl.BlockSpec((1, TT, QK_W), fwd),
                  qkv(bwd), qkv(bwd), pl.BlockSpec((1, TT, QK_W), bwd_g),
                  _const_spec(dmat.shape), _const_spec(erow.shape), _const_spec(kw.shape),
                  _const_spec(tri.shape)],
        out_specs=[pl.BlockSpec((1, TT, 2 * V_W), fwd), pl.BlockSpec((1, TT, 2 * V_W), bwd)],
        out_shape=[jax.ShapeDtypeStruct((b, s, 2 * V_W), BF16)] * 2,
        scratch_shapes=[pltpu.VMEM((16, DK, DV), F32), pltpu.VMEM((4, QK_W, V_W), BF16),
                        pltpu.VMEM((TT, V_W), F32)],
        compiler_params=_params(2),
        name="bidir_scan",
    )(ret, gla, lga, ret, gla, lga, dmat, erow, kw, tri)


def _hyena_gates(t, nt, cur_ref, prev_ref, next_row, w_ref, b_ref, x0_ref):
    has_prev = t >= 2
    has_next = jnp.logical_and(t >= 1, t <= nt - 2)
    ridx = lax.broadcasted_iota(jnp.int32, (TT, LANES), 0)
    u = []
    for part in range(3):
        cols = []
        for j in range(D_HY // LANES):
            cs = slice(part * D_HY + j * LANES, part * D_HY + (j + 1) * LANES)
            p = cur_ref[:, cs]
            pv = jnp.where(has_prev, prev_ref[7:8, cs], 0.0)
            nx = jnp.where(has_next, next_row[:, cs], 0.0)
            up = jnp.where(ridx == 0, pv, pltpu.roll(p, 1, 0))
            dn = jnp.where(ridx == TT - 1, nx, pltpu.roll(p, TT - 1, 0))
            cols.append(b_ref[:, cs] + up * w_ref[0:1, cs] + p * w_ref[1:2, cs] + dn * w_ref[2:3, cs])
        u.append(cols)
    x0_ref[0] = jnp.concatenate(u[0], axis=1).astype(BF16)
    z = jnp.concatenate([a * c for a, c in zip(u[1], u[2])], axis=1)
    return z.T


def _store_z(t, zt, zl_ref, zc_ref):
    @pl.when(t == 0)
    def _():
        zc_ref[0] = zt

    @pl.when(t > 0)
    def _():
        for r in range(TT // FFT_N2):
            zl_ref[0, :, r * FFT_CB:(r + 1) * FFT_CB, :] = _to_conv_rows(zt[:, r * FFT_N2:(r + 1) * FFT_N2])


@functools.lru_cache(maxsize=None)
def _filter_positions(length):
    n = np.arange(2 * length)
    m = np.where(n < length, n, 2 * length - n).astype(np.float64)
    m = np.where(n == length, 0.0, m)
    t = m / (length - 1)
    bands = np.linspace(1e-4, HY_BANDS - 1.0, HY_BANDS)
    ang = (2.0 * math.pi / length) * m[None, :] * bands[:, None]
    z = np.zeros((HY_EMB_PAD, 2 * length))
    z[0] = t
    z[1:1 + HY_BANDS] = np.cos(ang)
    z[1 + HY_BANDS:HY_EMB] = -np.sin(ang)
    return z.astype(np.float32)


def _filter_kernel(z_ref, w1_ref, b1_ref, w2_ref, b2_ref, w3_ref, fr_ref, dl_ref, k_ref, *, length, pb, split):
    z = z_ref[...]
    fr = fr_ref[...]
    hdn = jnp.sin(fr * (_dot(w1_ref[...], z, HIGHEST) + b1_ref[...]))
    for i in range(HY_INNER):
        hdn = jnp.sin(fr * (_dot(w2_ref[i], hdn, HIGHEST) + b2_ref[i]))
    h = _dot(w3_ref[0], hdn, HIGHEST)
    window = jnp.exp(-dl_ref[...] * z[0:1]) + HY_SHIFT
    pos = pl.program_id(0) * pb + lax.broadcasted_iota(jnp.int32, (1, pb), 1)
    k = jnp.where(pos == length, 0.0, h * window)
    if split:
        for r in range(pb // FFT_N2):
            k_ref[:, r * FFT_CB:(r + 1) * FFT_CB, :] = _to_conv_rows(k[:, r * FFT_N2:(r + 1) * FFT_N2])
    else:
        k_ref[...] = k


def _filter_call(length, w1, b1, w2, b2, w3, freq, split):
    pb = min(2048, length)
    if split:
        out_spec = pl.BlockSpec((D_HY // FFT_CB, pb // FFT_N2 * FFT_CB, FFT_N2), lambda j: (0, j, 0))
        out_shape = jax.ShapeDtypeStruct((D_HY // FFT_CB, 2 * length // FFT_N2 * FFT_CB, FFT_N2), F32)
    else:
        out_spec = pl.BlockSpec((D_HY, pb), lambda j: (0, j))
        out_shape = jax.ShapeDtypeStruct((D_HY, 2 * length), F32)
    zf = jnp.asarray(_filter_positions(length))
    w1t = jnp.pad(w1.T, ((0, 0), (0, HY_EMB_PAD - HY_EMB)))
    w2t = jnp.swapaxes(w2, 1, 2)
    w3t = w3.T.reshape(2, D_HY, HY_FILTER_WIDTH)
    col = lambda a: a.reshape(a.shape + (1,))
    deltas = np.abs(np.linspace(HY_MIN_DECAY, HY_MAX_DECAY, D_HY)).astype(np.float32)
    fw = HY_FILTER_WIDTH
    return pl.pallas_call(
        functools.partial(_filter_kernel, length=length, pb=pb, split=split),
        grid=(2 * length // pb,),
        in_specs=[pl.BlockSpec((HY_EMB_PAD, pb), lambda j: (0, j)),
                  _const_spec((fw, HY_EMB_PAD)), _const_spec((fw, 1)),
                  _const_spec((HY_INNER, fw, fw)), _const_spec((HY_INNER, fw, 1)),
                  pl.BlockSpec((1, D_HY, fw), lambda j: (jnp.where(j >= length // pb, 1, 0), 0, 0)),
                  _const_spec((fw, 1)), _const_spec((D_HY, 1))],
        out_specs=out_spec,
        out_shape=out_shape,
        compiler_params=_params(1),
        name="hyena_filter",
    )(zf, w1t, col(b1), w2t, col(b2), w3t, col(freq), jnp.asarray(deltas).reshape(D_HY, 1))


def _real_block(w):
    return np.block([[w.real, w.imag], [-w.imag, w.real]])


@functools.lru_cache(maxsize=None)
def _fft_tables(length):
    n = 2 * length
    n2 = FFT_N2
    n1 = n // n2
    f1 = np.arange(n1)
    wa = np.exp(-2j * np.pi * np.outer(f1, np.arange(n1)) / n1)
    stage_a = np.concatenate([wa.real, wa.imag], axis=0)
    tw = np.exp(-2j * np.pi * np.outer(f1, np.arange(n2)) / n)
    wb = np.exp(-2j * np.pi * np.outer(np.arange(n2), np.arange(n2)) / n2)
    wh = wa[:, :n1 // 2]
    a_pair = np.block([[wh.real, -wh.imag], [wh.imag, wh.real]])
    wi = np.conj(wa[:n1 // 2]) / n
    a_inv = np.block([[wi.real, -wi.imag], [wi.imag, wi.real]])
    f = lambda a: a.astype(np.float32)
    return dict(a_pair=f(a_pair), a_k=f(stage_a), tw_re=f(tw.real), tw_im=f(tw.imag),
                b_fwd=f(_real_block(wb)), b_inv=f(_real_block(np.conj(wb))), a_inv=f(a_inv))


def _fft_conv_kernel(bias_ref, z_ref, k_ref, ap_ref, ak_ref, twr_ref, twi_ref, bf_ref, bi_ref, ai_ref,
                     y_ref, sz_ref, sk_ref, sv_ref):
    n1, n2 = twr_ref.shape
    h, cb = n1 // 2, FFT_CB
    z0, z1 = (z_ref.at[b, 0] for b in range(2))
    y0, y1 = (y_ref.at[b, 0] for b in range(2))
    kk = k_ref.at[0]
    twr = twr_ref[...]
    twi = twi_ref[...]

    def twiddled(a):
        ar, ai = a[:n1], a[n1:]
        return jnp.concatenate([ar * twr - ai * twi, ar * twi + ai * twr], axis=1).astype(BF16)

    def stage_a(p, carry):
        pair = (2 * p, 2 * p + 1)
        zc = jnp.concatenate([jnp.concatenate([z0[pl.ds(c, h, stride=cb), :], z1[pl.ds(c, h, stride=cb), :]], axis=0)
                              for c in pair], axis=1)
        kc = jnp.concatenate([kk[pl.ds(c, n1, stride=cb), :] for c in pair], axis=1)
        az = _dot(ap_ref[...], zc.astype(BF16))
        ak = _dot(ak_ref[...], kc.astype(BF16))
        for i, c in enumerate(pair):
            rows = pl.ds(pl.multiple_of(c * n1, n1), n1)
            sz_ref[rows, :] = twiddled(az[:, i * n2:(i + 1) * n2])
            sk_ref[rows, :] = twiddled(ak[:, i * n2:(i + 1) * n2])
        return carry

    lax.fori_loop(0, cb // 2, stage_a, 0, unroll=4)

    gb = 4

    def stage_b(g, carry):
        rows = pl.ds(pl.multiple_of(g * (gb * n1), gb * n1), gb * n1)
        xs = _dot(sz_ref[rows, :], bf_ref[...])
        ks = _dot(sk_ref[rows, :], bf_ref[...])
        xr, xi, kr, ki = xs[:, :n2], xs[:, n2:], ks[:, :n2], ks[:, n2:]
        ys = jnp.concatenate([xr * kr - xi * ki, xr * ki + xi * kr], axis=1).astype(BF16)
        u = _dot(ys, bi_ref[...])
        for i in range(gb):
            ur, ui = u[i * n1:(i + 1) * n1, :n2], u[i * n1:(i + 1) * n1, n2:]
            lanes = slice((i % 2) * n2, (i % 2 + 1) * n2)
            sv_ref[g * (gb // 2) + i // 2, :n1, lanes] = (ur * twr + ui * twi).astype(BF16)
            sv_ref[g * (gb // 2) + i // 2, n1:, lanes] = (ui * twr - ur * twi).astype(BF16)
        return carry

    lax.fori_loop(0, cb // gb, stage_b, 0, unroll=2)

    def stage_c(p, carry):
        y = _dot(ai_ref[...], sv_ref[p])
        for i in range(2):
            c = 2 * p + i
            bias = bias_ref[pl.program_id(0) * cb + c]
            sel = pl.ds(c, h, stride=cb)
            y0[sel, :] = y[:h, i * n2:(i + 1) * n2] + z0[sel, :] * bias
            y1[sel, :] = y[h:, i * n2:(i + 1) * n2] + z1[sel, :] * bias
        return carry

    lax.fori_loop(0, cb // 2, stage_c, 0, unroll=4)


def _to_conv_rows(a):
    return a.reshape(a.shape[0] // FFT_CB, FFT_CB, a.shape[1])


def _fft_conv_call(zq, kq, bias):
    cb = FFT_CB
    b, nblk, rows, n2 = zq.shape
    assert b == 2 and n2 == FFT_N2
    h, ch = rows // cb, nblk * cb
    n1 = 2 * h
    tb = _fft_tables(h * n2)
    bf = lambda name: jnp.asarray(tb[name]).astype(BF16)
    consts = [bf("a_pair"), bf("a_k"), jnp.asarray(tb["tw_re"]), jnp.asarray(tb["tw_im"]),
              bf("b_fwd"), bf("b_inv"), bf("a_inv")]
    return pl.pallas_call(
        _fft_conv_kernel,
        grid=(nblk,),
        in_specs=[pl.BlockSpec(memory_space=pltpu.SMEM),
                  pl.BlockSpec((b, 1, h * cb, n2), lambda j: (0, j, 0, 0)),
                  pl.BlockSpec((1, n1 * cb, n2), lambda j: (j, 0, 0))]
                 + [_const_spec(a.shape) for a in consts],
        out_specs=pl.BlockSpec((b, 1, h * cb, n2), lambda j: (0, j, 0, 0)),
        out_shape=jax.ShapeDtypeStruct(zq.shape, F32),
        scratch_shapes=[pltpu.VMEM((cb * n1, 2 * n2), BF16), pltpu.VMEM((cb * n1, 2 * n2), BF16),
                        pltpu.VMEM((cb // 2, 2 * n1, 2 * n2), BF16)],
        compiler_params=_params(1),
        name="hyena_long_conv",
    )(bias, zq, kq, *consts)


@functools.lru_cache(maxsize=None)
def _dft_tables(length):
    n = 2 * length
    w = np.exp(-2j * np.pi * np.outer(np.arange(n), np.arange(n)) / n)
    fwd = np.concatenate([w.real, w.imag], axis=1)
    inv = np.concatenate([w.real[:, :length], w.imag[:, :length]], axis=0) / n
    return fwd[:length].astype(np.float32), fwd.astype(np.float32), inv.astype(np.float32)


def _dft_conv_kernel(z_ref, k_ref, bias_ref, fz_ref, fk_ref, fi_ref, y_ref):
    n = k_ref.shape[1]
    ks = _dot(k_ref[...], fk_ref[...], HIGHEST)
    kr, ki = ks[:, :n], ks[:, n:]
    for b in range(z_ref.shape[0]):
        z = z_ref[b]
        xs = _dot(z, fz_ref[...], HIGHEST)
        xr, xi = xs[:, :n], xs[:, n:]
        ys = jnp.concatenate([xr * kr - xi * ki, xr * ki + xi * kr], axis=1)
        y_ref[b] = _dot(ys, fi_ref[...], HIGHEST) + z * bias_ref[...]


def _dft_conv_call(zt, kt, bias):
    b, ch, length = zt.shape
    fz, fk, fi = (jnp.asarray(a) for a in _dft_tables(length))
    return pl.pallas_call(
        _dft_conv_kernel,
        out_shape=jax.ShapeDtypeStruct((b, ch, length), F32),
        compiler_params=pltpu.CompilerParams(vmem_limit_bytes=VMEM_LIMIT),
        name="hyena_ctx_conv",
    )(zt, kt, bias.reshape(ch, 1), fz, fk, fi)


def _head_norm(o):
    mu = jnp.mean(o, axis=-1, keepdims=True)
    var = jnp.mean(jnp.square(o - mu), axis=-1, keepdims=True)
    return (o - mu) * lax.rsqrt(var + HEAD_NORM_EPS)


def _merge_mix(of_ref, ob_ref, rg_ref, gr_ref, x0_ref, yl_ref, yc_ref, mg_ref, wb_ref, wo_ref, is_ctx):
    o = of_ref[0].astype(F32) + ob_ref[0].astype(F32)
    mixed = None
    for m, gate_ref in enumerate((rg_ref, gr_ref)):
        heads = [_head_norm(o[:, m * V_W + hd * DV:m * V_W + (hd + 1) * DV]) for hd in range(N_HEADS)]
        br = jnp.concatenate(heads, axis=1) * _silu(gate_ref[0].astype(F32))
        g = _sigmoid(mg_ref[0, :, m * D_MODEL:(m + 1) * D_MODEL].astype(F32))
        term = g * _dot(br.astype(BF16), wb_ref[m])
        mixed = term if mixed is None else mixed + term
    yt = jnp.concatenate([yl_ref[0, :, r * FFT_CB:(r + 1) * FFT_CB, :].reshape(D_HY, FFT_N2)
                          for r in range(TT // FFT_N2)], axis=1)
    if yc_ref is not None:
        yt = jnp.where(is_ctx, yc_ref[0], yt)
    hy = x0_ref[0].astype(F32) * yt.T
    g = _sigmoid(mg_ref[0, :, 2 * D_MODEL:3 * D_MODEL].astype(F32))
    mixed = mixed + g * _dot(hy.astype(BF16), wb_ref[2])
    return _dot(mixed.astype(BF16), wo_ref[...])


def _gelu(x):
    return 0.5 * x * (1.0 + lax.erf(x * (2.0 ** -0.5)))


def _mix_ffn_kernel(*refs, has_ctx, split, nt, final):
    refs = list(refs)
    mixers = refs[:6]
    del refs[:6]
    yc_ref = refs.pop(0) if has_ctx else None
    mg_ref = refs.pop(0)
    ctx_ref = refs.pop(0) if split else None
    (x_ref, modu_ref, modr_ref, wb_ref, wo_ref, g_ref, wu_ref, cw_ref, cb_ref, wd_ref, fg_ref,
     out_ref, a_ref, v_ref, tail_ref, acc_ref, xk_ref) = refs
    s = pl.program_id(1)
    p = lax.rem(s, 2)
    gw = GRID_W

    @pl.when(s == 0)
    def _():
        a_ref[...] = jnp.zeros_like(a_ref)
        v_ref[...] = jnp.zeros_like(v_ref)
        xk_ref[...] = jnp.zeros_like(xk_ref)

    tail_ref[...] = a_ref[p, TT - gw:, :]
    mu = modu_ref[0, 0]
    x = x_ref[0]
    if split:
        x = jnp.where(s == 0, ctx_ref[0], x)
    x1 = x + mu[2:3] * _merge_mix(*mixers, yc_ref, mg_ref, wb_ref, wo_ref, s == 0)
    xk_ref[p] = x1
    hb = _modulate(x1, g_ref[...], mu[3:4], mu[4:5]).astype(BF16)
    a_ref[p] = _dot(hb, wu_ref[:, :D_FF]).astype(BF16)
    v_ref[p] = _dot(hb, wu_ref[:, D_FF:]).astype(BF16)

    t = s - 1
    first_lat = 1 if has_ctx else 0
    is_ctx = t < first_lat
    up_ok = t >= first_lat + 1
    dn_ok = jnp.logical_and(t >= first_lat, t <= nt - 2)
    row_w = jnp.where(is_ctx, 0.0, 1.0)
    sub = lax.broadcasted_iota(jnp.int32, (8, LANES), 0)
    n_rows = TT // gw
    first_col = [jnp.logical_and(sub == 0, jnp.logical_or(k == 0, jnp.logical_not(is_ctx))) for k in range(n_rows)]
    last_col = [jnp.logical_and(sub == 7, jnp.logical_or(k == n_rows - 1, jnp.logical_not(is_ctx)))
                for k in range(n_rows)]

    def from_left(x):
        r = pltpu.roll(x, 1, 0)
        parts = []
        for k in range(n_rows):
            parts += [jnp.where(first_col[k], 0.0, r[k * gw:k * gw + 8]), r[k * gw + 8:(k + 1) * gw]]
        return jnp.concatenate(parts, axis=0)

    def from_right(x):
        r = pltpu.roll(x, TT - 1, 0)
        parts = []
        for k in range(n_rows):
            parts += [r[k * gw:(k + 1) * gw - 8], jnp.where(last_col[k], 0.0, r[(k + 1) * gw - 8:(k + 1) * gw])]
        return jnp.concatenate(parts, axis=0)

    def conv_chunk(cs):
        above = jnp.where(up_ok, tail_ref[:, cs].astype(F32), 0.0)
        below = jnp.where(dn_ok, a_ref[p, :gw, cs].astype(F32), 0.0)
        mid = a_ref[1 - p, :, cs].astype(F32)
        rows = (jnp.concatenate([above, mid[:TT - gw]], axis=0), mid, jnp.concatenate([mid[gw:], below], axis=0))
        w = [cw_ref[k:k + 1, cs] * (1.0 if k // 3 == 1 else row_w) for k in range(9)]
        taps = [rows[0] * w[dj] + rows[1] * w[3 + dj] + rows[2] * w[6 + dj] for dj in range(3)]
        acc = cb_ref[:, cs] + taps[1] + from_left(taps[0]) + from_right(taps[2])
        return _gelu(acc).astype(BF16) * v_ref[1 - p, :, cs]

    kb = 2 * LANES
    for j in range(D_FF // kb):
        act = jnp.concatenate([conv_chunk(slice(j * kb + i * LANES, j * kb + (i + 1) * LANES)) for i in range(2)],
                              axis=1)
        part = _dot(act, wd_ref[j * kb:(j + 1) * kb, :])
        if j == 0:
            acc_ref[...] = part
        else:
            acc_ref[...] += part
    y = xk_ref[1 - p] + modr_ref[0, 0, 5:6] * acc_ref[...]
    if final:
        y = y * lax.rsqrt(jnp.mean(y * y, axis=-1, keepdims=True) + NORM_EPS) * fg_ref[...]
    out_ref[0] = y


def _mix_ffn_call(o_f, o_b, ret, gla, x0, y_lat, y_ctx, mg, seq_in, modsel, w_branch, w_out,
                  g2, w_up, conv_w, conv_b, w_down, final_g, final):
    split = isinstance(seq_in, tuple)
    has_ctx = y_ctx is not None
    assert has_ctx or not split
    b, s, _ = o_f.shape
    d = D_MODEL
    off = 0 if has_ctx else 1
    n = s // TT - off
    cur = lambda t: jnp.minimum(t, n - 1) + off
    tile = lambda w, j=0: pl.BlockSpec((1, TT, w), lambda i, t: (i, cur(t), j))
    mod_spec = lambda f: pl.BlockSpec((1, 1, 6, d), lambda i, t: (i, jnp.minimum(f(t), 1), 0, 0))
    in_specs = [tile(2 * V_W), tile(2 * V_W), tile(V_W, 2), tile(V_W, 2), tile(D_HY),
                pl.BlockSpec((1, D_HY // FFT_CB, TT // FFT_N2 * FFT_CB, FFT_N2),
                             lambda i, t: (i, 0, jnp.maximum(cur(t) - 1, 0), 0))]
    args = [o_f, o_b, ret, gla, x0, y_lat]
    if has_ctx:
        in_specs.append(pl.BlockSpec((1, D_HY, CTX_LEN), lambda i, t: (i, 0, 0)))
        args.append(y_ctx)
    in_specs.append(tile(3 * D_MODEL))
    args.append(mg)
    if split:
        in_specs += [pl.BlockSpec((1, TT, d), lambda i, t: (i, 0, 0)),
                     pl.BlockSpec((1, TT, d), lambda i, t: (i, jnp.maximum(cur(t) - 1, 0), 0))]
        args += list(seq_in)
    else:
        in_specs.append(tile(d))
        args.append(seq_in)
    in_specs += [mod_spec(cur), mod_spec(lambda t: jnp.maximum(t - 1, 0) + off),
                 _const_spec((3, V_W, d)), _const_spec((d, d)),
                 _const_spec((1, d)), _const_spec((d, 2 * D_FF)),
                 _const_spec((9, D_FF)), _const_spec((1, D_FF)), _const_spec((D_FF, d)), _const_spec((1, d))]
    args += [modsel, modsel, w_branch, w_out, g2, w_up, conv_w, conv_b, w_down, final_g]
    return pl.pallas_call(
        functools.partial(_mix_ffn_kernel, has_ctx=has_ctx, split=split, nt=n, final=final),
        grid=(b, n + 1),
        in_specs=in_specs,
        out_specs=pl.BlockSpec((1, TT, d), lambda i, t: (i, jnp.maximum(t - 1, 0), 0)),
        out_shape=jax.ShapeDtypeStruct((b, n * TT, d), F32),
        scratch_shapes=[pltpu.VMEM((2, TT, D_FF), BF16), pltpu.VMEM((2, TT, D_FF), BF16),
                        pltpu.VMEM((GRID_W, D_FF), BF16), pltpu.VMEM((TT, d), F32),
                        pltpu.VMEM((2, TT, d), F32)],
        compiler_params=_params(2),
        name="mix_ffn",
    )(*args)


def kernel(x, c, ctx, c_ctx, ada_w, ada_b, norm1_g, w_in, gla_wa2, gla_ba, hy_short_w, hy_short_b,
           hy_w1, hy_b1, hy_w2, hy_b2, hy_w3, hy_freq, hy_bias, w_branch, w_out, norm2_g, w_up,
           ffn_conv_w, ffn_conv_b, w_down, final_g):
    bsz, seq, d = x.shape
    depth = ada_w.shape[0]
    assert d == D_MODEL and ctx.shape[1] == CTX_LEN == TT and seq % TT == 0
    assert (2 * seq) % (2 * FFT_N2) == 0 and bsz == 2

    cvec = jnp.zeros((8, d), F32).at[:bsz].set(c).at[bsz].set(c_ctx)
    mod_all = _ada_call(cvec, ada_w, ada_b).reshape(depth, 8, 6, d)

    stream = (ctx, x)
    for l in range(depth):
        last = l == depth - 1
        m = mod_all[l]
        modsel = jnp.stack([jnp.broadcast_to(m[bsz], (bsz, 6, d)), m[:bsz]], axis=1)
        w = w_in[l]
        w_l, w2 = _gate_weights(w[:, LR_OFF:LR_OFF + 2 * GLA_LOWRANK], gla_wa2[l])
        ret, gla, mg, lga, x0, z_lat, z_ctx = _inproj_call(
            stream, modsel, norm1_g[l].reshape(1, d), w[:, :LR_OFF].astype(BF16),
            w[:, LR_OFF + 2 * GLA_LOWRANK:].astype(BF16), w_l, w2, gla_ba[l].reshape(1, 2 * QK_W),
            hy_short_w[l], hy_short_b[l].reshape(1, 3 * D_HY))
        o_f, o_b = _scan_call(ret, gla, lga)

        filt = (hy_w1[l], hy_b1[l], hy_w2[l], hy_b2[l], hy_w3[l], hy_freq[l])
        y_lat = _fft_conv_call(z_lat, _filter_call(seq, *filt, split=True), hy_bias[l])
        y_ctx = None if last else _dft_conv_call(z_ctx, _filter_call(CTX_LEN, *filt, split=False), hy_bias[l])

        stream = _mix_ffn_call(o_f, o_b, ret, gla, x0, y_lat, y_ctx, mg, stream, modsel,
                               w_branch[l].astype(BF16), w_out[l].astype(BF16),
                               norm2_g[l].reshape(1, d), w_up[l].astype(BF16),
                               ffn_conv_w[l].reshape(9, D_FF), ffn_conv_b[l].reshape(1, D_FF),
                               w_down[l].astype(BF16), final_g.reshape(1, d), final=last)
    return stream
```

```python
import functools
import math

import numpy as np
import jax
import jax.numpy as jnp
from jax import lax
from jax.experimental import pallas as pl
from jax.experimental.pallas import tpu as pltpu

F32 = jnp.float32
BF16 = jnp.bfloat16
HIGHEST = lax.Precision.HIGHEST

D_MODEL = 1024
CTX_LEN = 256
GRID_W = 64
NORM_EPS = 1e-6
HEAD_NORM_EPS = 1e-5

N_HEADS = 4
DK = 64
DV = 128
QK_W = N_HEADS * DK
V_W = N_HEADS * DV
RET_LOG_DECAY = (tuple(math.log1p(-2.0 ** (-5.0 - h)) for h in range(N_HEADS)),
                 tuple(math.log1p(-2.0 ** (-5.5 - h)) for h in range(N_HEADS)))
GLA_LOWRANK = 16
GLA_GATE_NORM = 16.0
GLA_CHUNK = 64

D_HY = 512
HY_BANDS = 16
HY_EMB = 1 + 2 * HY_BANDS
HY_EMB_PAD = 40
HY_FILTER_WIDTH = 64
HY_INNER = 2
HY_MIN_DECAY = math.log(1e-2) / 1.5
HY_MAX_DECAY = math.log(1e-2) / 0.3
HY_SHIFT = 0.05

D_FF = 2816
D_IN_MAIN = 7680
LR_OFF = 3072

TT = 256
LANES = 128
FFT_N2 = 128
FFT_CB = 16
VMEM_LIMIT = 56 * 1024 * 1024


def _dot(a, b, precision=None):
    return jnp.dot(a, b, preferred_element_type=F32, precision=precision)


def _dot_nt(a, b):
    return lax.dot_general(a, b, (((1,), (1,)), ((), ())), preferred_element_type=F32)


def _dot_tn(a, b):
    return lax.dot_general(a, b, (((0,), (0,)), ((), ())), preferred_element_type=F32)


def _params(n_axes):
    return pltpu.CompilerParams(dimension_semantics=("arbitrary",) * n_axes,
                                vmem_limit_bytes=VMEM_LIMIT)


def _const_spec(shape):
    n = len(shape)
    return pl.BlockSpec(shape, lambda *_: (0,) * n, pipeline_mode=pl.Buffered(1))


def _modulate(x, g, shift, scale):
    y = x * lax.rsqrt(jnp.mean(x * x, axis=-1, keepdims=True) + NORM_EPS) * g
    return y * (1.0 + scale) + shift


def _sigmoid(x):
    return 0.5 * jnp.tanh(0.5 * x) + 0.5


def _silu(x):
    return x * _sigmoid(x)


def _ada_kernel(c_ref, w_ref, b_ref, o_ref):
    o_ref[0] = _dot(_silu(c_ref[...]), w_ref[0], HIGHEST) + b_ref[0]


def _ada_call(cvec, ada_w, ada_b):
    depth, d, n = ada_w.shape
    nb = 1536
    return pl.pallas_call(
        _ada_kernel,
        grid=(depth, n // nb),
        in_specs=[pl.BlockSpec((8, d), lambda l, j: (0, 0)),
                  pl.BlockSpec((1, d, nb), lambda l, j: (l, 0, j)),
                  pl.BlockSpec((1, 1, nb), lambda l, j: (l, 0, j))],
        out_specs=pl.BlockSpec((1, 8, nb), lambda l, j: (l, 0, j)),
        out_shape=jax.ShapeDtypeStruct((depth, 8, n), F32),
        compiler_params=_params(2),
        name="ada_ln",
    )(cvec, ada_w, ada_b.reshape(depth, 1, n))


def _log_sigmoid(x):
    return jnp.minimum(x, 0.0) - jnp.log1p(jnp.exp(-jnp.abs(x)))


def _split_bf16(a):
    hi = a.astype(BF16)
    return hi, (a - hi.astype(F32)).astype(BF16)


def _inproj_kernel(first_ref, next_ref, modf_ref, modn_ref, g_ref, wa_ref, wb_ref, wl_ref, w2_ref, ba_ref,
                   sw_ref, sb_ref, ret_ref, gla_ref, mg_ref, lga_ref, x0_ref, zl_ref, zc_ref,
                   hb_ref, cur_ref, prev_ref, *, nt):
    s = pl.program_id(1)
    g = g_ref[...]

    @pl.when(s == 0)
    def _():
        cur_ref[...] = jnp.zeros_like(cur_ref)
        prev_ref[...] = jnp.zeros_like(prev_ref)
        mf = modf_ref[0, 0]
        hb_ref[...] = _modulate(first_ref[0], g, mf[0:1], mf[1:2]).astype(BF16)

    hb = hb_ref[...]
    lr_hi, lr_lo = _split_bf16(_dot(hb, wl_ref[...]))
    gate = _dot(jnp.concatenate([lr_hi, lr_lo, lr_hi], axis=1), w2_ref[...]) + ba_ref[...]
    lga_ref[0] = _log_sigmoid(gate) * (1.0 / GLA_GATE_NORM)
    hy_new = _dot(hb, wb_ref[:, :3 * D_HY])
    zt = _hyena_gates(s - 1, nt, cur_ref, prev_ref, hy_new[0:1], sw_ref, sb_ref, x0_ref)
    prev_ref[...] = cur_ref[TT - 8:, :]
    cur_ref[...] = hy_new
    mg_ref[0] = _dot(hb, wb_ref[:, 3 * D_HY:]).astype(BF16)
    ret_ref[0] = _dot(hb, wa_ref[:, :1536]).astype(BF16)
    gla_ref[0] = _dot(hb, wa_ref[:, 1536:]).astype(BF16)
    mn = modn_ref[0, 0]
    hb_ref[...] = _modulate(next_ref[0], g, mn[0:1], mn[1:2]).astype(BF16)
    _store_z(s - 1, zt, zl_ref, zc_ref)


@functools.lru_cache(maxsize=None)
def _gate_halving():
    scale = np.ones((LR_OFF + 2 * GLA_LOWRANK + 3 * D_HY + 3 * D_MODEL,), np.float32)
    scale[2 * QK_W + V_W:2 * QK_W + 2 * V_W] = 0.5
    scale[4 * QK_W + 3 * V_W:LR_OFF] = 0.5
    scale[LR_OFF + 2 * GLA_LOWRANK + 3 * D_HY:] = 0.5
    return scale


def _gate_weights(w_lr, wa2):
    d = w_lr.shape[0]
    hi, lo = _split_bf16(w_lr)
    w1 = jnp.concatenate([hi, lo, jnp.zeros((d, LANES - 4 * GLA_LOWRANK), BF16)], axis=1)
    bd = jnp.zeros((2 * GLA_LOWRANK, 2 * QK_W), F32)
    bd = bd.at[:GLA_LOWRANK, :QK_W].set(wa2[0]).at[GLA_LOWRANK:, QK_W:].set(wa2[1])
    w2 = jnp.concatenate([bd, bd, jnp.zeros((LANES - 4 * GLA_LOWRANK, 2 * QK_W), F32)], axis=0)
    w2_hi, w2_lo = _split_bf16(w2)
    return w1, jnp.concatenate([w2_hi, w2_hi, w2_lo], axis=0)


def _inproj_call(seq_in, modsel, g1, w_a, w_b, w_l, w2, ba, short_w, short_b):
    split = isinstance(seq_in, tuple)
    if split:
        ctx, x = seq_in
        b, d = x.shape[0], x.shape[2]
        s = ctx.shape[1] + x.shape[1]
    else:
        b, s, d = seq_in.shape
    nt = s // TT
    assert nt >= 2
    cur = lambda t: jnp.minimum(t, nt - 1)
    nxt = lambda t: jnp.minimum(t + 1, nt - 1)
    tile = lambda w: pl.BlockSpec((1, TT, w), lambda i, t: (i, cur(t), 0))
    bf = lambda w: jax.ShapeDtypeStruct((b, s, w), BF16)
    first_spec = pl.BlockSpec((1, TT, d), lambda i, t: (i, 0, 0))
    if split:
        x_specs = [first_spec, pl.BlockSpec((1, TT, d), lambda i, t: (i, nxt(t) - 1, 0))]
        x_args = [ctx, x]
    else:
        x_specs = [first_spec, pl.BlockSpec((1, TT, d), lambda i, t: (i, nxt(t), 0))]
        x_args = [seq_in, seq_in]
    return pl.pallas_call(
        functools.partial(_inproj_kernel, nt=nt),
        grid=(b, nt + 1),
        in_specs=x_specs + [
            pl.BlockSpec((1, 1, 6, d), lambda i, t: (i, 0, 0, 0)),
            pl.BlockSpec((1, 1, 6, d), lambda i, t: (i, 1, 0, 0)),
            _const_spec((1, d)),
            _const_spec(w_a.shape), _const_spec(w_b.shape), _const_spec(w_l.shape),
            _const_spec((3 * LANES, 2 * QK_W)), _const_spec((1, 2 * QK_W)),
            _const_spec((3, 3 * D_HY)), _const_spec((1, 3 * D_HY))],
        out_specs=[tile(1536), tile(1536), tile(3072), tile(2 * QK_W),
                   pl.BlockSpec((1, TT, D_HY), lambda i, t: (i, jnp.maximum(t - 1, 0), 0)),
                   pl.BlockSpec((1, D_HY // FFT_CB, TT // FFT_N2 * FFT_CB, FFT_N2),
                                lambda i, t: (i, 0, jnp.maximum(t - 2, 0), 0)),
                   pl.BlockSpec((1, D_HY, CTX_LEN), lambda i, t: (i, 0, 0))],
        out_shape=[bf(1536), bf(1536), bf(3072), jax.ShapeDtypeStruct((b, s, 2 * QK_W), F32),
                   bf(D_HY),
                   jax.ShapeDtypeStruct((b, D_HY // FFT_CB, (s - CTX_LEN) // FFT_N2 * FFT_CB, FFT_N2), F32),
                   jax.ShapeDtypeStruct((b, D_HY, CTX_LEN), F32)],
        scratch_shapes=[pltpu.VMEM((TT, d), BF16), pltpu.VMEM((TT, 3 * D_HY), F32),
                        pltpu.VMEM((8, 3 * D_HY), F32)],
        compiler_params=_params(2),
        name="in_proj",
    )(*x_args, modsel, modsel, g1, w_a, w_b, w_l, w2, ba, short_w, short_b)


@functools.lru_cache(maxsize=None)
def _scan_tables():
    i = np.arange(TT, dtype=np.float64)
    scale = DK ** -0.5
    diff = i[:, None] - i[None, :]
    same = (i[:, None] // GLA_CHUNK) == (i[None, :] // GLA_CHUNK)
    dmat = np.zeros((2, N_HEADS, TT, TT))
    erow = np.zeros((2, TT, V_W))
    kw = np.zeros((2, TT, QK_W))
    tri = np.zeros((2, TT, 2 * TT))
    for d in range(2):
        for h in range(N_HEADS):
            lg = RET_LOG_DECAY[d][h]
            if d == 0:
                dmat[d, h] = np.where(diff >= 0, np.exp(np.maximum(diff, 0) * lg), 0.0) * scale
                erow[d, :, h * DV:(h + 1) * DV] = np.exp((i + 1) * lg)[:, None]
                kw[d, :, h * DK:(h + 1) * DK] = (np.exp((TT - 1 - i) * lg) * scale)[:, None]
            else:
                dmat[d, h] = np.where(diff < 0, np.exp(np.maximum(-diff, 0) * lg), 0.0) * scale
                erow[d, :, h * DV:(h + 1) * DV] = np.exp((TT - i) * lg)[:, None]
                kw[d, :, h * DK:(h + 1) * DK] = (np.exp(i * lg) * scale)[:, None]
        block = same & ((diff >= 0) if d == 0 else (diff <= 0))
        tri[d] = np.concatenate([block, block], axis=1)
    f = lambda a: a.astype(np.float32)
    return f(dmat), f(erow), f(kw), f(tri)


def _scan_kernel(retf_ref, glaf_ref, lgf_ref, retb_ref, glab_ref, lgb_ref,
                 dmat_ref, erow_ref, kw_ref, tri_ref, of_ref, ob_ref, h_ref, hbd_ref, og_ref):
    @pl.when(pl.program_id(1) == 0)
    def _():
        h_ref[...] = jnp.zeros_like(h_ref)
        hbd_ref[...] = jnp.zeros_like(hbd_ref)

    c = GLA_CHUNK
    n_chunks = TT // c
    row = lax.broadcasted_iota(jnp.int32, (TT, TT), 0)
    col = lax.broadcasted_iota(jnp.int32, (TT, TT), 1)
    same_chunk = (row // c) == (col // c)
    heads = [(slice(hd * DK, (hd + 1) * DK), slice(hd * DV, (hd + 1) * DV)) for hd in range(N_HEADS)]
    def direction(d, ret_ref, gla_ref, lg_ref, o_ref):
        q = ret_ref[0, :, 0:QK_W]
        k = ret_ref[0, :, QK_W:2 * QK_W]
        v = ret_ref[0, :, 2 * QK_W:]
        o_inter = _dot(q, hbd_ref[2 * d]) * erow_ref[d]
        s_new = _dot_tn((k.astype(F32) * kw_ref[d]).astype(BF16), v)
        for hd, (ks, vs) in enumerate(heads):
            s = (_dot_nt(q[:, ks], k[:, ks]) * dmat_ref[d, hd]).astype(BF16)
            o_ref[0, :, vs] = (_dot(s, v[:, vs]) + o_inter[:, vs]).astype(BF16)
            h_new = math.exp(TT * RET_LOG_DECAY[d][hd]) * h_ref[d * 8 + hd] + s_new[ks, vs]
            h_ref[d * 8 + hd] = h_new
            hbd_ref[2 * d, ks, vs] = h_new.astype(BF16)
        yield

        hi, lo = _split_bf16(lg_ref[0])
        cum = _dot(tri_ref[d], jnp.concatenate([hi, lo], axis=0))
        edge = c - 1 if d == 0 else 0
        tot = jnp.concatenate([jnp.broadcast_to(cum[j * c + edge:j * c + edge + 1], (c, QK_W))
                               for j in range(n_chunks)], axis=0)
        a_tot = jnp.exp(tot)
        k_neg = gla_ref[0, :, QK_W:2 * QK_W].astype(F32) * jnp.exp(-cum)
        q_in = (gla_ref[0, :, 0:QK_W].astype(F32) * jnp.exp(cum) * DK ** -0.5).astype(BF16)
        k_out = (k_neg * a_tot).astype(BF16)
        k_neg = k_neg.astype(BF16)
        v = gla_ref[0, :, 2 * QK_W:]
        mask = jnp.logical_and(same_chunk, row >= col if d == 0 else row < col)
        yield
        for hd, (ks, vs) in enumerate(heads):
            s = jnp.where(mask, _dot_nt(q_in[:, ks], k_neg[:, ks]), 0.0).astype(BF16)
            og_ref[d, :, vs] = _dot(s, v[:, vs])
        for j in (range(n_chunks) if d == 0 else range(n_chunks - 1, -1, -1)):
            yield
            rows = slice(j * c, (j + 1) * c)
            inter = _dot(q_in[rows], hbd_ref[2 * d + 1])
            o_ref[0, rows, V_W:] = (og_ref[d, rows, :] + inter).astype(BF16)
            s_j = _dot_tn(k_out[rows], v[rows])
            a_col = jnp.broadcast_to(a_tot[j * c:j * c + 1], (DV, QK_W)).T
            for hd, (ks, vs) in enumerate(heads):
                h_new = a_col[ks] * h_ref[d * 8 + 4 + hd] + s_j[ks, vs]
                h_ref[d * 8 + 4 + hd] = h_new
                hbd_ref[2 * d + 1, ks, vs] = h_new.astype(BF16)

    pending = [direction(0, retf_ref, glaf_ref, lgf_ref, of_ref), direction(1, retb_ref, glab_ref, lgb_ref, ob_ref)]
    while pending:
        pending = [g for g in pending if next(g, True) is None]


def _scan_call(ret, gla, lga):
    b, s, _ = ret.shape
    nt = s // TT
    dmat, erow, kw, tri = (jnp.asarray(z) for z in _scan_tables())
    tri = tri.astype(BF16)
    fwd = lambda i, t: (i, t, 0)
    bwd = lambda i, t: (i, jnp.where(t == 0, 0, nt - t), 0)
    bwd_g = lambda i, t: (i, jnp.where(t == 0, 0, nt - t), 1)
    qkv = lambda f: pl.BlockSpec((1, TT, 2 * QK_W + V_W), f)
    return pl.pallas_call(
        _scan_kernel,
        grid=(b, nt),
        in_specs=[qkv(fwd), qkv(fwd), pl.BlockSpec((1, TT, QK_W), fwd),
                  qkv(bwd), qkv(bwd), pl.BlockSpec((1, TT, QK_W), bwd_g),
                  _const_spec(dmat.shape), _const_spec(erow.shape), _const_spec(kw.shape),
                  _const_spec(tri.shape)],
        out_specs=[pl.BlockSpec((1, TT, 2 * V_W), fwd), pl.BlockSpec((1, TT, 2 * V_W), bwd)],
        out_shape=[jax.ShapeDtypeStruct((b, s, 2 * V_W), BF16)] * 2,
        scratch_shapes=[pltpu.VMEM((16, DK, DV), F32), pltpu.VMEM((4, QK_W, V_W), BF16),
                        pltpu.VMEM((2, TT, V_W), F32)],
        compiler_params=_params(2),
        name="bidir_scan",
    )(ret, gla, lga, ret, gla, lga, dmat, erow, kw, tri)


def _hyena_gates(t, nt, cur_ref, prev_ref, next_row, w_ref, b_ref, x0_ref):
    has_prev = t >= 2
    has_next = jnp.logical_and(t >= 1, t <= nt - 2)
    ridx = lax.broadcasted_iota(jnp.int32, (TT, LANES), 0)
    u = []
    for part in range(3):
        cols = []
        for j in range(D_HY // LANES):
            cs = slice(part * D_HY + j * LANES, part * D_HY + (j + 1) * LANES)
            p = cur_ref[:, cs]
            pv = jnp.where(has_prev, prev_ref[7:8, cs], 0.0)
            nx = jnp.where(has_next, next_row[:, cs], 0.0)
            up = jnp.where(ridx == 0, pv, pltpu.roll(p, 1, 0))
            dn = jnp.where(ridx == TT - 1, nx, pltpu.roll(p, TT - 1, 0))
            cols.append(b_ref[:, cs] + up * w_ref[0:1, cs] + p * w_ref[1:2, cs] + dn * w_ref[2:3, cs])
        u.append(cols)
    x0_ref[0] = jnp.concatenate(u[0], axis=1).astype(BF16)
    z = jnp.concatenate([a * c for a, c in zip(u[1], u[2])], axis=1)
    return z.T


def _store_z(t, zt, zl_ref, zc_ref):
    @pl.when(t == 0)
    def _():
        zc_ref[0] = zt

    @pl.when(t > 0)
    def _():
        for r in range(TT // FFT_N2):
            zl_ref[0, :, r * FFT_CB:(r + 1) * FFT_CB, :] = _to_conv_rows(zt[:, r * FFT_N2:(r + 1) * FFT_N2])


@functools.lru_cache(maxsize=None)
def _filter_positions(length):
    n = np.arange(2 * length)
    m = np.where(n < length, n, 2 * length - n).astype(np.float64)
    m = np.where(n == length, 0.0, m)
    t = m / (length - 1)
    bands = np.linspace(1e-4, HY_BANDS - 1.0, HY_BANDS)
    ang = (2.0 * math.pi / length) * m[None, :] * bands[:, None]
    z = np.zeros((HY_EMB_PAD, 2 * length))
    z[0] = t
    z[1:1 + HY_BANDS] = np.cos(ang)
    z[1 + HY_BANDS:HY_EMB] = -np.sin(ang)
    return z.astype(np.float32)


def _filter_kernel(z_ref, w1_ref, b1_ref, w2_ref, b2_ref, w3_ref, fr_ref, dl_ref, k_ref, *, length, pb, split):
    z = z_ref[...]
    fr = fr_ref[...]
    hdn = jnp.sin(fr * (_dot(w1_ref[...], z, HIGHEST) + b1_ref[...]))
    for i in range(HY_INNER):
        hdn = jnp.sin(fr * (_dot(w2_ref[i], hdn, HIGHEST) + b2_ref[i]))
    h = _dot(w3_ref[0], hdn, HIGHEST)
    window = jnp.exp(-dl_ref[...] * z[0:1]) + HY_SHIFT
    pos = pl.program_id(0) * pb + lax.broadcasted_iota(jnp.int32, (1, pb), 1)
    k = jnp.where(pos == length, 0.0, h * window)
    if split:
        for r in range(pb // FFT_N2):
            k_ref[:, r * FFT_CB:(r + 1) * FFT_CB, :] = _to_conv_rows(k[:, r * FFT_N2:(r + 1) * FFT_N2])
    else:
        k_ref[...] = k


def _filter_call(length, w1, b1, w2, b2, w3, freq, split):
    pb = min(2048, length)
    if split:
        out_spec = pl.BlockSpec((D_HY // FFT_CB, pb // FFT_N2 * FFT_CB, FFT_N2), lambda j: (0, j, 0))
        out_shape = jax.ShapeDtypeStruct((D_HY // FFT_CB, 2 * length // FFT_N2 * FFT_CB, FFT_N2), F32)
    else:
        out_spec = pl.BlockSpec((D_HY, pb), lambda j: (0, j))
        out_shape = jax.ShapeDtypeStruct((D_HY, 2 * length), F32)
    zf = jnp.asarray(_filter_positions(length))
    w1t = jnp.pad(w1.T, ((0, 0), (0, HY_EMB_PAD - HY_EMB)))
    w2t = jnp.swapaxes(w2, 1, 2)
    w3t = w3.T.reshape(2, D_HY, HY_FILTER_WIDTH)
    col = lambda a: a.reshape(a.shape + (1,))
    deltas = np.abs(np.linspace(HY_MIN_DECAY, HY_MAX_DECAY, D_HY)).astype(np.float32)
    fw = HY_FILTER_WIDTH
    return pl.pallas_call(
        functools.partial(_filter_kernel, length=length, pb=pb, split=split),
        grid=(2 * length // pb,),
        in_specs=[pl.BlockSpec((HY_EMB_PAD, pb), lambda j: (0, j)),
                  _const_spec((fw, HY_EMB_PAD)), _const_spec((fw, 1)),
                  _const_spec((HY_INNER, fw, fw)), _const_spec((HY_INNER, fw, 1)),
                  pl.BlockSpec((1, D_HY, fw), lambda j: (jnp.where(j >= length // pb, 1, 0), 0, 0)),
                  _const_spec((fw, 1)), _const_spec((D_HY, 1))],
        out_specs=out_spec,
        out_shape=out_shape,
        compiler_params=_params(1),
        name="hyena_filter",
    )(zf, w1t, col(b1), w2t, col(b2), w3t, col(freq), jnp.asarray(deltas).reshape(D_HY, 1))


def _real_block(w):
    return np.block([[w.real, w.imag], [-w.imag, w.real]])


@functools.lru_cache(maxsize=None)
def _fft_tables(length):
    n = 2 * length
    n2 = FFT_N2
    n1 = n // n2
    f1 = np.arange(n1)
    wa = np.exp(-2j * np.pi * np.outer(f1, np.arange(n1)) / n1)
    stage_a = np.concatenate([wa.real, wa.imag], axis=0)
    tw = np.exp(-2j * np.pi * np.outer(f1, np.arange(n2)) / n)
    wb = np.exp(-2j * np.pi * np.outer(np.arange(n2), np.arange(n2)) / n2)
    wh = wa[:, :n1 // 2]
    a_pair = np.block([[wh.real, -wh.imag], [wh.imag, wh.real]])
    wi = np.conj(wa[:n1 // 2]) / n
    a_inv = np.block([[wi.real, -wi.imag], [wi.imag, wi.real]])
    f = lambda a: a.astype(np.float32)
    return dict(a_pair=f(a_pair), a_k=f(stage_a), tw_re=f(tw.real), tw_im=f(tw.imag),
                b_fwd=f(_real_block(wb)), b_inv=f(_real_block(np.conj(wb))), a_inv=f(a_inv))


def _fft_conv_kernel(bias_ref, z_ref, k_ref, ap_ref, ak_ref, twr_ref, twi_ref, bf_ref, bi_ref, ai_ref,
                     y_ref, sz_ref, sk_ref, sv_ref):
    n1, n2 = twr_ref.shape
    h, cb = n1 // 2, FFT_CB
    z0, z1 = (z_ref.at[b, 0] for b in range(2))
    y0, y1 = (y_ref.at[b, 0] for b in range(2))
    kk = k_ref.at[0]
    twr = twr_ref[...]
    twi = twi_ref[...]

    def twiddled(a):
        ar, ai = a[:n1], a[n1:]
        return jnp.concatenate([ar * twr - ai * twi, ar * twi + ai * twr], axis=1).astype(BF16)

    def stage_a(p, carry):
        pair = (2 * p, 2 * p + 1)
        zc = jnp.concatenate([jnp.concatenate([z0[pl.ds(c, h, stride=cb), :], z1[pl.ds(c, h, stride=cb), :]], axis=0)
                              for c in pair], axis=1)
        kc = jnp.concatenate([kk[pl.ds(c, n1, stride=cb), :] for c in pair], axis=1)
        az = _dot(ap_ref[...], zc.astype(BF16))
        ak = _dot(ak_ref[...], kc.astype(BF16))
        for i, c in enumerate(pair):
            rows = pl.ds(pl.multiple_of(c * n1, n1), n1)
            sz_ref[rows, :] = twiddled(az[:, i * n2:(i + 1) * n2])
            sk_ref[rows, :] = twiddled(ak[:, i * n2:(i + 1) * n2])
        return carry

    lax.fori_loop(0, cb // 2, stage_a, 0, unroll=4)

    gb = 4

    def stage_b(g, carry):
        rows = pl.ds(pl.multiple_of(g * (gb * n1), gb * n1), gb * n1)
        xs = _dot(sz_ref[rows, :], bf_ref[...])
        ks = _dot(sk_ref[rows, :], bf_ref[...])
        xr, xi, kr, ki = xs[:, :n2], xs[:, n2:], ks[:, :n2], ks[:, n2:]
        ys = jnp.concatenate([xr * kr - xi * ki, xr * ki + xi * kr], axis=1).astype(BF16)
        u = _dot(ys, bi_ref[...])
        for i in range(gb):
            ur, ui = u[i * n1:(i + 1) * n1, :n2], u[i * n1:(i + 1) * n1, n2:]
            lanes = slice((i % 2) * n2, (i % 2 + 1) * n2)
            sv_ref[g * (gb // 2) + i // 2, :n1, lanes] = (ur * twr + ui * twi).astype(BF16)
            sv_ref[g * (gb // 2) + i // 2, n1:, lanes] = (ui * twr - ur * twi).astype(BF16)
        return carry

    lax.fori_loop(0, cb // gb, stage_b, 0, unroll=2)

    def stage_c(p, carry):
        y = _dot(ai_ref[...], sv_ref[p])
        for i in range(2):
            c = 2 * p + i
            bias = bias_ref[pl.program_id(0) * cb + c]
            sel = pl.ds(c, h, stride=cb)
            y0[sel, :] = y[:h, i * n2:(i + 1) * n2] + z0[sel, :] * bias
            y1[sel, :] = y[h:, i * n2:(i + 1) * n2] + z1[sel, :] * bias
        return carry

    lax.fori_loop(0, cb // 2, stage_c, 0, unroll=4)


def _to_conv_rows(a):
    return a.reshape(a.shape[0] // FFT_CB, FFT_CB, a.shape[1])


def _fft_conv_call(zq, kq, bias):
    cb = FFT_CB
    b, nblk, rows, n2 = zq.shape
    assert b == 2 and n2 == FFT_N2
    h, ch = rows // cb, nblk * cb
    n1 = 2 * h
    tb = _fft_tables(h * n2)
    bf = lambda name: jnp.asarray(tb[name]).astype(BF16)
    consts = [bf("a_pair"), bf("a_k"), jnp.asarray(tb["tw_re"]), jnp.asarray(tb["tw_im"]),
              bf("b_fwd"), bf("b_inv"), bf("a_inv")]
    return pl.pallas_call(
        _fft_conv_kernel,
        grid=(nblk,),
        in_specs=[pl.BlockSpec(memory_space=pltpu.SMEM),
                  pl.BlockSpec((b, 1, h * cb, n2), lambda j: (0, j, 0, 0)),
                  pl.BlockSpec((1, n1 * cb, n2), lambda j: (j, 0, 0))]
                 + [_const_spec(a.shape) for a in consts],
        out_specs=pl.BlockSpec((b, 1, h * cb, n2), lambda j: (0, j, 0, 0)),
        out_shape=jax.ShapeDtypeStruct(zq.shape, F32),
        scratch_shapes=[pltpu.VMEM((cb * n1, 2 * n2), BF16), pltpu.VMEM((cb * n1, 2 * n2), BF16),
                        pltpu.VMEM((cb // 2, 2 * n1, 2 * n2), BF16)],
        compiler_params=_params(1),
        name="hyena_long_conv",
    )(bias, zq, kq, *consts)


@functools.lru_cache(maxsize=None)
def _dft_tables(length):
    n = 2 * length
    w = np.exp(-2j * np.pi * np.outer(np.arange(n), np.arange(n)) / n)
    fwd = np.concatenate([w.real, w.imag], axis=1)
    inv = np.concatenate([w.real[:, :length], w.imag[:, :length]], axis=0) / n
    return fwd[:length].astype(np.float32), fwd.astype(np.float32), inv.astype(np.float32)


def _dft_conv_kernel(z_ref, k_ref, bias_ref, fz_ref, fk_ref, fi_ref, y_ref):
    n = k_ref.shape[1]
    ks = _dot(k_ref[...], fk_ref[...], HIGHEST)
    kr, ki = ks[:, :n], ks[:, n:]
    for b in range(z_ref.shape[0]):
        z = z_ref[b]
        xs = _dot(z, fz_ref[...], HIGHEST)
        xr, xi = xs[:, :n], xs[:, n:]
        ys = jnp.concatenate([xr * kr - xi * ki, xr * ki + xi * kr], axis=1)
        y_ref[b] = _dot(ys, fi_ref[...], HIGHEST) + z * bias_ref[...]


def _dft_conv_call(zt, kt, bias):
    b, ch, length = zt.shape
    fz, fk, fi = (jnp.asarray(a) for a in _dft_tables(length))
    return pl.pallas_call(
        _dft_conv_kernel,
        out_shape=jax.ShapeDtypeStruct((b, ch, length), F32),
        compiler_params=pltpu.CompilerParams(vmem_limit_bytes=VMEM_LIMIT),
        name="hyena_ctx_conv",
    )(zt, kt, bias.reshape(ch, 1), fz, fk, fi)


def _head_norm(o):
    mu = jnp.mean(o, axis=-1, keepdims=True)
    var = jnp.mean(jnp.square(o - mu), axis=-1, keepdims=True)
    return (o - mu) * lax.rsqrt(var + HEAD_NORM_EPS)


def _merge_mix(of_ref, ob_ref, rg_ref, gr_ref, x0_ref, yl_ref, yc_ref, mg_ref, wb_ref, wo_ref, is_ctx):
    o = of_ref[0].astype(F32) + ob_ref[0].astype(F32)
    mixed = None
    for m, gate_ref in enumerate((rg_ref, gr_ref)):
        heads = [_head_norm(o[:, m * V_W + hd * DV:m * V_W + (hd + 1) * DV]) for hd in range(N_HEADS)]
        gh = gate_ref[0].astype(F32)
        br = jnp.concatenate(heads, axis=1) * (gh * (1.0 + jnp.tanh(gh)))
        g2 = 1.0 + jnp.tanh(mg_ref[0, :, m * D_MODEL:(m + 1) * D_MODEL].astype(F32))
        term = g2 * _dot(br.astype(BF16), wb_ref[m])
        mixed = term if mixed is None else mixed + term
    yt = jnp.concatenate([yl_ref[0, :, r * FFT_CB:(r + 1) * FFT_CB, :].reshape(D_HY, FFT_N2)
                          for r in range(TT // FFT_N2)], axis=1)
    if yc_ref is not None:
        yt = jnp.where(is_ctx, yc_ref[0], yt)
    hy = x0_ref[0].astype(F32) * yt.T
    g2 = 1.0 + jnp.tanh(mg_ref[0, :, 2 * D_MODEL:3 * D_MODEL].astype(F32))
    mixed = mixed + g2 * _dot(hy.astype(BF16), wb_ref[2])
    return _dot(mixed.astype(BF16), wo_ref[...])


def _gelu(x):
    return 0.5 * x * (1.0 + lax.erf(x * (2.0 ** -0.5)))


def _mix_ffn_kernel(*refs, has_ctx, split, nt, final):
    refs = list(refs)
    mixers = refs[:6]
    del refs[:6]
    yc_ref = refs.pop(0) if has_ctx else None
    mg_ref = refs.pop(0)
    ctx_ref = refs.pop(0) if split else None
    (x_ref, modu_ref, modr_ref, wb_ref, wo_ref, g_ref, wu_ref, cw_ref, cb_ref, wd_ref, fg_ref,
     out_ref, a_ref, v_ref, tail_ref, acc_ref, xk_ref) = refs
    s = pl.program_id(1)
    p = lax.rem(s, 2)
    gw = GRID_W

    @pl.when(s == 0)
    def _():
        a_ref[...] = jnp.zeros_like(a_ref)
        v_ref[...] = jnp.zeros_like(v_ref)
        xk_ref[...] = jnp.zeros_like(xk_ref)

    tail_ref[...] = a_ref[p, TT - gw:, :]
    mu = modu_ref[0, 0]
    x = x_ref[0]
    if split:
        x = jnp.where(s == 0, ctx_ref[0], x)
    x1 = x + mu[2:3] * _merge_mix(*mixers, yc_ref, mg_ref, wb_ref, wo_ref, s == 0)
    xk_ref[p] = x1
    hb = _modulate(x1, g_ref[...], mu[3:4], mu[4:5]).astype(BF16)
    a_ref[p] = _dot(hb, wu_ref[:, :D_FF]).astype(BF16)
    v_ref[p] = _dot(hb, wu_ref[:, D_FF:]).astype(BF16)

    t = s - 1
    first_lat = 1 if has_ctx else 0
    is_ctx = t < first_lat
    up_ok = t >= first_lat + 1
    dn_ok = jnp.logical_and(t >= first_lat, t <= nt - 2)
    row_w = jnp.where(is_ctx, 0.0, 1.0)
    sub = lax.broadcasted_iota(jnp.int32, (8, LANES), 0)
    n_rows = TT // gw
    first_col = [jnp.logical_and(sub == 0, jnp.logical_or(k == 0, jnp.logical_not(is_ctx))) for k in range(n_rows)]
    last_col = [jnp.logical_and(sub == 7, jnp.logical_or(k == n_rows - 1, jnp.logical_not(is_ctx)))
                for k in range(n_rows)]

    def from_left(x):
        r = pltpu.roll(x, 1, 0)
        parts = []
        for k in range(n_rows):
            parts += [jnp.where(first_col[k], 0.0, r[k * gw:k * gw + 8]), r[k * gw + 8:(k + 1) * gw]]
        return jnp.concatenate(parts, axis=0)

    def from_right(x):
        r = pltpu.roll(x, TT - 1, 0)
        parts = []
        for k in range(n_rows):
            parts += [r[k * gw:(k + 1) * gw - 8], jnp.where(last_col[k], 0.0, r[(k + 1) * gw - 8:(k + 1) * gw])]
        return jnp.concatenate(parts, axis=0)

    def conv_chunk(cs):
        above = jnp.where(up_ok, tail_ref[:, cs].astype(F32), 0.0)
        below = jnp.where(dn_ok, a_ref[p, :gw, cs].astype(F32), 0.0)
        mid = a_ref[1 - p, :, cs].astype(F32)
        rows = (jnp.concatenate([above, mid[:TT - gw]], axis=0), mid, jnp.concatenate([mid[gw:], below], axis=0))
        w = [cw_ref[k:k + 1, cs] * (1.0 if k // 3 == 1 else row_w) for k in range(9)]
        taps = [rows[0] * w[dj] + rows[1] * w[3 + dj] + rows[2] * w[6 + dj] for dj in range(3)]
        acc = cb_ref[:, cs] + taps[1] + from_left(taps[0]) + from_right(taps[2])
        return _gelu(acc).astype(BF16) * v_ref[1 - p, :, cs]

    kb = 2 * LANES
    for j in range(D_FF // kb):
        act = jnp.concatenate([conv_chunk(slice(j * kb + i * LANES, j * kb + (i + 1) * LANES)) for i in range(2)],
                              axis=1)
        part = _dot(act, wd_ref[j * kb:(j + 1) * kb, :])
        if j == 0:
            acc_ref[...] = part
        else:
            acc_ref[...] += part
    y = xk_ref[1 - p] + modr_ref[0, 0, 5:6] * acc_ref[...]
    if final:
        y = y * lax.rsqrt(jnp.mean(y * y, axis=-1, keepdims=True) + NORM_EPS) * fg_ref[...]
    out_ref[0] = y


def _mix_ffn_call(o_f, o_b, ret, gla, x0, y_lat, y_ctx, mg, seq_in, modsel, w_branch, w_out,
                  g2, w_up, conv_w, conv_b, w_down, final_g, final):
    split = isinstance(seq_in, tuple)
    has_ctx = y_ctx is not None
    assert has_ctx or not split
    b, s, _ = o_f.shape
    d = D_MODEL
    off = 0 if has_ctx else 1
    n = s // TT - off
    cur = lambda t: jnp.minimum(t, n - 1) + off
    tile = lambda w, j=0: pl.BlockSpec((1, TT, w), lambda i, t: (i, cur(t), j))
    mod_spec = lambda f: pl.BlockSpec((1, 1, 6, d), lambda i, t: (i, jnp.minimum(f(t), 1), 0, 0))
    in_specs = [tile(2 * V_W), tile(2 * V_W), tile(V_W, 2), tile(V_W, 2), tile(D_HY),
                pl.BlockSpec((1, D_HY // FFT_CB, TT // FFT_N2 * FFT_CB, FFT_N2),
                             lambda i, t: (i, 0, jnp.maximum(cur(t) - 1, 0), 0))]
    args = [o_f, o_b, ret, gla, x0, y_lat]
    if has_ctx:
        in_specs.append(pl.BlockSpec((1, D_HY, CTX_LEN), lambda i, t: (i, 0, 0)))
        args.append(y_ctx)
    in_specs.append(tile(3 * D_MODEL))
    args.append(mg)
    if split:
        in_specs += [pl.BlockSpec((1, TT, d), lambda i, t: (i, 0, 0)),
                     pl.BlockSpec((1, TT, d), lambda i, t: (i, jnp.maximum(cur(t) - 1, 0), 0))]
        args += list(seq_in)
    else:
        in_specs.append(tile(d))
        args.append(seq_in)
    in_specs += [mod_spec(cur), mod_spec(lambda t: jnp.maximum(t - 1, 0) + off),
                 _const_spec((3, V_W, d)), _const_spec((d, d)),
                 _const_spec((1, d)), _const_spec((d, 2 * D_FF)),
                 _const_spec((9, D_FF)), _const_spec((1, D_FF)), _const_spec((D_FF, d)), _const_spec((1, d))]
    args += [modsel, modsel, w_branch, w_out, g2, w_up, conv_w, conv_b, w_down, final_g]
    return pl.pallas_call(
        functools.partial(_mix_ffn_kernel, has_ctx=has_ctx, split=split, nt=n, final=final),
        grid=(b, n + 1),
        in_specs=in_specs,
        out_specs=pl.BlockSpec((1, TT, d), lambda i, t: (i, jnp.maximum(t - 1, 0), 0)),
        out_shape=jax.ShapeDtypeStruct((b, n * TT, d), F32),
        scratch_shapes=[pltpu.VMEM((2, TT, D_FF), BF16), pltpu.VMEM((2, TT, D_FF), BF16),
                        pltpu.VMEM((GRID_W, D_FF), BF16), pltpu.VMEM((TT, d), F32),
                        pltpu.VMEM((2, TT, d), F32)],
        compiler_params=_params(2),
        name="mix_ffn",
    )(*args)


def kernel(x, c, ctx, c_ctx, ada_w, ada_b, norm1_g, w_in, gla_wa2, gla_ba, hy_short_w, hy_short_b,
           hy_w1, hy_b1, hy_w2, hy_b2, hy_w3, hy_freq, hy_bias, w_branch, w_out, norm2_g, w_up,
           ffn_conv_w, ffn_conv_b, w_down, final_g):
    bsz, seq, d = x.shape
    depth = ada_w.shape[0]
    assert d == D_MODEL and ctx.shape[1] == CTX_LEN == TT and seq % TT == 0
    assert (2 * seq) % (2 * FFT_N2) == 0 and bsz == 2

    cvec = jnp.zeros((8, d), F32).at[:bsz].set(c).at[bsz].set(c_ctx)
    mod_all = _ada_call(cvec, ada_w, ada_b).reshape(depth, 8, 6, d)

    stream = (ctx, x)
    for l in range(depth):
        last = l == depth - 1
        m = mod_all[l]
        modsel = jnp.stack([jnp.broadcast_to(m[bsz], (bsz, 6, d)), m[:bsz]], axis=1)
        w = w_in[l] * jnp.asarray(_gate_halving())
        w_l, w2 = _gate_weights(w[:, LR_OFF:LR_OFF + 2 * GLA_LOWRANK], gla_wa2[l])
        ret, gla, mg, lga, x0, z_lat, z_ctx = _inproj_call(
            stream, modsel, norm1_g[l].reshape(1, d), w[:, :LR_OFF].astype(BF16),
            w[:, LR_OFF + 2 * GLA_LOWRANK:].astype(BF16), w_l, w2, gla_ba[l].reshape(1, 2 * QK_W),
            hy_short_w[l], hy_short_b[l].reshape(1, 3 * D_HY))
        o_f, o_b = _scan_call(ret, gla, lga)

        filt = (hy_w1[l], hy_b1[l], hy_w2[l], hy_b2[l], hy_w3[l], hy_freq[l])
        y_lat = _fft_conv_call(z_lat, _filter_call(seq, *filt, split=True), hy_bias[l])
        y_ctx = None if last else _dft_conv_call(z_ctx, _filter_call(CTX_LEN, *filt, split=False), hy_bias[l])

        stream = _mix_ffn_call(o_f, o_b, ret, gla, x0, y_lat, y_ctx, mg, stream, modsel,
                               (0.5 * w_branch[l]).astype(BF16), w_out[l].astype(BF16),
                               norm2_g[l].reshape(1, d), w_up[l].astype(BF16),
                               ffn_conv_w[l].reshape(9, D_FF), ffn_conv_b[l].reshape(1, D_FF),
                               w_down[l].astype(BF16), final_g.reshape(1, d), final=last)
    return stream
```

```python
import functools
import math

import numpy as np
import jax
import jax.numpy as jnp
from jax import lax
from jax.experimental import pallas as pl
from jax.experimental.pallas import tpu as pltpu

F32 = jnp.float32
BF16 = jnp.bfloat16
HIGHEST = lax.Precision.HIGHEST

D_MODEL = 1024
CTX_LEN = 256
GRID_W = 64
NORM_EPS = 1e-6
HEAD_NORM_EPS = 1e-5

N_HEADS = 4
DK = 64
DV = 128
QK_W = N_HEADS * DK
V_W = N_HEADS * DV
RET_LOG_DECAY = (tuple(math.log1p(-2.0 ** (-5.0 - h)) for h in range(N_HEADS)),
                 tuple(math.log1p(-2.0 ** (-5.5 - h)) for h in range(N_HEADS)))
GLA_LOWRANK = 16
GLA_GATE_NORM = 16.0
GLA_CHUNK = 64

D_HY = 512
HY_BANDS = 16
HY_EMB = 1 + 2 * HY_BANDS
HY_EMB_PAD = 40
HY_FILTER_WIDTH = 64
HY_INNER = 2
HY_MIN_DECAY = math.log(1e-2) / 1.5
HY_MAX_DECAY = math.log(1e-2) / 0.3
HY_SHIFT = 0.05

D_FF = 2816
D_IN_MAIN = 7680
LR_OFF = 3072

TT = 256
LANES = 128
FFT_N2 = 128
FFT_CB = 16
VMEM_LIMIT = 56 * 1024 * 1024


def _dot(a, b, precision=None):
    return jnp.dot(a, b, preferred_element_type=F32, precision=precision)


def _dot_nt(a, b):
    return lax.dot_general(a, b, (((1,), (1,)), ((), ())), preferred_element_type=F32)


def _dot_tn(a, b):
    return lax.dot_general(a, b, (((0,), (0,)), ((), ())), preferred_element_type=F32)


def _params(n_axes):
    return pltpu.CompilerParams(dimension_semantics=("arbitrary",) * n_axes,
                                vmem_limit_bytes=VMEM_LIMIT)


def _const_spec(shape):
    n = len(shape)
    return pl.BlockSpec(shape, lambda *_: (0,) * n, pipeline_mode=pl.Buffered(1))


def _modulate(x, g, shift, scale):
    y = x * lax.rsqrt(jnp.mean(x * x, axis=-1, keepdims=True) + NORM_EPS) * g
    return y * (1.0 + scale) + shift


def _sigmoid(x):
    return 0.5 * jnp.tanh(0.5 * x) + 0.5


def _silu(x):
    return x * _sigmoid(x)


def _cast_kernel(w_ref, o_ref, *, scale):
    o_ref[...] = (w_ref[...] * scale).astype(BF16) if scale != 1.0 else w_ref[...].astype(BF16)


def _to_bf16(w_stack, l, scale=1.0):
    shape = w_stack.shape[1:]
    cols = shape[-1]
    rows = math.prod(shape[:-1])
    rb = 256
    assert rows % rb == 0 and cols % LANES == 0
    nb = rows // rb
    out = pl.pallas_call(
        functools.partial(_cast_kernel, scale=scale),
        grid=(nb,),
        in_specs=[pl.BlockSpec((rb, cols), lambda i: (l * nb + i, 0))],
        out_specs=pl.BlockSpec((rb, cols), lambda i: (i, 0)),
        out_shape=jax.ShapeDtypeStruct((rows, cols), BF16),
        compiler_params=_params(1),
        name="weight_cast",
    )(w_stack.reshape(-1, cols))
    return out.reshape(shape)


def _ada_kernel(c_ref, w_ref, b_ref, o_ref):
    o_ref[0] = _dot(_silu(c_ref[...]), w_ref[0], HIGHEST) + b_ref[0]


def _ada_call(cvec, ada_w, ada_b):
    depth, d, n = ada_w.shape
    nb = 1536
    return pl.pallas_call(
        _ada_kernel,
        grid=(depth, n // nb),
        in_specs=[pl.BlockSpec((8, d), lambda l, j: (0, 0)),
                  pl.BlockSpec((1, d, nb), lambda l, j: (l, 0, j)),
                  pl.BlockSpec((1, 1, nb), lambda l, j: (l, 0, j))],
        out_specs=pl.BlockSpec((1, 8, nb), lambda l, j: (l, 0, j)),
        out_shape=jax.ShapeDtypeStruct((depth, 8, n), F32),
        compiler_params=_params(2),
        name="ada_ln",
    )(cvec, ada_w, ada_b.reshape(depth, 1, n))


def _log_sigmoid(x):
    return jnp.minimum(x, 0.0) - jnp.log1p(jnp.exp(-jnp.abs(x)))


def _split_bf16(a):
    hi = a.astype(BF16)
    return hi, (a - hi.astype(F32)).astype(BF16)


def _inproj_kernel(first_ref, next_ref, modf_ref, modn_ref, g_ref, wa_ref, wb_ref, wl_ref, w2_ref, ba_ref,
                   sw_ref, sb_ref, ret_ref, gla_ref, mg_ref, lga_ref, x0_ref, zl_ref, zc_ref,
                   hb_ref, cur_ref, prev_ref, *, nt):
    s = pl.program_id(1)
    g = g_ref[...]

    @pl.when(s == 0)
    def _():
        cur_ref[...] = jnp.zeros_like(cur_ref)
        prev_ref[...] = jnp.zeros_like(prev_ref)
        mf = modf_ref[0, 0]
        hb_ref[...] = _modulate(first_ref[0], g, mf[0:1], mf[1:2]).astype(BF16)

    hb = hb_ref[...]
    lr_hi, lr_lo = _split_bf16(_dot(hb, wl_ref[...]))
    gate = _dot(jnp.concatenate([lr_hi, lr_lo, lr_hi], axis=1), w2_ref[...]) + ba_ref[...]
    lga_ref[0] = _log_sigmoid(gate) * (1.0 / GLA_GATE_NORM)
    hy_new = _dot(hb, wb_ref[:, :3 * D_HY])
    zt = _hyena_gates(s - 1, nt, cur_ref, prev_ref, hy_new[0:1], sw_ref, sb_ref, x0_ref)
    prev_ref[...] = cur_ref[TT - 8:, :]
    cur_ref[...] = hy_new
    mg_ref[0] = _dot(hb, wb_ref[:, 3 * D_HY:]).astype(BF16)
    ret_ref[0] = _dot(hb, wa_ref[:, :1536]).astype(BF16)
    gla_ref[0] = _dot(hb, wa_ref[:, 1536:]).astype(BF16)
    mn = modn_ref[0, 0]
    hb_ref[...] = _modulate(next_ref[0], g, mn[0:1], mn[1:2]).astype(BF16)
    _store_z(s - 1, zt, zl_ref, zc_ref)


@functools.lru_cache(maxsize=None)
def _gate_halving():
    scale = np.ones((LR_OFF + 2 * GLA_LOWRANK + 3 * D_HY + 3 * D_MODEL,), np.float32)
    scale[2 * QK_W + V_W:2 * QK_W + 2 * V_W] = 0.5
    scale[4 * QK_W + 3 * V_W:LR_OFF] = 0.5
    scale[LR_OFF + 2 * GLA_LOWRANK + 3 * D_HY:] = 0.5
    return scale


def _gate_weights(w_lr, wa2):
    d = w_lr.shape[0]
    hi, lo = _split_bf16(w_lr)
    w1 = jnp.concatenate([hi, lo, jnp.zeros((d, LANES - 4 * GLA_LOWRANK), BF16)], axis=1)
    bd = jnp.zeros((2 * GLA_LOWRANK, 2 * QK_W), F32)
    bd = bd.at[:GLA_LOWRANK, :QK_W].set(wa2[0]).at[GLA_LOWRANK:, QK_W:].set(wa2[1])
    w2 = jnp.concatenate([bd, bd, jnp.zeros((LANES - 4 * GLA_LOWRANK, 2 * QK_W), F32)], axis=0)
    w2_hi, w2_lo = _split_bf16(w2)
    return w1, jnp.concatenate([w2_hi, w2_hi, w2_lo], axis=0)


def _inproj_call(seq_in, modsel, g1, w_a, w_b, w_l, w2, ba, short_w, short_b):
    split = isinstance(seq_in, tuple)
    if split:
        ctx, x = seq_in
        b, d = x.shape[0], x.shape[2]
        s = ctx.shape[1] + x.shape[1]
    else:
        b, s, d = seq_in.shape
    nt = s // TT
    assert nt >= 2
    cur = lambda t: jnp.minimum(t, nt - 1)
    nxt = lambda t: jnp.minimum(t + 1, nt - 1)
    tile = lambda w: pl.BlockSpec((1, TT, w), lambda i, t: (i, cur(t), 0))
    bf = lambda w: jax.ShapeDtypeStruct((b, s, w), BF16)
    first_spec = pl.BlockSpec((1, TT, d), lambda i, t: (i, 0, 0))
    if split:
        x_specs = [first_spec, pl.BlockSpec((1, TT, d), lambda i, t: (i, nxt(t) - 1, 0))]
        x_args = [ctx, x]
    else:
        x_specs = [first_spec, pl.BlockSpec((1, TT, d), lambda i, t: (i, nxt(t), 0))]
        x_args = [seq_in, seq_in]
    return pl.pallas_call(
        functools.partial(_inproj_kernel, nt=nt),
        grid=(b, nt + 1),
        in_specs=x_specs + [
            pl.BlockSpec((1, 1, 6, d), lambda i, t: (i, 0, 0, 0)),
            pl.BlockSpec((1, 1, 6, d), lambda i, t: (i, 1, 0, 0)),
            _const_spec((1, d)),
            _const_spec(w_a.shape), _const_spec(w_b.shape), _const_spec(w_l.shape),
            _const_spec((3 * LANES, 2 * QK_W)), _const_spec((1, 2 * QK_W)),
            _const_spec((3, 3 * D_HY)), _const_spec((1, 3 * D_HY))],
        out_specs=[tile(1536), tile(1536), tile(3072), tile(2 * QK_W),
                   pl.BlockSpec((1, TT, D_HY), lambda i, t: (i, jnp.maximum(t - 1, 0), 0)),
                   pl.BlockSpec((1, D_HY // FFT_CB, TT // FFT_N2 * FFT_CB, FFT_N2),
                                lambda i, t: (i, 0, jnp.maximum(t - 2, 0), 0)),
                   pl.BlockSpec((1, D_HY, CTX_LEN), lambda i, t: (i, 0, 0))],
        out_shape=[bf(1536), bf(1536), bf(3072), jax.ShapeDtypeStruct((b, s, 2 * QK_W), F32),
                   bf(D_HY),
                   jax.ShapeDtypeStruct((b, D_HY // FFT_CB, (s - CTX_LEN) // FFT_N2 * FFT_CB, FFT_N2), F32),
                   jax.ShapeDtypeStruct((b, D_HY, CTX_LEN), F32)],
        scratch_shapes=[pltpu.VMEM((TT, d), BF16), pltpu.VMEM((TT, 3 * D_HY), F32),
                        pltpu.VMEM((8, 3 * D_HY), F32)],
        compiler_params=_params(2),
        name="in_proj",
    )(*x_args, modsel, modsel, g1, w_a, w_b, w_l, w2, ba, short_w, short_b)


@functools.lru_cache(maxsize=None)
def _scan_tables():
    i = np.arange(TT, dtype=np.float64)
    scale = DK ** -0.5
    diff = i[:, None] - i[None, :]
    same = (i[:, None] // GLA_CHUNK) == (i[None, :] // GLA_CHUNK)
    dmat = np.zeros((2, N_HEADS, TT, TT))
    erow = np.zeros((2, TT, V_W))
    kw = np.zeros((2, TT, QK_W))
    tri = np.zeros((2, TT, 2 * TT))
    for d in range(2):
        for h in range(N_HEADS):
            lg = RET_LOG_DECAY[d][h]
            if d == 0:
                dmat[d, h] = np.where(diff >= 0, np.exp(np.maximum(diff, 0) * lg), 0.0) * scale
                erow[d, :, h * DV:(h + 1) * DV] = np.exp((i + 1) * lg)[:, None]
                kw[d, :, h * DK:(h + 1) * DK] = (np.exp((TT - 1 - i) * lg) * scale)[:, None]
            else:
                dmat[d, h] = np.where(diff < 0, np.exp(np.maximum(-diff, 0) * lg), 0.0) * scale
                erow[d, :, h * DV:(h + 1) * DV] = np.exp((TT - i) * lg)[:, None]
                kw[d, :, h * DK:(h + 1) * DK] = (np.exp(i * lg) * scale)[:, None]
        block = same & ((diff >= 0) if d == 0 else (diff <= 0))
        tri[d] = np.concatenate([block, block], axis=1)
    f = lambda a: a.astype(np.float32)
    return f(dmat), f(erow), f(kw), f(tri)


def _scan_kernel(retf_ref, glaf_ref, lgf_ref, retb_ref, glab_ref, lgb_ref,
                 dmat_ref, erow_ref, kw_ref, tri_ref, of_ref, ob_ref, h_ref, hbd_ref, og_ref):
    @pl.when(pl.program_id(1) == 0)
    def _():
        h_ref[...] = jnp.zeros_like(h_ref)
        hbd_ref[...] = jnp.zeros_like(hbd_ref)

    c = GLA_CHUNK
    n_chunks = TT // c
    row = lax.broadcasted_iota(jnp.int32, (TT, TT), 0)
    col = lax.broadcasted_iota(jnp.int32, (TT, TT), 1)
    same_chunk = (row // c) == (col // c)
    heads = [(slice(hd * DK, (hd + 1) * DK), slice(hd * DV, (hd + 1) * DV)) for hd in range(N_HEADS)]
    def direction(d, ret_ref, gla_ref, lg_ref, o_ref):
        q = ret_ref[0, :, 0:QK_W]
        k = ret_ref[0, :, QK_W:2 * QK_W]
        v = ret_ref[0, :, 2 * QK_W:]
        o_inter = _dot(q, hbd_ref[2 * d]) * erow_ref[d]
        s_new = _dot_tn((k.astype(F32) * kw_ref[d]).astype(BF16), v)
        for hd, (ks, vs) in enumerate(heads):
            s = (_dot_nt(q[:, ks], k[:, ks]) * dmat_ref[d, hd]).astype(BF16)
            o_ref[0, :, vs] = (_dot(s, v[:, vs]) + o_inter[:, vs]).astype(BF16)
            h_new = math.exp(TT * RET_LOG_DECAY[d][hd]) * h_ref[d * 8 + hd] + s_new[ks, vs]
            h_ref[d * 8 + hd] = h_new
            hbd_ref[2 * d, ks, vs] = h_new.astype(BF16)
        yield

        hi, lo = _split_bf16(lg_ref[0])
        cum = _dot(tri_ref[d], jnp.concatenate([hi, lo], axis=0))
        edge = c - 1 if d == 0 else 0
        tot = jnp.concatenate([jnp.broadcast_to(cum[j * c + edge:j * c + edge + 1], (c, QK_W))
                               for j in range(n_chunks)], axis=0)
        a_tot = jnp.exp(tot)
        k_neg = gla_ref[0, :, QK_W:2 * QK_W].astype(F32) * jnp.exp(-cum)
        q_in = (gla_ref[0, :, 0:QK_W].astype(F32) * jnp.exp(cum) * DK ** -0.5).astype(BF16)
        k_out = (k_neg * a_tot).astype(BF16)
        k_neg = k_neg.astype(BF16)
        v = gla_ref[0, :, 2 * QK_W:]
        mask = jnp.logical_and(same_chunk, row >= col if d == 0 else row < col)
        yield
        for hd, (ks, vs) in enumerate(heads):
            s = jnp.where(mask, _dot_nt(q_in[:, ks], k_neg[:, ks]), 0.0).astype(BF16)
            og_ref[d, :, vs] = _dot(s, v[:, vs])
        for j in (range(n_chunks) if d == 0 else range(n_chunks - 1, -1, -1)):
            yield
            rows = slice(j * c, (j + 1) * c)
            inter = _dot(q_in[rows], hbd_ref[2 * d + 1])
            o_ref[0, rows, V_W:] = (og_ref[d, rows, :] + inter).astype(BF16)
            s_j = _dot_tn(k_out[rows], v[rows])
            a_col = jnp.broadcast_to(a_tot[j * c:j * c + 1], (DV, QK_W)).T
            for hd, (ks, vs) in enumerate(heads):
                h_new = a_col[ks] * h_ref[d * 8 + 4 + hd] + s_j[ks, vs]
                h_ref[d * 8 + 4 + hd] = h_new
                hbd_ref[2 * d + 1, ks, vs] = h_new.astype(BF16)

    pending = [direction(0, retf_ref, glaf_ref, lgf_ref, of_ref), direction(1, retb_ref, glab_ref, lgb_ref, ob_ref)]
    while pending:
        pending = [g for g in pending if next(g, True) is None]


def _scan_call(ret, gla, lga):
    b, s, _ = ret.shape
    nt = s // TT
    dmat, erow, kw, tri = (jnp.asarray(z) for z in _scan_tables())
    tri = tri.astype(BF16)
    fwd = lambda i, t: (i, t, 0)
    bwd = lambda i, t: (i, jnp.where(t == 0, 0, nt - t), 0)
    bwd_g = lambda i, t: (i, jnp.where(t == 0, 0, nt - t), 1)
    qkv = lambda f: pl.BlockSpec((1, TT, 2 * QK_W + V_W), f)
    return pl.pallas_call(
        _scan_kernel,
        grid=(b, nt),
        in_specs=[qkv(fwd), qkv(fwd), pl.BlockSpec((1, TT, QK_W), fwd),
                  qkv(bwd), qkv(bwd), pl.BlockSpec((1, TT, QK_W), bwd_g),
                  _const_spec(dmat.shape), _const_spec(erow.shape), _const_spec(kw.shape),
                  _const_spec(tri.shape)],
        out_specs=[pl.BlockSpec((1, TT, 2 * V_W), fwd), pl.BlockSpec((1, TT, 2 * V_W), bwd)],
        out_shape=[jax.ShapeDtypeStruct((b, s, 2 * V_W), BF16)] * 2,
        scratch_shapes=[pltpu.VMEM((16, DK, DV), F32), pltpu.VMEM((4, QK_W, V_W), BF16),
                        pltpu.VMEM((2, TT, V_W), F32)],
        compiler_params=_params(2),
        name="bidir_scan",
    )(ret, gla, lga, ret, gla, lga, dmat, erow, kw, tri)


def _hyena_gates(t, nt, cur_ref, prev_ref, next_row, w_ref, b_ref, x0_ref):
    has_prev = t >= 2
    has_next = jnp.logical_and(t >= 1, t <= nt - 2)
    ridx = lax.broadcasted_iota(jnp.int32, (TT, LANES), 0)
    u = []
    for part in range(3):
        cols = []
        for j in range(D_HY // LANES):
            cs = slice(part * D_HY + j * LANES, part * D_HY + (j + 1) * LANES)
            p = cur_ref[:, cs]
            pv = jnp.where(has_prev, prev_ref[7:8, cs], 0.0)
            nx = jnp.where(has_next, next_row[:, cs], 0.0)
            up = jnp.where(ridx == 0, pv, pltpu.roll(p, 1, 0))
            dn = jnp.where(ridx == TT - 1, nx, pltpu.roll(p, TT - 1, 0))
            cols.append(b_ref[:, cs] + up * w_ref[0:1, cs] + p * w_ref[1:2, cs] + dn * w_ref[2:3, cs])
        u.append(cols)
    x0_ref[0] = jnp.concatenate(u[0], axis=1).astype(BF16)
    z = jnp.concatenate([a * c for a, c in zip(u[1], u[2])], axis=1)
    return z.T


def _store_z(t, zt, zl_ref, zc_ref):
    @pl.when(t == 0)
    def _():
        zc_ref[0] = zt

    @pl.when(t > 0)
    def _():
        for r in range(TT // FFT_N2):
            zl_ref[0, :, r * FFT_CB:(r + 1) * FFT_CB, :] = _to_conv_rows(zt[:, r * FFT_N2:(r + 1) * FFT_N2])


@functools.lru_cache(maxsize=None)
def _filter_positions(length):
    n = np.arange(2 * length)
    m = np.where(n < length, n, 2 * length - n).astype(np.float64)
    m = np.where(n == length, 0.0, m)
    t = m / (length - 1)
    bands = np.linspace(1e-4, HY_BANDS - 1.0, HY_BANDS)
    ang = (2.0 * math.pi / length) * m[None, :] * bands[:, None]
    z = np.zeros((HY_EMB_PAD, 2 * length))
    z[0] = t
    z[1:1 + HY_BANDS] = np.cos(ang)
    z[1 + HY_BANDS:HY_EMB] = -np.sin(ang)
    return z.astype(np.float32)


def _filter_kernel(z_ref, w1_ref, b1_ref, w2_ref, b2_ref, w3_ref, fr_ref, dl_ref, k_ref, *, length, pb, split):
    z = z_ref[...]
    fr = fr_ref[...]
    hdn = jnp.sin(fr * (_dot(w1_ref[...], z, HIGHEST) + b1_ref[...]))
    for i in range(HY_INNER):
        hdn = jnp.sin(fr * (_dot(w2_ref[i], hdn, HIGHEST) + b2_ref[i]))
    h = _dot(w3_ref[0], hdn, HIGHEST)
    window = jnp.exp(-dl_ref[...] * z[0:1]) + HY_SHIFT
    pos = pl.program_id(0) * pb + lax.broadcasted_iota(jnp.int32, (1, pb), 1)
    k = jnp.where(pos == length, 0.0, h * window)
    if split:
        for r in range(pb // FFT_N2):
            k_ref[:, r * FFT_CB:(r + 1) * FFT_CB, :] = _to_conv_rows(k[:, r * FFT_N2:(r + 1) * FFT_N2])
    else:
        k_ref[...] = k


def _filter_call(length, w1, b1, w2, b2, w3, freq, split):
    pb = min(2048, length)
    if split:
        out_spec = pl.BlockSpec((D_HY // FFT_CB, pb // FFT_N2 * FFT_CB, FFT_N2), lambda j: (0, j, 0))
        out_shape = jax.ShapeDtypeStruct((D_HY // FFT_CB, 2 * length // FFT_N2 * FFT_CB, FFT_N2), F32)
    else:
        out_spec = pl.BlockSpec((D_HY, pb), lambda j: (0, j))
        out_shape = jax.ShapeDtypeStruct((D_HY, 2 * length), F32)
    zf = jnp.asarray(_filter_positions(length))
    w1t = jnp.pad(w1.T, ((0, 0), (0, HY_EMB_PAD - HY_EMB)))
    w2t = jnp.swapaxes(w2, 1, 2)
    w3t = w3.T.reshape(2, D_HY, HY_FILTER_WIDTH)
    col = lambda a: a.reshape(a.shape + (1,))
    deltas = np.abs(np.linspace(HY_MIN_DECAY, HY_MAX_DECAY, D_HY)).astype(np.float32)
    fw = HY_FILTER_WIDTH
    return pl.pallas_call(
        functools.partial(_filter_kernel, length=length, pb=pb, split=split),
        grid=(2 * length // pb,),
        in_specs=[pl.BlockSpec((HY_EMB_PAD, pb), lambda j: (0, j)),
                  _const_spec((fw, HY_EMB_PAD)), _const_spec((fw, 1)),
                  _const_spec((HY_INNER, fw, fw)), _const_spec((HY_INNER, fw, 1)),
                  pl.BlockSpec((1, D_HY, fw), lambda j: (jnp.where(j >= length // pb, 1, 0), 0, 0)),
                  _const_spec((fw, 1)), _const_spec((D_HY, 1))],
        out_specs=out_spec,
        out_shape=out_shape,
        compiler_params=_params(1),
        name="hyena_filter",
    )(zf, w1t, col(b1), w2t, col(b2), w3t, col(freq), jnp.asarray(deltas).reshape(D_HY, 1))


def _real_block(w):
    return np.block([[w.real, w.imag], [-w.imag, w.real]])


@functools.lru_cache(maxsize=None)
def _fft_tables(length):
    n = 2 * length
    n2 = FFT_N2
    n1 = n // n2
    f1 = np.arange(n1)
    wa = np.exp(-2j * np.pi * np.outer(f1, np.arange(n1)) / n1)
    stage_a = np.concatenate([wa.real, wa.imag], axis=0)
    tw = np.exp(-2j * np.pi * np.outer(f1, np.arange(n2)) / n)
    wb = np.exp(-2j * np.pi * np.outer(np.arange(n2), np.arange(n2)) / n2)
    wh = wa[:, :n1 // 2]
    a_pair = np.block([[wh.real, -wh.imag], [wh.imag, wh.real]])
    wi = np.conj(wa[:n1 // 2]) / n
    a_inv = np.block([[wi.real, -wi.imag], [wi.imag, wi.real]])
    f = lambda a: a.astype(np.float32)
    return dict(a_pair=f(a_pair), a_k=f(stage_a), tw_re=f(tw.real), tw_im=f(tw.imag),
                b_fwd=f(_real_block(wb)), b_inv=f(_real_block(np.conj(wb))), a_inv=f(a_inv))


def _fft_conv_kernel(bias_ref, z_ref, k_ref, ap_ref, ak_ref, twr_ref, twi_ref, bf_ref, bi_ref, ai_ref,
                     y_ref, sz_ref, sk_ref, sv_ref):
    n1, n2 = twr_ref.shape
    h, cb = n1 // 2, FFT_CB
    z0, z1 = (z_ref.at[b, 0] for b in range(2))
    y0, y1 = (y_ref.at[b, 0] for b in range(2))
    kk = k_ref.at[0]
    twr = twr_ref[...]
    twi = twi_ref[...]

    def twiddled(a):
        ar, ai = a[:n1], a[n1:]
        return jnp.concatenate([ar * twr - ai * twi, ar * twi + ai * twr], axis=1).astype(BF16)

    def stage_a(p):
        pair = (2 * p, 2 * p + 1)
        zc = jnp.concatenate([jnp.concatenate([z0[pl.ds(c, h, stride=cb), :], z1[pl.ds(c, h, stride=cb), :]], axis=0)
                              for c in pair], axis=1)
        kc = jnp.concatenate([kk[pl.ds(c, n1, stride=cb), :] for c in pair], axis=1)
        az = _dot(ap_ref[...], zc.astype(BF16))
        ak = _dot(ak_ref[...], kc.astype(BF16))
        for i, c in enumerate(pair):
            rows = slice(c * n1, (c + 1) * n1)
            sz_ref[rows, :] = twiddled(az[:, i * n2:(i + 1) * n2])
            sk_ref[rows, :] = twiddled(ak[:, i * n2:(i + 1) * n2])

    gb = 4

    def stage_b(g):
        rows = slice(g * gb * n1, (g + 1) * gb * n1)
        xs = _dot(sz_ref[rows, :], bf_ref[...])
        ks = _dot(sk_ref[rows, :], bf_ref[...])
        xr, xi, kr, ki = xs[:, :n2], xs[:, n2:], ks[:, :n2], ks[:, n2:]
        ys = jnp.concatenate([xr * kr - xi * ki, xr * ki + xi * kr], axis=1).astype(BF16)
        u = _dot(ys, bi_ref[...])
        for i in range(gb):
            ur, ui = u[i * n1:(i + 1) * n1, :n2], u[i * n1:(i + 1) * n1, n2:]
            lanes = slice((i % 2) * n2, (i % 2 + 1) * n2)
            sv_ref[g * (gb // 2) + i // 2, :n1, lanes] = (ur * twr + ui * twi).astype(BF16)
            sv_ref[g * (gb // 2) + i // 2, n1:, lanes] = (ui * twr - ur * twi).astype(BF16)

    def stage_c(p):
        y = _dot(ai_ref[...], sv_ref[p])
        for i in range(2):
            c = 2 * p + i
            bias = bias_ref[pl.program_id(0) * cb + c]
            sel = pl.ds(c, h, stride=cb)
            y0[sel, :] = y[:h, i * n2:(i + 1) * n2] + z0[sel, :] * bias
            y1[sel, :] = y[h:, i * n2:(i + 1) * n2] + z1[sel, :] * bias

    n_groups = cb // gb
    for step in range(n_groups + 2):
        if step < n_groups:
            for p in range(step * gb // 2, (step + 1) * gb // 2):
                stage_a(p)
        if 0 <= step - 1 < n_groups:
            stage_b(step - 1)
        if 0 <= step - 2 < n_groups:
            for p in range((step - 2) * gb // 2, (step - 1) * gb // 2):
                stage_c(p)


def _to_conv_rows(a):
    return a.reshape(a.shape[0] // FFT_CB, FFT_CB, a.shape[1])


def _fft_conv_call(zq, kq, bias):
    cb = FFT_CB
    b, nblk, rows, n2 = zq.shape
    assert b == 2 and n2 == FFT_N2
    h, ch = rows // cb, nblk * cb
    n1 = 2 * h
    tb = _fft_tables(h * n2)
    bf = lambda name: jnp.asarray(tb[name]).astype(BF16)
    consts = [bf("a_pair"), bf("a_k"), jnp.asarray(tb["tw_re"]), jnp.asarray(tb["tw_im"]),
              bf("b_fwd"), bf("b_inv"), bf("a_inv")]
    return pl.pallas_call(
        _fft_conv_kernel,
        grid=(nblk,),
        in_specs=[pl.BlockSpec(memory_space=pltpu.SMEM),
                  pl.BlockSpec((b, 1, h * cb, n2), lambda j: (0, j, 0, 0)),
                  pl.BlockSpec((1, n1 * cb, n2), lambda j: (j, 0, 0))]
                 + [_const_spec(a.shape) for a in consts],
        out_specs=pl.BlockSpec((b, 1, h * cb, n2), lambda j: (0, j, 0, 0)),
        out_shape=jax.ShapeDtypeStruct(zq.shape, F32),
        scratch_shapes=[pltpu.VMEM((cb * n1, 2 * n2), BF16), pltpu.VMEM((cb * n1, 2 * n2), BF16),
                        pltpu.VMEM((cb // 2, 2 * n1, 2 * n2), BF16)],
        compiler_params=_params(1),
        name="hyena_long_conv",
    )(bias, zq, kq, *consts)


@functools.lru_cache(maxsize=None)
def _dft_tables(length):
    n = 2 * length
    w = np.exp(-2j * np.pi * np.outer(np.arange(n), np.arange(n)) / n)
    fwd = np.concatenate([w.real, w.imag], axis=1)
    inv = np.concatenate([w.real[:, :length], w.imag[:, :length]], axis=0) / n
    return fwd[:length].astype(np.float32), fwd.astype(np.float32), inv.astype(np.float32)


def _dft_conv_kernel(z_ref, k_ref, bias_ref, fz_ref, fk_ref, fi_ref, y_ref):
    n = k_ref.shape[1]
    ks = _dot(k_ref[...], fk_ref[...], HIGHEST)
    kr, ki = ks[:, :n], ks[:, n:]
    for b in range(z_ref.shape[0]):
        z = z_ref[b]
        xs = _dot(z, fz_ref[...], HIGHEST)
        xr, xi = xs[:, :n], xs[:, n:]
        ys = jnp.concatenate([xr * kr - xi * ki, xr * ki + xi * kr], axis=1)
        y_ref[b] = _dot(ys, fi_ref[...], HIGHEST) + z * bias_ref[...]


def _dft_conv_call(zt, kt, bias):
    b, ch, length = zt.shape
    fz, fk, fi = (jnp.asarray(a) for a in _dft_tables(length))
    return pl.pallas_call(
        _dft_conv_kernel,
        out_shape=jax.ShapeDtypeStruct((b, ch, length), F32),
        compiler_params=pltpu.CompilerParams(vmem_limit_bytes=VMEM_LIMIT),
        name="hyena_ctx_conv",
    )(zt, kt, bias.reshape(ch, 1), fz, fk, fi)


def _head_norm(o):
    mu = jnp.mean(o, axis=-1, keepdims=True)
    var = jnp.mean(jnp.square(o - mu), axis=-1, keepdims=True)
    return (o - mu) * lax.rsqrt(var + HEAD_NORM_EPS)


def _merge_mix(of_ref, ob_ref, rg_ref, gr_ref, x0_ref, yl_ref, yc_ref, mg_ref, wb_ref, wo_ref, is_ctx):
    o = of_ref[0].astype(F32) + ob_ref[0].astype(F32)
    mixed = None
    for m, gate_ref in enumerate((rg_ref, gr_ref)):
        heads = [_head_norm(o[:, m * V_W + hd * DV:m * V_W + (hd + 1) * DV]) for hd in range(N_HEADS)]
        gh = gate_ref[0].astype(F32)
        br = jnp.concatenate(heads, axis=1) * (gh * (1.0 + jnp.tanh(gh)))
        g2 = 1.0 + jnp.tanh(mg_ref[0, :, m * D_MODEL:(m + 1) * D_MODEL].astype(F32))
        term = g2 * _dot(br.astype(BF16), wb_ref[m])
        mixed = term if mixed is None else mixed + term
    yt = jnp.concatenate([yl_ref[0, :, r * FFT_CB:(r + 1) * FFT_CB, :].reshape(D_HY, FFT_N2)
                          for r in range(TT // FFT_N2)], axis=1)
    if yc_ref is not None:
        yt = jnp.where(is_ctx, yc_ref[0], yt)
    hy = x0_ref[0].astype(F32) * yt.T
    g2 = 1.0 + jnp.tanh(mg_ref[0, :, 2 * D_MODEL:3 * D_MODEL].astype(F32))
    mixed = mixed + g2 * _dot(hy.astype(BF16), wb_ref[2])
    return _dot(mixed.astype(BF16), wo_ref[...])


def _gelu(x):
    return 0.5 * x * (1.0 + lax.erf(x * (2.0 ** -0.5)))


def _mix_ffn_kernel(*refs, has_ctx, split, nt, final):
    refs = list(refs)
    mixers = refs[:6]
    del refs[:6]
    yc_ref = refs.pop(0) if has_ctx else None
    mg_ref = refs.pop(0)
    ctx_ref = refs.pop(0) if split else None
    (x_ref, modu_ref, modr_ref, wb_ref, wo_ref, g_ref, wu_ref, cw_ref, cb_ref, wd_ref, fg_ref,
     out_ref, a_ref, v_ref, tail_ref, acc_ref, xk_ref) = refs
    s = pl.program_id(1)
    p = lax.rem(s, 2)
    gw = GRID_W

    @pl.when(s == 0)
    def _():
        a_ref[...] = jnp.zeros_like(a_ref)
        v_ref[...] = jnp.zeros_like(v_ref)
        xk_ref[...] = jnp.zeros_like(xk_ref)

    tail_ref[...] = a_ref[p, TT - gw:, :]
    mu = modu_ref[0, 0]
    x = x_ref[0]
    if split:
        x = jnp.where(s == 0, ctx_ref[0], x)
    x1 = x + mu[2:3] * _merge_mix(*mixers, yc_ref, mg_ref, wb_ref, wo_ref, s == 0)
    xk_ref[p] = x1
    hb = _modulate(x1, g_ref[...], mu[3:4], mu[4:5]).astype(BF16)
    a_ref[p] = _dot(hb, wu_ref[:, :D_FF]).astype(BF16)
    v_ref[p] = _dot(hb, wu_ref[:, D_FF:]).astype(BF16)

    t = s - 1
    first_lat = 1 if has_ctx else 0
    is_ctx = t < first_lat
    up_ok = t >= first_lat + 1
    dn_ok = jnp.logical_and(t >= first_lat, t <= nt - 2)
    row_w = jnp.where(is_ctx, 0.0, 1.0)
    sub = lax.broadcasted_iota(jnp.int32, (8, LANES), 0)
    n_rows = TT // gw
    first_col = [jnp.logical_and(sub == 0, jnp.logical_or(k == 0, jnp.logical_not(is_ctx))) for k in range(n_rows)]
    last_col = [jnp.logical_and(sub == 7, jnp.logical_or(k == n_rows - 1, jnp.logical_not(is_ctx)))
                for k in range(n_rows)]

    def from_left(x):
        r = pltpu.roll(x, 1, 0)
        parts = []
        for k in range(n_rows):
            parts += [jnp.where(first_col[k], 0.0, r[k * gw:k * gw + 8]), r[k * gw + 8:(k + 1) * gw]]
        return jnp.concatenate(parts, axis=0)

    def from_right(x):
        r = pltpu.roll(x, TT - 1, 0)
        parts = []
        for k in range(n_rows):
            parts += [r[k * gw:(k + 1) * gw - 8], jnp.where(last_col[k], 0.0, r[(k + 1) * gw - 8:(k + 1) * gw])]
        return jnp.concatenate(parts, axis=0)

    def conv_chunk(cs):
        above = jnp.where(up_ok, tail_ref[:, cs].astype(F32), 0.0)
        below = jnp.where(dn_ok, a_ref[p, :gw, cs].astype(F32), 0.0)
        mid = a_ref[1 - p, :, cs].astype(F32)
        rows = (jnp.concatenate([above, mid[:TT - gw]], axis=0), mid, jnp.concatenate([mid[gw:], below], axis=0))
        w = [cw_ref[k:k + 1, cs] * (1.0 if k // 3 == 1 else row_w) for k in range(9)]
        taps = [rows[0] * w[dj] + rows[1] * w[3 + dj] + rows[2] * w[6 + dj] for dj in range(3)]
        acc = cb_ref[:, cs] + taps[1] + from_left(taps[0]) + from_right(taps[2])
        return _gelu(acc).astype(BF16) * v_ref[1 - p, :, cs]

    kb = 2 * LANES
    for j in range(D_FF // kb):
        act = jnp.concatenate([conv_chunk(slice(j * kb + i * LANES, j * kb + (i + 1) * LANES)) for i in range(2)],
                              axis=1)
        part = _dot(act, wd_ref[j * kb:(j + 1) * kb, :])
        if j == 0:
            acc_ref[...] = part
        else:
            acc_ref[...] += part
    y = xk_ref[1 - p] + modr_ref[0, 0, 5:6] * acc_ref[...]
    if final:
        y = y * lax.rsqrt(jnp.mean(y * y, axis=-1, keepdims=True) + NORM_EPS) * fg_ref[...]
    out_ref[0] = y


def _mix_ffn_call(o_f, o_b, ret, gla, x0, y_lat, y_ctx, mg, seq_in, modsel, w_branch, w_out,
                  g2, w_up, conv_w, conv_b, w_down, final_g, final):
    split = isinstance(seq_in, tuple)
    has_ctx = y_ctx is not None
    assert has_ctx or not split
    b, s, _ = o_f.shape
    d = D_MODEL
    off = 0 if has_ctx else 1
    n = s // TT - off
    cur = lambda t: jnp.minimum(t, n - 1) + off
    tile = lambda w, j=0: pl.BlockSpec((1, TT, w), lambda i, t: (i, cur(t), j))
    mod_spec = lambda f: pl.BlockSpec((1, 1, 6, d), lambda i, t: (i, jnp.minimum(f(t), 1), 0, 0))
    in_specs = [tile(2 * V_W), tile(2 * V_W), tile(V_W, 2), tile(V_W, 2), tile(D_HY),
                pl.BlockSpec((1, D_HY // FFT_CB, TT // FFT_N2 * FFT_CB, FFT_N2),
                             lambda i, t: (i, 0, jnp.maximum(cur(t) - 1, 0), 0))]
    args = [o_f, o_b, ret, gla, x0, y_lat]
    if has_ctx:
        in_specs.append(pl.BlockSpec((1, D_HY, CTX_LEN), lambda i, t: (i, 0, 0)))
        args.append(y_ctx)
    in_specs.append(tile(3 * D_MODEL))
    args.append(mg)
    if split:
        in_specs += [pl.BlockSpec((1, TT, d), lambda i, t: (i, 0, 0)),
                     pl.BlockSpec((1, TT, d), lambda i, t: (i, jnp.maximum(cur(t) - 1, 0), 0))]
        args += list(seq_in)
    else:
        in_specs.append(tile(d))
        args.append(seq_in)
    in_specs += [mod_spec(cur), mod_spec(lambda t: jnp.maximum(t - 1, 0) + off),
                 _const_spec((3, V_W, d)), _const_spec((d, d)),
                 _const_spec((1, d)), _const_spec((d, 2 * D_FF)),
                 _const_spec((9, D_FF)), _const_spec((1, D_FF)), _const_spec((D_FF, d)), _const_spec((1, d))]
    args += [modsel, modsel, w_branch, w_out, g2, w_up, conv_w, conv_b, w_down, final_g]
    return pl.pallas_call(
        functools.partial(_mix_ffn_kernel, has_ctx=has_ctx, split=split, nt=n, final=final),
        grid=(b, n + 1),
        in_specs=in_specs,
        out_specs=pl.BlockSpec((1, TT, d), lambda i, t: (i, jnp.maximum(t - 1, 0), 0)),
        out_shape=jax.ShapeDtypeStruct((b, n * TT, d), F32),
        scratch_shapes=[pltpu.VMEM((2, TT, D_FF), BF16), pltpu.VMEM((2, TT, D_FF), BF16),
                        pltpu.VMEM((GRID_W, D_FF), BF16), pltpu.VMEM((TT, d), F32),
                        pltpu.VMEM((2, TT, d), F32)],
        compiler_params=_params(2),
        name="mix_ffn",
    )(*args)


def kernel(x, c, ctx, c_ctx, ada_w, ada_b, norm1_g, w_in, gla_wa2, gla_ba, hy_short_w, hy_short_b,
           hy_w1, hy_b1, hy_w2, hy_b2, hy_w3, hy_freq, hy_bias, w_branch, w_out, norm2_g, w_up,
           ffn_conv_w, ffn_conv_b, w_down, final_g):
    bsz, seq, d = x.shape
    depth = ada_w.shape[0]
    assert d == D_MODEL and ctx.shape[1] == CTX_LEN == TT and seq % TT == 0
    assert (2 * seq) % (2 * FFT_N2) == 0 and bsz == 2

    cvec = jnp.zeros((8, d), F32).at[:bsz].set(c).at[bsz].set(c_ctx)
    mod_all = _ada_call(cvec, ada_w, ada_b).reshape(depth, 8, 6, d)

    stream = (ctx, x)
    for l in range(depth):
        last = l == depth - 1
        m = mod_all[l]
        modsel = jnp.stack([jnp.broadcast_to(m[bsz], (bsz, 6, d)), m[:bsz]], axis=1)
        w = w_in[l]
        half = jnp.asarray(_gate_halving())
        lo, hi = LR_OFF, LR_OFF + 2 * GLA_LOWRANK
        w_l, w2 = _gate_weights(w[:, lo:hi], gla_wa2[l])
        ret, gla, mg, lga, x0, z_lat, z_ctx = _inproj_call(
            stream, modsel, norm1_g[l].reshape(1, d), (w[:, :lo] * half[:lo]).astype(BF16),
            (w[:, hi:] * half[hi:]).astype(BF16), w_l, w2, gla_ba[l].reshape(1, 2 * QK_W),
            hy_short_w[l], hy_short_b[l].reshape(1, 3 * D_HY))
        o_f, o_b = _scan_call(ret, gla, lga)

        filt = (hy_w1[l], hy_b1[l], hy_w2[l], hy_b2[l], hy_w3[l], hy_freq[l])
        y_lat = _fft_conv_call(z_lat, _filter_call(seq, *filt, split=True), hy_bias[l])
        y_ctx = None if last else _dft_conv_call(z_ctx, _filter_call(CTX_LEN, *filt, split=False), hy_bias[l])

        stream = _mix_ffn_call(o_f, o_b, ret, gla, x0, y_lat, y_ctx, mg, stream, modsel,
                               _to_bf16(w_branch, l, 0.5), _to_bf16(w_out, l),
                               norm2_g[l].reshape(1, d), _to_bf16(w_up, l),
                               ffn_conv_w[l].reshape(9, D_FF), ffn_conv_b[l].reshape(1, D_FF),
                               _to_bf16(w_down, l), final_g.reshape(1, d), final=last)
    return stream
```

```python
import functools
import math

import numpy as np
import jax
import jax.numpy as jnp
from jax import lax
from jax.experimental import pallas as pl
from jax.experimental.pallas import tpu as pltpu

F32 = jnp.float32
BF16 = jnp.bfloat16
HIGHEST = lax.Precision.HIGHEST

D_MODEL = 1024
CTX_LEN = 256
GRID_W = 64
NORM_EPS = 1e-6
HEAD_NORM_EPS = 1e-5

N_HEADS = 4
DK = 64
DV = 128
QK_W = N_HEADS * DK
V_W = N_HEADS * DV
RET_LOG_DECAY = (tuple(math.log1p(-2.0 ** (-5.0 - h)) for h in range(N_HEADS)),
                 tuple(math.log1p(-2.0 ** (-5.5 - h)) for h in range(N_HEADS)))
GLA_LOWRANK = 16
GLA_GATE_NORM = 16.0
GLA_CHUNK = 64

D_HY = 512
HY_BANDS = 16
HY_EMB = 1 + 2 * HY_BANDS
HY_EMB_PAD = 40
HY_FILTER_WIDTH = 64
HY_INNER = 2
HY_MIN_DECAY = math.log(1e-2) / 1.5
HY_MAX_DECAY = math.log(1e-2) / 0.3
HY_SHIFT = 0.05

D_FF = 2816
D_IN_MAIN = 7680
LR_OFF = 3072

TT = 256
LANES = 128
FFT_N2 = 128
FFT_CB = 16
VMEM_LIMIT = 56 * 1024 * 1024


def _dot(a, b, precision=None):
    return jnp.dot(a, b, preferred_element_type=F32, precision=precision)


def _dot_nt(a, b):
    return lax.dot_general(a, b, (((1,), (1,)), ((), ())), preferred_element_type=F32)


def _dot_tn(a, b):
    return lax.dot_general(a, b, (((0,), (0,)), ((), ())), preferred_element_type=F32)


def _params(n_axes):
    return pltpu.CompilerParams(dimension_semantics=("arbitrary",) * n_axes,
                                vmem_limit_bytes=VMEM_LIMIT)


def _const_spec(shape):
    n = len(shape)
    return pl.BlockSpec(shape, lambda *_: (0,) * n, pipeline_mode=pl.Buffered(1))


def _modulate(x, g, shift, scale):
    y = x * lax.rsqrt(jnp.mean(x * x, axis=-1, keepdims=True) + NORM_EPS) * g
    return y * (1.0 + scale) + shift


def _sigmoid(x):
    return 0.5 * jnp.tanh(0.5 * x) + 0.5


def _silu(x):
    return x * _sigmoid(x)


def _cast_kernel(w_ref, o_ref, *, scale):
    o_ref[...] = (w_ref[...] * scale).astype(BF16) if scale != 1.0 else w_ref[...].astype(BF16)


def _to_bf16(w_stack, l, scale=1.0):
    shape = w_stack.shape[1:]
    cols = shape[-1]
    rows = math.prod(shape[:-1])
    rb = 256
    assert rows % rb == 0 and cols % LANES == 0
    nb = rows // rb
    out = pl.pallas_call(
        functools.partial(_cast_kernel, scale=scale),
        grid=(nb,),
        in_specs=[pl.BlockSpec((rb, cols), lambda i: (l * nb + i, 0))],
        out_specs=pl.BlockSpec((rb, cols), lambda i: (i, 0)),
        out_shape=jax.ShapeDtypeStruct((rows, cols), BF16),
        compiler_params=_params(1),
        name="weight_cast",
    )(w_stack.reshape(-1, cols))
    return out.reshape(shape)


def _ada_kernel(c_ref, w_ref, b_ref, o_ref):
    o_ref[0] = _dot(_silu(c_ref[...]), w_ref[0], HIGHEST) + b_ref[0]


def _ada_call(cvec, ada_w, ada_b):
    depth, d, n = ada_w.shape
    nb = 1536
    return pl.pallas_call(
        _ada_kernel,
        grid=(depth, n // nb),
        in_specs=[pl.BlockSpec((8, d), lambda l, j: (0, 0)),
                  pl.BlockSpec((1, d, nb), lambda l, j: (l, 0, j)),
                  pl.BlockSpec((1, 1, nb), lambda l, j: (l, 0, j))],
        out_specs=pl.BlockSpec((1, 8, nb), lambda l, j: (l, 0, j)),
        out_shape=jax.ShapeDtypeStruct((depth, 8, n), F32),
        compiler_params=_params(2),
        name="ada_ln",
    )(cvec, ada_w, ada_b.reshape(depth, 1, n))


def _log_sigmoid(x):
    return jnp.minimum(x, 0.0) - jnp.log1p(jnp.exp(-jnp.abs(x)))


def _split_bf16(a):
    hi = a.astype(BF16)
    return hi, (a - hi.astype(F32)).astype(BF16)


def _inproj_kernel(first_ref, next_ref, modf_ref, modn_ref, g_ref, wa_ref, wb_ref, wl_ref, w2_ref, ba_ref,
                   sw_ref, sb_ref, ret_ref, gla_ref, mg_ref, lga_ref, x0_ref, zl_ref, zc_ref,
                   hb_ref, cur_ref, prev_ref, *, nt):
    s = pl.program_id(1)
    g = g_ref[...]

    @pl.when(s == 0)
    def _():
        cur_ref[...] = jnp.zeros_like(cur_ref)
        prev_ref[...] = jnp.zeros_like(prev_ref)
        mf = modf_ref[0, 0]
        hb_ref[...] = _modulate(first_ref[0], g, mf[0:1], mf[1:2]).astype(BF16)

    hb = hb_ref[...]
    lr_hi, lr_lo = _split_bf16(_dot(hb, wl_ref[...]))
    gate = _dot(jnp.concatenate([lr_hi, lr_lo, lr_hi], axis=1), w2_ref[...]) + ba_ref[...]
    lga_ref[0] = _log_sigmoid(gate) * (1.0 / GLA_GATE_NORM)
    hy_new = _dot(hb, wb_ref[:, :3 * D_HY])
    mg_ref[0] = _dot(hb, wb_ref[:, 3 * D_HY:]).astype(BF16)
    ret_ref[0] = _dot(hb, wa_ref[:, :1536]).astype(BF16)
    gla_ref[0] = _dot(hb, wa_ref[:, 1536:]).astype(BF16)
    zt = _hyena_gates(s - 1, nt, cur_ref, prev_ref, hy_new[0:1], sw_ref, sb_ref, x0_ref)
    prev_ref[...] = cur_ref[TT - 8:, :]
    cur_ref[...] = hy_new
    mn = modn_ref[0, 0]
    hb_ref[...] = _modulate(next_ref[0], g, mn[0:1], mn[1:2]).astype(BF16)
    _store_z(s - 1, zt, zl_ref, zc_ref)


@functools.lru_cache(maxsize=None)
def _gate_halving():
    scale = np.ones((LR_OFF + 2 * GLA_LOWRANK + 3 * D_HY + 3 * D_MODEL,), np.float32)
    scale[2 * QK_W + V_W:2 * QK_W + 2 * V_W] = 0.5
    scale[4 * QK_W + 3 * V_W:LR_OFF] = 0.5
    scale[LR_OFF + 2 * GLA_LOWRANK + 3 * D_HY:] = 0.5
    return scale


def _gate_weights(w_lr, wa2):
    d = w_lr.shape[0]
    hi, lo = _split_bf16(w_lr)
    w1 = jnp.concatenate([hi, lo, jnp.zeros((d, LANES - 4 * GLA_LOWRANK), BF16)], axis=1)
    bd = jnp.zeros((2 * GLA_LOWRANK, 2 * QK_W), F32)
    bd = bd.at[:GLA_LOWRANK, :QK_W].set(wa2[0]).at[GLA_LOWRANK:, QK_W:].set(wa2[1])
    w2 = jnp.concatenate([bd, bd, jnp.zeros((LANES - 4 * GLA_LOWRANK, 2 * QK_W), F32)], axis=0)
    w2_hi, w2_lo = _split_bf16(w2)
    return w1, jnp.concatenate([w2_hi, w2_hi, w2_lo], axis=0)


def _inproj_call(seq_in, modsel, g1, w_a, w_b, w_l, w2, ba, short_w, short_b):
    split = isinstance(seq_in, tuple)
    if split:
        ctx, x = seq_in
        b, d = x.shape[0], x.shape[2]
        s = ctx.shape[1] + x.shape[1]
    else:
        b, s, d = seq_in.shape
    nt = s // TT
    assert nt >= 2
    cur = lambda t: jnp.minimum(t, nt - 1)
    nxt = lambda t: jnp.minimum(t + 1, nt - 1)
    tile = lambda w: pl.BlockSpec((1, TT, w), lambda i, t: (i, cur(t), 0))
    bf = lambda w: jax.ShapeDtypeStruct((b, s, w), BF16)
    first_spec = pl.BlockSpec((1, TT, d), lambda i, t: (i, 0, 0))
    if split:
        x_specs = [first_spec, pl.BlockSpec((1, TT, d), lambda i, t: (i, nxt(t) - 1, 0))]
        x_args = [ctx, x]
    else:
        x_specs = [first_spec, pl.BlockSpec((1, TT, d), lambda i, t: (i, nxt(t), 0))]
        x_args = [seq_in, seq_in]
    return pl.pallas_call(
        functools.partial(_inproj_kernel, nt=nt),
        grid=(b, nt + 1),
        in_specs=x_specs + [
            pl.BlockSpec((1, 1, 6, d), lambda i, t: (i, 0, 0, 0)),
            pl.BlockSpec((1, 1, 6, d), lambda i, t: (i, 1, 0, 0)),
            _const_spec((1, d)),
            _const_spec(w_a.shape), _const_spec(w_b.shape), _const_spec(w_l.shape),
            _const_spec((3 * LANES, 2 * QK_W)), _const_spec((1, 2 * QK_W)),
            _const_spec((3, 3 * D_HY)), _const_spec((1, 3 * D_HY))],
        out_specs=[tile(1536), tile(1536), tile(3072), tile(2 * QK_W),
                   pl.BlockSpec((1, TT, D_HY), lambda i, t: (i, jnp.maximum(t - 1, 0), 0)),
                   pl.BlockSpec((1, D_HY // FFT_CB, TT // FFT_N2 * FFT_CB, FFT_N2),
                                lambda i, t: (i, 0, jnp.maximum(t - 2, 0), 0)),
                   pl.BlockSpec((1, D_HY, CTX_LEN), lambda i, t: (i, 0, 0))],
        out_shape=[bf(1536), bf(1536), bf(3072), jax.ShapeDtypeStruct((b, s, 2 * QK_W), F32),
                   bf(D_HY),
                   jax.ShapeDtypeStruct((b, D_HY // FFT_CB, (s - CTX_LEN) // FFT_N2 * FFT_CB, FFT_N2), F32),
                   jax.ShapeDtypeStruct((b, D_HY, CTX_LEN), F32)],
        scratch_shapes=[pltpu.VMEM((TT, d), BF16), pltpu.VMEM((TT, 3 * D_HY), F32),
                        pltpu.VMEM((8, 3 * D_HY), F32)],
        compiler_params=_params(2),
        name="in_proj",
    )(*x_args, modsel, modsel, g1, w_a, w_b, w_l, w2, ba, short_w, short_b)


@functools.lru_cache(maxsize=None)
def _scan_tables():
    i = np.arange(TT, dtype=np.float64)
    scale = DK ** -0.5
    diff = i[:, None] - i[None, :]
    same = (i[:, None] // GLA_CHUNK) == (i[None, :] // GLA_CHUNK)
    dmat = np.zeros((2, N_HEADS, TT, TT))
    erow = np.zeros((2, TT, V_W))
    kw = np.zeros((2, TT, QK_W))
    tri = np.zeros((2, TT, 2 * TT))
    for d in range(2):
        for h in range(N_HEADS):
            lg = RET_LOG_DECAY[d][h]
            if d == 0:
                dmat[d, h] = np.where(diff >= 0, np.exp(np.maximum(diff, 0) * lg), 0.0) * scale
                erow[d, :, h * DV:(h + 1) * DV] = np.exp((i + 1) * lg)[:, None]
                kw[d, :, h * DK:(h + 1) * DK] = (np.exp((TT - 1 - i) * lg) * scale)[:, None]
            else:
                dmat[d, h] = np.where(diff < 0, np.exp(np.maximum(-diff, 0) * lg), 0.0) * scale
                erow[d, :, h * DV:(h + 1) * DV] = np.exp((TT - i) * lg)[:, None]
                kw[d, :, h * DK:(h + 1) * DK] = (np.exp(i * lg) * scale)[:, None]
        block = same & ((diff >= 0) if d == 0 else (diff <= 0))
        tri[d] = np.concatenate([block, block], axis=1)
    f = lambda a: a.astype(np.float32)
    return f(dmat), f(erow), f(kw), f(tri)


def _scan_kernel(retf_ref, glaf_ref, lgf_ref, retb_ref, glab_ref, lgb_ref,
                 dmat_ref, erow_ref, kw_ref, tri_ref, of_ref, ob_ref, h_ref, hbd_ref, og_ref):
    @pl.when(pl.program_id(1) == 0)
    def _():
        h_ref[...] = jnp.zeros_like(h_ref)
        hbd_ref[...] = jnp.zeros_like(hbd_ref)

    c = GLA_CHUNK
    n_chunks = TT // c
    row = lax.broadcasted_iota(jnp.int32, (TT, TT), 0)
    col = lax.broadcasted_iota(jnp.int32, (TT, TT), 1)
    same_chunk = (row // c) == (col // c)
    heads = [(slice(hd * DK, (hd + 1) * DK), slice(hd * DV, (hd + 1) * DV)) for hd in range(N_HEADS)]
    def direction(d, ret_ref, gla_ref, lg_ref, o_ref):
        q = ret_ref[0, :, 0:QK_W]
        k = ret_ref[0, :, QK_W:2 * QK_W]
        v = ret_ref[0, :, 2 * QK_W:]
        o_inter = _dot(q, hbd_ref[2 * d]) * erow_ref[d]
        s_new = _dot_tn((k.astype(F32) * kw_ref[d]).astype(BF16), v)
        for hd, (ks, vs) in enumerate(heads):
            s = (_dot_nt(q[:, ks], k[:, ks]) * dmat_ref[d, hd]).astype(BF16)
            o_ref[0, :, vs] = (_dot(s, v[:, vs]) + o_inter[:, vs]).astype(BF16)
            h_new = math.exp(TT * RET_LOG_DECAY[d][hd]) * h_ref[d * 8 + hd] + s_new[ks, vs]
            h_ref[d * 8 + hd] = h_new
            hbd_ref[2 * d, ks, vs] = h_new.astype(BF16)
        yield

        hi, lo = _split_bf16(lg_ref[0])
        cum = _dot(tri_ref[d], jnp.concatenate([hi, lo], axis=0))
        edge = c - 1 if d == 0 else 0
        tot = jnp.concatenate([jnp.broadcast_to(cum[j * c + edge:j * c + edge + 1], (c, QK_W))
                               for j in range(n_chunks)], axis=0)
        a_tot = jnp.exp(tot)
        k_neg = gla_ref[0, :, QK_W:2 * QK_W].astype(F32) * jnp.exp(-cum)
        q_in = (gla_ref[0, :, 0:QK_W].astype(F32) * jnp.exp(cum) * DK ** -0.5).astype(BF16)
        k_out = (k_neg * a_tot).astype(BF16)
        k_neg = k_neg.astype(BF16)
        v = gla_ref[0, :, 2 * QK_W:]
        mask = jnp.logical_and(same_chunk, row >= col if d == 0 else row < col)
        yield
        for hd, (ks, vs) in enumerate(heads):
            s = jnp.where(mask, _dot_nt(q_in[:, ks], k_neg[:, ks]), 0.0).astype(BF16)
            og_ref[d, :, vs] = _dot(s, v[:, vs])
        for j in (range(n_chunks) if d == 0 else range(n_chunks - 1, -1, -1)):
            yield
            rows = slice(j * c, (j + 1) * c)
            inter = _dot(q_in[rows], hbd_ref[2 * d + 1])
            o_ref[0, rows, V_W:] = (og_ref[d, rows, :] + inter).astype(BF16)
            s_j = _dot_tn(k_out[rows], v[rows])
            a_col = jnp.broadcast_to(a_tot[j * c:j * c + 1], (DV, QK_W)).T
            for hd, (ks, vs) in enumerate(heads):
                h_new = a_col[ks] * h_ref[d * 8 + 4 + hd] + s_j[ks, vs]
                h_ref[d * 8 + 4 + hd] = h_new
                hbd_ref[2 * d + 1, ks, vs] = h_new.astype(BF16)

    pending = [direction(0, retf_ref, glaf_ref, lgf_ref, of_ref), direction(1, retb_ref, glab_ref, lgb_ref, ob_ref)]
    while pending:
        pending = [g for g in pending if next(g, True) is None]


def _scan_call(ret, gla, lga):
    b, s, _ = ret.shape
    nt = s // TT
    dmat, erow, kw, tri = (jnp.asarray(z) for z in _scan_tables())
    tri = tri.astype(BF16)
    fwd = lambda i, t: (i, t, 0)
    bwd = lambda i, t: (i, jnp.where(t == 0, 0, nt - t), 0)
    bwd_g = lambda i, t: (i, jnp.where(t == 0, 0, nt - t), 1)
    qkv = lambda f: pl.BlockSpec((1, TT, 2 * QK_W + V_W), f)
    return pl.pallas_call(
        _scan_kernel,
        grid=(b, nt),
        in_specs=[qkv(fwd), qkv(fwd), pl.BlockSpec((1, TT, QK_W), fwd),
                  qkv(bwd), qkv(bwd), pl.BlockSpec((1, TT, QK_W), bwd_g),
                  _const_spec(dmat.shape), _const_spec(erow.shape), _const_spec(kw.shape),
                  _const_spec(tri.shape)],
        out_specs=[pl.BlockSpec((1, TT, 2 * V_W), fwd), pl.BlockSpec((1, TT, 2 * V_W), bwd)],
        out_shape=[jax.ShapeDtypeStruct((b, s, 2 * V_W), BF16)] * 2,
        scratch_shapes=[pltpu.VMEM((16, DK, DV), F32), pltpu.VMEM((4, QK_W, V_W), BF16),
                        pltpu.VMEM((2, TT, V_W), F32)],
        compiler_params=_params(2),
        name="bidir_scan",
    )(ret, gla, lga, ret, gla, lga, dmat, erow, kw, tri)


def _hyena_gates(t, nt, cur_ref, prev_ref, next_row, w_ref, b_ref, x0_ref):
    has_prev = t >= 2
    has_next = jnp.logical_and(t >= 1, t <= nt - 2)
    ridx = lax.broadcasted_iota(jnp.int32, (TT, LANES), 0)
    u = []
    for part in range(3):
        cols = []
        for j in range(D_HY // LANES):
            cs = slice(part * D_HY + j * LANES, part * D_HY + (j + 1) * LANES)
            p = cur_ref[:, cs]
            pv = jnp.where(has_prev, prev_ref[7:8, cs], 0.0)
            nx = jnp.where(has_next, next_row[:, cs], 0.0)
            up = jnp.where(ridx == 0, pv, pltpu.roll(p, 1, 0))
            dn = jnp.where(ridx == TT - 1, nx, pltpu.roll(p, TT - 1, 0))
            cols.append(b_ref[:, cs] + up * w_ref[0:1, cs] + p * w_ref[1:2, cs] + dn * w_ref[2:3, cs])
        u.append(cols)
    x0_ref[0] = jnp.concatenate(u[0], axis=1).astype(BF16)
    z = jnp.concatenate([a * c for a, c in zip(u[1], u[2])], axis=1)
    return z.T


def _store_z(t, zt, zl_ref, zc_ref):
    @pl.when(t == 0)
    def _():
        zc_ref[0] = zt

    @pl.when(t > 0)
    def _():
        for r in range(TT // FFT_N2):
            zl_ref[0, :, r * FFT_CB:(r + 1) * FFT_CB, :] = _to_conv_rows(zt[:, r * FFT_N2:(r + 1) * FFT_N2])


@functools.lru_cache(maxsize=None)
def _filter_positions(length):
    n = np.arange(2 * length)
    m = np.where(n < length, n, 2 * length - n).astype(np.float64)
    m = np.where(n == length, 0.0, m)
    t = m / (length - 1)
    bands = np.linspace(1e-4, HY_BANDS - 1.0, HY_BANDS)
    ang = (2.0 * math.pi / length) * m[None, :] * bands[:, None]
    z = np.zeros((HY_EMB_PAD, 2 * length))
    z[0] = t
    z[1:1 + HY_BANDS] = np.cos(ang)
    z[1 + HY_BANDS:HY_EMB] = -np.sin(ang)
    return z.astype(np.float32)


def _filter_kernel(z_ref, w1_ref, b1_ref, w2_ref, b2_ref, w3_ref, fr_ref, dl_ref, k_ref, *, length, pb, split):
    z = z_ref[...]
    fr = fr_ref[...]
    hdn = jnp.sin(fr * (_dot(w1_ref[...], z, HIGHEST) + b1_ref[...]))
    for i in range(HY_INNER):
        hdn = jnp.sin(fr * (_dot(w2_ref[i], hdn, HIGHEST) + b2_ref[i]))
    h = _dot(w3_ref[0], hdn, HIGHEST)
    window = jnp.exp(-dl_ref[...] * z[0:1]) + HY_SHIFT
    pos = pl.program_id(0) * pb + lax.broadcasted_iota(jnp.int32, (1, pb), 1)
    k = jnp.where(pos == length, 0.0, h * window)
    if split:
        for r in range(pb // FFT_N2):
            k_ref[:, r * FFT_CB:(r + 1) * FFT_CB, :] = _to_conv_rows(k[:, r * FFT_N2:(r + 1) * FFT_N2])
    else:
        k_ref[...] = k


def _filter_call(length, w1, b1, w2, b2, w3, freq, split):
    pb = min(2048, length)
    if split:
        out_spec = pl.BlockSpec((D_HY // FFT_CB, pb // FFT_N2 * FFT_CB, FFT_N2), lambda j: (0, j, 0))
        out_shape = jax.ShapeDtypeStruct((D_HY // FFT_CB, 2 * length // FFT_N2 * FFT_CB, FFT_N2), F32)
    else:
        out_spec = pl.BlockSpec((D_HY, pb), lambda j: (0, j))
        out_shape = jax.ShapeDtypeStruct((D_HY, 2 * length), F32)
    zf = jnp.asarray(_filter_positions(length))
    w1t = jnp.pad(w1.T, ((0, 0), (0, HY_EMB_PAD - HY_EMB)))
    w2t = jnp.swapaxes(w2, 1, 2)
    w3t = w3.T.reshape(2, D_HY, HY_FILTER_WIDTH)
    col = lambda a: a.reshape(a.shape + (1,))
    deltas = np.abs(np.linspace(HY_MIN_DECAY, HY_MAX_DECAY, D_HY)).astype(np.float32)
    fw = HY_FILTER_WIDTH
    return pl.pallas_call(
        functools.partial(_filter_kernel, length=length, pb=pb, split=split),
        grid=(2 * length // pb,),
        in_specs=[pl.BlockSpec((HY_EMB_PAD, pb), lambda j: (0, j)),
                  _const_spec((fw, HY_EMB_PAD)), _const_spec((fw, 1)),
                  _const_spec((HY_INNER, fw, fw)), _const_spec((HY_INNER, fw, 1)),
                  pl.BlockSpec((1, D_HY, fw), lambda j: (jnp.where(j >= length // pb, 1, 0), 0, 0)),
                  _const_spec((fw, 1)), _const_spec((D_HY, 1))],
        out_specs=out_spec,
        out_shape=out_shape,
        compiler_params=_params(1),
        name="hyena_filter",
    )(zf, w1t, col(b1), w2t, col(b2), w3t, col(freq), jnp.asarray(deltas).reshape(D_HY, 1))


def _real_block(w):
    return np.block([[w.real, w.imag], [-w.imag, w.real]])


@functools.lru_cache(maxsize=None)
def _fft_tables(length):
    n = 2 * length
    n2 = FFT_N2
    n1 = n // n2
    f1 = np.arange(n1)
    wa = np.exp(-2j * np.pi * np.outer(f1, np.arange(n1)) / n1)
    stage_a = np.concatenate([wa.real, wa.imag], axis=0)
    tw = np.exp(-2j * np.pi * np.outer(f1, np.arange(n2)) / n)
    wb = np.exp(-2j * np.pi * np.outer(np.arange(n2), np.arange(n2)) / n2)
    wh = wa[:, :n1 // 2]
    a_pair = np.block([[wh.real, -wh.imag], [wh.imag, wh.real]])
    wi = np.conj(wa[:n1 // 2]) / n
    a_inv = np.block([[wi.real, -wi.imag], [wi.imag, wi.real]])
    f = lambda a: a.astype(np.float32)
    return dict(a_pair=f(a_pair), a_k=f(stage_a), tw_re=f(tw.real), tw_im=f(tw.imag),
                b_fwd=f(_real_block(wb)), b_inv=f(_real_block(np.conj(wb))), a_inv=f(a_inv))


def _fft_conv_kernel(bias_ref, z_ref, k_ref, ap_ref, ak_ref, twr_ref, twi_ref, bf_ref, bi_ref, ai_ref,
                     y_ref, sz_ref, sk_ref, sv_ref):
    n1, n2 = twr_ref.shape
    h, cb = n1 // 2, FFT_CB
    z0, z1 = (z_ref.at[b, 0] for b in range(2))
    y0, y1 = (y_ref.at[b, 0] for b in range(2))
    kk = k_ref.at[0]
    twr = twr_ref[...]
    twi = twi_ref[...]

    def twiddled(a):
        ar, ai = a[:n1], a[n1:]
        return jnp.concatenate([ar * twr - ai * twi, ar * twi + ai * twr], axis=1).astype(BF16)

    def stage_a(p):
        pair = (2 * p, 2 * p + 1)
        zc = jnp.concatenate([jnp.concatenate([z0[pl.ds(c, h, stride=cb), :], z1[pl.ds(c, h, stride=cb), :]], axis=0)
                              for c in pair], axis=1)
        kc = jnp.concatenate([kk[pl.ds(c, n1, stride=cb), :] for c in pair], axis=1)
        az = _dot(ap_ref[...], zc.astype(BF16))
        ak = _dot(ak_ref[...], kc.astype(BF16))
        for i, c in enumerate(pair):
            rows = slice(c * n1, (c + 1) * n1)
            sz_ref[rows, :] = twiddled(az[:, i * n2:(i + 1) * n2])
            sk_ref[rows, :] = twiddled(ak[:, i * n2:(i + 1) * n2])

    gb = 4

    def stage_b(g):
        rows = slice(g * gb * n1, (g + 1) * gb * n1)
        xs = _dot(sz_ref[rows, :], bf_ref[...])
        ks = _dot(sk_ref[rows, :], bf_ref[...])
        xr, xi, kr, ki = xs[:, :n2], xs[:, n2:], ks[:, :n2], ks[:, n2:]
        ys = jnp.concatenate([xr * kr - xi * ki, xr * ki + xi * kr], axis=1).astype(BF16)
        u = _dot(ys, bi_ref[...])
        for i in range(gb):
            ur, ui = u[i * n1:(i + 1) * n1, :n2], u[i * n1:(i + 1) * n1, n2:]
            lanes = slice((i % 2) * n2, (i % 2 + 1) * n2)
            sv_ref[g * (gb // 2) + i // 2, :n1, lanes] = (ur * twr + ui * twi).astype(BF16)
            sv_ref[g * (gb // 2) + i // 2, n1:, lanes] = (ui * twr - ur * twi).astype(BF16)

    def stage_c(p):
        y = _dot(ai_ref[...], sv_ref[p])
        for i in range(2):
            c = 2 * p + i
            bias = bias_ref[pl.program_id(0) * cb + c]
            sel = pl.ds(c, h, stride=cb)
            y0[sel, :] = y[:h, i * n2:(i + 1) * n2] + z0[sel, :] * bias
            y1[sel, :] = y[h:, i * n2:(i + 1) * n2] + z1[sel, :] * bias

    n_groups = cb // gb
    for step in range(n_groups + 2):
        if step < n_groups:
            for p in range(step * gb // 2, (step + 1) * gb // 2):
                stage_a(p)
        if 0 <= step - 1 < n_groups:
            stage_b(step - 1)
        if 0 <= step - 2 < n_groups:
            for p in range((step - 2) * gb // 2, (step - 1) * gb // 2):
                stage_c(p)


def _to_conv_rows(a):
    return a.reshape(a.shape[0] // FFT_CB, FFT_CB, a.shape[1])


def _fft_conv_call(zq, kq, bias):
    cb = FFT_CB
    b, nblk, rows, n2 = zq.shape
    assert b == 2 and n2 == FFT_N2
    h, ch = rows // cb, nblk * cb
    n1 = 2 * h
    tb = _fft_tables(h * n2)
    bf = lambda name: jnp.asarray(tb[name]).astype(BF16)
    consts = [bf("a_pair"), bf("a_k"), jnp.asarray(tb["tw_re"]), jnp.asarray(tb["tw_im"]),
              bf("b_fwd"), bf("b_inv"), bf("a_inv")]
    return pl.pallas_call(
        _fft_conv_kernel,
        grid=(nblk,),
        in_specs=[pl.BlockSpec(memory_space=pltpu.SMEM),
                  pl.BlockSpec((b, 1, h * cb, n2), lambda j: (0, j, 0, 0)),
                  pl.BlockSpec((1, n1 * cb, n2), lambda j: (j, 0, 0))]
                 + [_const_spec(a.shape) for a in consts],
        out_specs=pl.BlockSpec((b, 1, h * cb, n2), lambda j: (0, j, 0, 0)),
        out_shape=jax.ShapeDtypeStruct(zq.shape, F32),
        scratch_shapes=[pltpu.VMEM((cb * n1, 2 * n2), BF16), pltpu.VMEM((cb * n1, 2 * n2), BF16),
                        pltpu.VMEM((cb // 2, 2 * n1, 2 * n2), BF16)],
        compiler_params=_params(1),
        name="hyena_long_conv",
    )(bias, zq, kq, *consts)


@functools.lru_cache(maxsize=None)
def _dft_tables(length):
    n = 2 * length
    w = np.exp(-2j * np.pi * np.outer(np.arange(n), np.arange(n)) / n)
    fwd = np.concatenate([w.real, w.imag], axis=1)
    inv = np.concatenate([w.real[:, :length], w.imag[:, :length]], axis=0) / n
    return fwd[:length].astype(np.float32), fwd.astype(np.float32), inv.astype(np.float32)


def _dft_conv_kernel(z_ref, k_ref, bias_ref, fz_ref, fk_ref, fi_ref, y_ref):
    n = k_ref.shape[1]
    ks = _dot(k_ref[...], fk_ref[...], HIGHEST)
    kr, ki = ks[:, :n], ks[:, n:]
    for b in range(z_ref.shape[0]):
        z = z_ref[b]
        xs = _dot(z, fz_ref[...], HIGHEST)
        xr, xi = xs[:, :n], xs[:, n:]
        ys = jnp.concatenate([xr * kr - xi * ki, xr * ki + xi * kr], axis=1)
        y_ref[b] = _dot(ys, fi_ref[...], HIGHEST) + z * bias_ref[...]


def _dft_conv_call(zt, kt, bias):
    b, ch, length = zt.shape
    fz, fk, fi = (jnp.asarray(a) for a in _dft_tables(length))
    return pl.pallas_call(
        _dft_conv_kernel,
        out_shape=jax.ShapeDtypeStruct((b, ch, length), F32),
        compiler_params=pltpu.CompilerParams(vmem_limit_bytes=VMEM_LIMIT),
        name="hyena_ctx_conv",
    )(zt, kt, bias.reshape(ch, 1), fz, fk, fi)


def _head_norm(o):
    mu = jnp.mean(o, axis=-1, keepdims=True)
    var = jnp.mean(jnp.square(o - mu), axis=-1, keepdims=True)
    return (o - mu) * lax.rsqrt(var + HEAD_NORM_EPS)


def _merge_mix(of_ref, ob_ref, rg_ref, gr_ref, x0_ref, yl_ref, yc_ref, mg_ref, wb_ref, wo_ref, is_ctx):
    o = of_ref[0].astype(F32) + ob_ref[0].astype(F32)
    mixed = None
    for m, gate_ref in enumerate((rg_ref, gr_ref)):
        heads = [_head_norm(o[:, m * V_W + hd * DV:m * V_W + (hd + 1) * DV]) for hd in range(N_HEADS)]
        gh = gate_ref[0].astype(F32)
        br = jnp.concatenate(heads, axis=1) * (gh * (1.0 + jnp.tanh(gh)))
        g2 = 1.0 + jnp.tanh(mg_ref[0, :, m * D_MODEL:(m + 1) * D_MODEL].astype(F32))
        term = g2 * _dot(br.astype(BF16), wb_ref[m])
        mixed = term if mixed is None else mixed + term
    yt = jnp.concatenate([yl_ref[0, :, r * FFT_CB:(r + 1) * FFT_CB, :].reshape(D_HY, FFT_N2)
                          for r in range(TT // FFT_N2)], axis=1)
    if yc_ref is not None:
        yt = jnp.where(is_ctx, yc_ref[0], yt)
    hy = x0_ref[0].astype(F32) * yt.T
    g2 = 1.0 + jnp.tanh(mg_ref[0, :, 2 * D_MODEL:3 * D_MODEL].astype(F32))
    mixed = mixed + g2 * _dot(hy.astype(BF16), wb_ref[2])
    return _dot(mixed.astype(BF16), wo_ref[...])


def _gelu_x2(x):
    return x * (1.0 + lax.erf(x * (2.0 ** -0.5)))


def _mix_ffn_kernel(*refs, has_ctx, split, nt, final):
    refs = list(refs)
    mixers = refs[:6]
    del refs[:6]
    yc_ref = refs.pop(0) if has_ctx else None
    mg_ref = refs.pop(0)
    ctx_ref = refs.pop(0) if split else None
    (x_ref, modu_ref, modr_ref, wb_ref, wo_ref, g_ref, wu_ref, cw_ref, cb_ref, wd_ref, fg_ref,
     out_ref, a_ref, v_ref, tail_ref, acc_ref, xk_ref) = refs
    s = pl.program_id(1)
    p = lax.rem(s, 2)
    gw = GRID_W

    @pl.when(s == 0)
    def _():
        a_ref[...] = jnp.zeros_like(a_ref)
        v_ref[...] = jnp.zeros_like(v_ref)
        xk_ref[...] = jnp.zeros_like(xk_ref)

    tail_ref[...] = a_ref[p, TT - gw:, :]
    mu = modu_ref[0, 0]
    x = x_ref[0]
    if split:
        x = jnp.where(s == 0, ctx_ref[0], x)
    x1 = x + mu[2:3] * _merge_mix(*mixers, yc_ref, mg_ref, wb_ref, wo_ref, s == 0)
    xk_ref[p] = x1
    hb = _modulate(x1, g_ref[...], mu[3:4], mu[4:5]).astype(BF16)
    a_ref[p] = _dot(hb, wu_ref[:, :D_FF])
    v_ref[p] = _dot(hb, wu_ref[:, D_FF:]).astype(BF16)

    t = s - 1
    first_lat = 1 if has_ctx else 0
    is_ctx = t < first_lat
    up_ok = t >= first_lat + 1
    dn_ok = jnp.logical_and(t >= first_lat, t <= nt - 2)
    row_w = jnp.where(is_ctx, 0.0, 1.0)
    sub = lax.broadcasted_iota(jnp.int32, (8, LANES), 0)
    n_rows = TT // gw
    first_col = [jnp.logical_and(sub == 0, jnp.logical_or(k == 0, jnp.logical_not(is_ctx))) for k in range(n_rows)]
    last_col = [jnp.logical_and(sub == 7, jnp.logical_or(k == n_rows - 1, jnp.logical_not(is_ctx)))
                for k in range(n_rows)]

    def from_left(x):
        r = pltpu.roll(x, 1, 0)
        parts = []
        for k in range(n_rows):
            parts += [jnp.where(first_col[k], 0.0, r[k * gw:k * gw + 8]), r[k * gw + 8:(k + 1) * gw]]
        return jnp.concatenate(parts, axis=0)

    def from_right(x):
        r = pltpu.roll(x, TT - 1, 0)
        parts = []
        for k in range(n_rows):
            parts += [r[k * gw:(k + 1) * gw - 8], jnp.where(last_col[k], 0.0, r[(k + 1) * gw - 8:(k + 1) * gw])]
        return jnp.concatenate(parts, axis=0)

    def conv_chunk(cs):
        above = jnp.where(up_ok, tail_ref[:, cs], 0.0)
        below = jnp.where(dn_ok, a_ref[p, :gw, cs], 0.0)
        mid = a_ref[1 - p, :, cs]
        rows = (jnp.concatenate([above, mid[:TT - gw]], axis=0), mid, jnp.concatenate([mid[gw:], below], axis=0))
        w = [cw_ref[k:k + 1, cs] * (1.0 if k // 3 == 1 else row_w) for k in range(9)]
        taps = [rows[0] * w[dj] + rows[1] * w[3 + dj] + rows[2] * w[6 + dj] for dj in range(3)]
        acc = cb_ref[:, cs] + taps[1] + from_left(taps[0]) + from_right(taps[2])
        return _gelu_x2(acc).astype(BF16) * v_ref[1 - p, :, cs]

    kb = 2 * LANES
    for j in range(D_FF // kb):
        act = jnp.concatenate([conv_chunk(slice(j * kb + i * LANES, j * kb + (i + 1) * LANES)) for i in range(2)],
                              axis=1)
        part = _dot(act, wd_ref[j * kb:(j + 1) * kb, :])
        if j == 0:
            acc_ref[...] = part
        else:
            acc_ref[...] += part
    y = xk_ref[1 - p] + modr_ref[0, 0, 5:6] * acc_ref[...]
    if final:
        y = y * lax.rsqrt(jnp.mean(y * y, axis=-1, keepdims=True) + NORM_EPS) * fg_ref[...]
    out_ref[0] = y


def _mix_ffn_call(o_f, o_b, ret, gla, x0, y_lat, y_ctx, mg, seq_in, modsel, w_branch, w_out,
                  g2, w_up, conv_w, conv_b, w_down, final_g, final):
    split = isinstance(seq_in, tuple)
    has_ctx = y_ctx is not None
    assert has_ctx or not split
    b, s, _ = o_f.shape
    d = D_MODEL
    off = 0 if has_ctx else 1
    n = s // TT - off
    cur = lambda t: jnp.minimum(t, n - 1) + off
    tile = lambda w, j=0: pl.BlockSpec((1, TT, w), lambda i, t: (i, cur(t), j))
    mod_spec = lambda f: pl.BlockSpec((1, 1, 6, d), lambda i, t: (i, jnp.minimum(f(t), 1), 0, 0))
    in_specs = [tile(2 * V_W), tile(2 * V_W), tile(V_W, 2), tile(V_W, 2), tile(D_HY),
                pl.BlockSpec((1, D_HY // FFT_CB, TT // FFT_N2 * FFT_CB, FFT_N2),
                             lambda i, t: (i, 0, jnp.maximum(cur(t) - 1, 0), 0))]
    args = [o_f, o_b, ret, gla, x0, y_lat]
    if has_ctx:
        in_specs.append(pl.BlockSpec((1, D_HY, CTX_LEN), lambda i, t: (i, 0, 0)))
        args.append(y_ctx)
    in_specs.append(tile(3 * D_MODEL))
    args.append(mg)
    if split:
        in_specs += [pl.BlockSpec((1, TT, d), lambda i, t: (i, 0, 0)),
                     pl.BlockSpec((1, TT, d), lambda i, t: (i, jnp.maximum(cur(t) - 1, 0), 0))]
        args += list(seq_in)
    else:
        in_specs.append(tile(d))
        args.append(seq_in)
    in_specs += [mod_spec(cur), mod_spec(lambda t: jnp.maximum(t - 1, 0) + off),
                 _const_spec((3, V_W, d)), _const_spec((d, d)),
                 _const_spec((1, d)), _const_spec((d, 2 * D_FF)),
                 _const_spec((9, D_FF)), _const_spec((1, D_FF)), _const_spec((D_FF, d)), _const_spec((1, d))]
    args += [modsel, modsel, w_branch, w_out, g2, w_up, conv_w, conv_b, w_down, final_g]
    return pl.pallas_call(
        functools.partial(_mix_ffn_kernel, has_ctx=has_ctx, split=split, nt=n, final=final),
        grid=(b, n + 1),
        in_specs=in_specs,
        out_specs=pl.BlockSpec((1, TT, d), lambda i, t: (i, jnp.maximum(t - 1, 0), 0)),
        out_shape=jax.ShapeDtypeStruct((b, n * TT, d), F32),
        scratch_shapes=[pltpu.VMEM((2, TT, D_FF), F32), pltpu.VMEM((2, TT, D_FF), BF16),
                        pltpu.VMEM((GRID_W, D_FF), F32), pltpu.VMEM((TT, d), F32),
                        pltpu.VMEM((2, TT, d), F32)],
        compiler_params=_params(2),
        name="mix_ffn",
    )(*args)


def kernel(x, c, ctx, c_ctx, ada_w, ada_b, norm1_g, w_in, gla_wa2, gla_ba, hy_short_w, hy_short_b,
           hy_w1, hy_b1, hy_w2, hy_b2, hy_w3, hy_freq, hy_bias, w_branch, w_out, norm2_g, w_up,
           ffn_conv_w, ffn_conv_b, w_down, final_g):
    bsz, seq, d = x.shape
    depth = ada_w.shape[0]
    assert d == D_MODEL and ctx.shape[1] == CTX_LEN == TT and seq % TT == 0
    assert (2 * seq) % (2 * FFT_N2) == 0 and bsz == 2

    cvec = jnp.zeros((8, d), F32).at[:bsz].set(c).at[bsz].set(c_ctx)
    mod_all = _ada_call(cvec, ada_w, ada_b).reshape(depth, 8, 6, d)

    stream = (ctx, x)
    for l in range(depth):
        last = l == depth - 1
        m = mod_all[l]
        modsel = jnp.stack([jnp.broadcast_to(m[bsz], (bsz, 6, d)), m[:bsz]], axis=1)
        w = w_in[l]
        half = jnp.asarray(_gate_halving())
        lo, hi = LR_OFF, LR_OFF + 2 * GLA_LOWRANK
        w_l, w2 = _gate_weights(w[:, lo:hi], gla_wa2[l])
        ret, gla, mg, lga, x0, z_lat, z_ctx = _inproj_call(
            stream, modsel, norm1_g[l].reshape(1, d), (w[:, :lo] * half[:lo]).astype(BF16),
            (w[:, hi:] * half[hi:]).astype(BF16), w_l, w2, gla_ba[l].reshape(1, 2 * QK_W),
            hy_short_w[l], hy_short_b[l].reshape(1, 3 * D_HY))
        o_f, o_b = _scan_call(ret, gla, lga)

        filt = (hy_w1[l], hy_b1[l], hy_w2[l], hy_b2[l], hy_w3[l], hy_freq[l])
        y_lat = _fft_conv_call(z_lat, _filter_call(seq, *filt, split=True), hy_bias[l])
        y_ctx = None if last else _dft_conv_call(z_ctx, _filter_call(CTX_LEN, *filt, split=False), hy_bias[l])

        stream = _mix_ffn_call(o_f, o_b, ret, gla, x0, y_lat, y_ctx, mg, stream, modsel,
                               _to_bf16(w_branch, l, 0.5), _to_bf16(w_out, l),
                               norm2_g[l].reshape(1, d), _to_bf16(w_up, l),
                               ffn_conv_w[l].reshape(9, D_FF), ffn_conv_b[l].reshape(1, D_FF),
                               _to_bf16(w_down, l, 0.5), final_g.reshape(1, d), final=last)
    return stream
```

```python
import functools
import math

import numpy as np
import jax
import jax.numpy as jnp
from jax import lax
from jax.experimental import pallas as pl
from jax.experimental.pallas import tpu as pltpu

F32 = jnp.float32
BF16 = jnp.bfloat16
HIGHEST = lax.Precision.HIGHEST

D_MODEL = 1024
CTX_LEN = 256
GRID_W = 64
NORM_EPS = 1e-6
HEAD_NORM_EPS = 1e-5

N_HEADS = 4
DK = 64
DV = 128
QK_W = N_HEADS * DK
V_W = N_HEADS * DV
RET_LOG_DECAY = (tuple(math.log1p(-2.0 ** (-5.0 - h)) for h in range(N_HEADS)),
                 tuple(math.log1p(-2.0 ** (-5.5 - h)) for h in range(N_HEADS)))
GLA_LOWRANK = 16
GLA_GATE_NORM = 16.0
GLA_CHUNK = 64

D_HY = 512
HY_BANDS = 16
HY_EMB = 1 + 2 * HY_BANDS
HY_EMB_PAD = 40
HY_FILTER_WIDTH = 64
HY_INNER = 2
HY_MIN_DECAY = math.log(1e-2) / 1.5
HY_MAX_DECAY = math.log(1e-2) / 0.3
HY_SHIFT = 0.05

D_FF = 2816
D_IN_MAIN = 7680
LR_OFF = 3072

TT = 256
LANES = 128
FFT_N2 = 128
FFT_CB = 16
VMEM_LIMIT = 56 * 1024 * 1024


def _dot(a, b, precision=None):
    return jnp.dot(a, b, preferred_element_type=F32, precision=precision)


def _dot_nt(a, b):
    return lax.dot_general(a, b, (((1,), (1,)), ((), ())), preferred_element_type=F32)


def _dot_tn(a, b):
    return lax.dot_general(a, b, (((0,), (0,)), ((), ())), preferred_element_type=F32)


def _params(n_axes):
    return pltpu.CompilerParams(dimension_semantics=("arbitrary",) * n_axes,
                                vmem_limit_bytes=VMEM_LIMIT)


def _const_spec(shape):
    n = len(shape)
    return pl.BlockSpec(shape, lambda *_: (0,) * n, pipeline_mode=pl.Buffered(1))


def _modulate(x, g, shift, scale):
    y = x * lax.rsqrt(jnp.mean(x * x, axis=-1, keepdims=True) + NORM_EPS) * g
    return y * (1.0 + scale) + shift


def _sigmoid(x):
    return 0.5 * jnp.tanh(0.5 * x) + 0.5


def _silu(x):
    return x * _sigmoid(x)


def _cast_kernel(w_ref, o_ref, *, scale):
    o_ref[...] = (w_ref[...] * scale).astype(BF16) if scale != 1.0 else w_ref[...].astype(BF16)


def _to_bf16(w_stack, l, scale=1.0):
    shape = w_stack.shape[1:]
    cols = shape[-1]
    rows = math.prod(shape[:-1])
    rb = 256
    assert rows % rb == 0 and cols % LANES == 0
    nb = rows // rb
    out = pl.pallas_call(
        functools.partial(_cast_kernel, scale=scale),
        grid=(nb,),
        in_specs=[pl.BlockSpec((rb, cols), lambda i: (l * nb + i, 0))],
        out_specs=pl.BlockSpec((rb, cols), lambda i: (i, 0)),
        out_shape=jax.ShapeDtypeStruct((rows, cols), BF16),
        compiler_params=_params(1),
        name="weight_cast",
    )(w_stack.reshape(-1, cols))
    return out.reshape(shape)


def _ada_kernel(c_ref, w_ref, b_ref, o_ref):
    o_ref[0] = _dot(_silu(c_ref[...]), w_ref[0], HIGHEST) + b_ref[0]


def _ada_call(cvec, ada_w, ada_b):
    depth, d, n = ada_w.shape
    nb = 1536
    return pl.pallas_call(
        _ada_kernel,
        grid=(depth, n // nb),
        in_specs=[pl.BlockSpec((8, d), lambda l, j: (0, 0)),
                  pl.BlockSpec((1, d, nb), lambda l, j: (l, 0, j)),
                  pl.BlockSpec((1, 1, nb), lambda l, j: (l, 0, j))],
        out_specs=pl.BlockSpec((1, 8, nb), lambda l, j: (l, 0, j)),
        out_shape=jax.ShapeDtypeStruct((depth, 8, n), F32),
        compiler_params=_params(2),
        name="ada_ln",
    )(cvec, ada_w, ada_b.reshape(depth, 1, n))


def _log_sigmoid(x):
    return jnp.minimum(x, 0.0) - jnp.log1p(jnp.exp(-jnp.abs(x)))


def _split_bf16(a):
    hi = a.astype(BF16)
    return hi, (a - hi.astype(F32)).astype(BF16)


def _inproj_kernel(first_ref, next_ref, modf_ref, modn_ref, g_ref, wa_ref, wb_ref, wl_ref, w2_ref, ba_ref,
                   sw_ref, sb_ref, ret_ref, gla_ref, mg_ref, lga_ref, x0_ref, zl_ref, zc_ref,
                   hb_ref, cur_ref, prev_ref, *, nt):
    s = pl.program_id(1)
    g = g_ref[...]

    @pl.when(s == 0)
    def _():
        cur_ref[...] = jnp.zeros_like(cur_ref)
        prev_ref[...] = jnp.zeros_like(prev_ref)
        mf = modf_ref[0, 0]
        hb_ref[...] = _modulate(first_ref[0], g, mf[0:1], mf[1:2]).astype(BF16)

    hb = hb_ref[...]
    lr_hi, lr_lo = _split_bf16(_dot(hb, wl_ref[...]))
    gate = _dot(jnp.concatenate([lr_hi, lr_lo, lr_hi], axis=1), w2_ref[...]) + ba_ref[...]
    lga_ref[0] = _log_sigmoid(gate) * (1.0 / GLA_GATE_NORM)
    hy_new = _dot(hb, wb_ref[:, :3 * D_HY])
    zt = _hyena_gates(s - 1, nt, cur_ref, prev_ref, hy_new[0:1], sw_ref, sb_ref, x0_ref)
    prev_ref[...] = cur_ref[TT - 8:, :]
    cur_ref[...] = hy_new
    mg_ref[0] = _dot(hb, wb_ref[:, 3 * D_HY:]).astype(BF16)
    ret_ref[0] = _dot(hb, wa_ref[:, :1536]).astype(BF16)
    gla_ref[0] = _dot(hb, wa_ref[:, 1536:]).astype(BF16)
    mn = modn_ref[0, 0]
    hb_ref[...] = _modulate(next_ref[0], g, mn[0:1], mn[1:2]).astype(BF16)
    _store_z(s - 1, zt, zl_ref, zc_ref)


@functools.lru_cache(maxsize=None)
def _gate_halving():
    scale = np.ones((LR_OFF + 2 * GLA_LOWRANK + 3 * D_HY + 3 * D_MODEL,), np.float32)
    scale[2 * QK_W + V_W:2 * QK_W + 2 * V_W] = 0.5
    scale[4 * QK_W + 3 * V_W:LR_OFF] = 0.5
    scale[LR_OFF + 2 * GLA_LOWRANK + 3 * D_HY:] = 0.5
    return scale


def _gate_weights(w_lr, wa2):
    d = w_lr.shape[0]
    hi, lo = _split_bf16(w_lr)
    w1 = jnp.concatenate([hi, lo, jnp.zeros((d, LANES - 4 * GLA_LOWRANK), BF16)], axis=1)
    bd = jnp.zeros((2 * GLA_LOWRANK, 2 * QK_W), F32)
    bd = bd.at[:GLA_LOWRANK, :QK_W].set(wa2[0]).at[GLA_LOWRANK:, QK_W:].set(wa2[1])
    w2 = jnp.concatenate([bd, bd, jnp.zeros((LANES - 4 * GLA_LOWRANK, 2 * QK_W), F32)], axis=0)
    w2_hi, w2_lo = _split_bf16(w2)
    return w1, jnp.concatenate([w2_hi, w2_hi, w2_lo], axis=0)


def _inproj_call(seq_in, modsel, g1, w_a, w_b, w_l, w2, ba, short_w, short_b):
    split = isinstance(seq_in, tuple)
    if split:
        ctx, x = seq_in
        b, d = x.shape[0], x.shape[2]
        s = ctx.shape[1] + x.shape[1]
    else:
        b, s, d = seq_in.shape
    nt = s // TT
    assert nt >= 2
    cur = lambda t: jnp.minimum(t, nt - 1)
    nxt = lambda t: jnp.minimum(t + 1, nt - 1)
    tile = lambda w: pl.BlockSpec((1, TT, w), lambda i, t: (i, cur(t), 0))
    bf = lambda w: jax.ShapeDtypeStruct((b, s, w), BF16)
    first_spec = pl.BlockSpec((1, TT, d), lambda i, t: (i, 0, 0))
    if split:
        x_specs = [first_spec, pl.BlockSpec((1, TT, d), lambda i, t: (i, nxt(t) - 1, 0))]
        x_args = [ctx, x]
    else:
        x_specs = [first_spec, pl.BlockSpec((1, TT, d), lambda i, t: (i, nxt(t), 0))]
        x_args = [seq_in, seq_in]
    return pl.pallas_call(
        functools.partial(_inproj_kernel, nt=nt),
        grid=(b, nt + 1),
        in_specs=x_specs + [
            pl.BlockSpec((1, 1, 6, d), lambda i, t: (i, 0, 0, 0)),
            pl.BlockSpec((1, 1, 6, d), lambda i, t: (i, 1, 0, 0)),
            _const_spec((1, d)),
            _const_spec(w_a.shape), _const_spec(w_b.shape), _const_spec(w_l.shape),
            _const_spec((3 * LANES, 2 * QK_W)), _const_spec((1, 2 * QK_W)),
            _const_spec((3, 3 * D_HY)), _const_spec((1, 3 * D_HY))],
        out_specs=[tile(1536), tile(1536), tile(3072), tile(2 * QK_W),
                   pl.BlockSpec((1, TT, D_HY), lambda i, t: (i, jnp.maximum(t - 1, 0), 0)),
                   pl.BlockSpec((1, D_HY // FFT_CB, TT // FFT_N2 * FFT_CB, FFT_N2),
                                lambda i, t: (i, 0, jnp.maximum(t - 2, 0), 0)),
                   pl.BlockSpec((1, D_HY, CTX_LEN), lambda i, t: (i, 0, 0))],
        out_shape=[bf(1536), bf(1536), bf(3072), jax.ShapeDtypeStruct((b, s, 2 * QK_W), F32),
                   bf(D_HY),
                   jax.ShapeDtypeStruct((b, D_HY // FFT_CB, (s - CTX_LEN) // FFT_N2 * FFT_CB, FFT_N2), F32),
                   jax.ShapeDtypeStruct((b, D_HY, CTX_LEN), F32)],
        scratch_shapes=[pltpu.VMEM((TT, d), BF16), pltpu.VMEM((TT, 3 * D_HY), F32),
                        pltpu.VMEM((8, 3 * D_HY), F32)],
        compiler_params=_params(2),
        name="in_proj",
    )(*x_args, modsel, modsel, g1, w_a, w_b, w_l, w2, ba, short_w, short_b)


@functools.lru_cache(maxsize=None)
def _scan_tables():
    i = np.arange(TT, dtype=np.float64)
    scale = DK ** -0.5
    diff = i[:, None] - i[None, :]
    same = (i[:, None] // GLA_CHUNK) == (i[None, :] // GLA_CHUNK)
    dmat = np.zeros((2, N_HEADS, TT, TT))
    erow = np.zeros((2, TT, V_W))
    kw = np.zeros((2, TT, QK_W))
    tri = np.zeros((2, TT, 2 * TT))
    for d in range(2):
        for h in range(N_HEADS):
            lg = RET_LOG_DECAY[d][h]
            if d == 0:
                dmat[d, h] = np.where(diff >= 0, np.exp(np.maximum(diff, 0) * lg), 0.0) * scale
                erow[d, :, h * DV:(h + 1) * DV] = np.exp((i + 1) * lg)[:, None]
                kw[d, :, h * DK:(h + 1) * DK] = (np.exp((TT - 1 - i) * lg) * scale)[:, None]
            else:
                dmat[d, h] = np.where(diff < 0, np.exp(np.maximum(-diff, 0) * lg), 0.0) * scale
                erow[d, :, h * DV:(h + 1) * DV] = np.exp((TT - i) * lg)[:, None]
                kw[d, :, h * DK:(h + 1) * DK] = (np.exp(i * lg) * scale)[:, None]
        block = same & ((diff >= 0) if d == 0 else (diff <= 0))
        tri[d] = np.concatenate([block, block], axis=1)
    f = lambda a: a.astype(np.float32)
    return f(dmat), f(erow), f(kw), f(tri)


def _scan_kernel(retf_ref, glaf_ref, lgf_ref, retb_ref, glab_ref, lgb_ref,
                 dmat_ref, erow_ref, kw_ref, tri_ref, of_ref, ob_ref, h_ref, hbd_ref, og_ref):
    @pl.when(pl.program_id(1) == 0)
    def _():
        h_ref[...] = jnp.zeros_like(h_ref)
        hbd_ref[...] = jnp.zeros_like(hbd_ref)

    c = GLA_CHUNK
    n_chunks = TT // c
    row = lax.broadcasted_iota(jnp.int32, (TT, TT), 0)
    col = lax.broadcasted_iota(jnp.int32, (TT, TT), 1)
    same_chunk = (row // c) == (col // c)
    heads = [(slice(hd * DK, (hd + 1) * DK), slice(hd * DV, (hd + 1) * DV)) for hd in range(N_HEADS)]
    def retention(d, ret_ref, o_ref):
        q = ret_ref[0, :, 0:QK_W]
        k = ret_ref[0, :, QK_W:2 * QK_W]
        v = ret_ref[0, :, 2 * QK_W:]
        o_inter = _dot(q, hbd_ref[2 * d]) * erow_ref[d]
        s_new = _dot_tn((k.astype(F32) * kw_ref[d]).astype(BF16), v)
        for hd, (ks, vs) in enumerate(heads):
            s = (_dot_nt(q[:, ks], k[:, ks]) * dmat_ref[d, hd]).astype(BF16)
            o_ref[0, :, vs] = (_dot(s, v[:, vs]) + o_inter[:, vs]).astype(BF16)
            h_new = math.exp(TT * RET_LOG_DECAY[d][hd]) * h_ref[d * 8 + hd] + s_new[ks, vs]
            h_ref[d * 8 + hd] = h_new
            hbd_ref[2 * d, ks, vs] = h_new.astype(BF16)
            yield

    def gla(d, gla_ref, lg_ref, o_ref):
        hi, lo = _split_bf16(lg_ref[0])
        cum = _dot(tri_ref[d], jnp.concatenate([hi, lo], axis=0))
        edge = c - 1 if d == 0 else 0
        tot = jnp.concatenate([jnp.broadcast_to(cum[j * c + edge:j * c + edge + 1], (c, QK_W))
                               for j in range(n_chunks)], axis=0)
        a_tot = jnp.exp(tot)
        k_neg = gla_ref[0, :, QK_W:2 * QK_W].astype(F32) * jnp.exp(-cum)
        q_in = (gla_ref[0, :, 0:QK_W].astype(F32) * jnp.exp(cum) * DK ** -0.5).astype(BF16)
        k_out = (k_neg * a_tot).astype(BF16)
        k_neg = k_neg.astype(BF16)
        v = gla_ref[0, :, 2 * QK_W:]
        mask = jnp.logical_and(same_chunk, row >= col if d == 0 else row < col)
        yield
        for hd, (ks, vs) in enumerate(heads):
            s = jnp.where(mask, _dot_nt(q_in[:, ks], k_neg[:, ks]), 0.0).astype(BF16)
            og_ref[d, :, vs] = _dot(s, v[:, vs])
        for j in (range(n_chunks) if d == 0 else range(n_chunks - 1, -1, -1)):
            yield
            rows = slice(j * c, (j + 1) * c)
            inter = _dot(q_in[rows], hbd_ref[2 * d + 1])
            o_ref[0, rows, V_W:] = (og_ref[d, rows, :] + inter).astype(BF16)
            s_j = _dot_tn(k_out[rows], v[rows])
            a_col = jnp.broadcast_to(a_tot[j * c:j * c + 1], (DV, QK_W)).T
            for hd, (ks, vs) in enumerate(heads):
                h_new = a_col[ks] * h_ref[d * 8 + 4 + hd] + s_j[ks, vs]
                h_ref[d * 8 + 4 + hd] = h_new
                hbd_ref[2 * d + 1, ks, vs] = h_new.astype(BF16)

    pending = [gla(0, glaf_ref, lgf_ref, of_ref), gla(1, glab_ref, lgb_ref, ob_ref),
               retention(0, retf_ref, of_ref), retention(1, retb_ref, ob_ref)]
    while pending:
        pending = [g for g in pending if next(g, True) is None]


def _scan_call(ret, gla, lga):
    b, s, _ = ret.shape
    nt = s // TT
    dmat, erow, kw, tri = (jnp.asarray(z) for z in _scan_tables())
    tri = tri.astype(BF16)
    fwd = lambda i, t: (i, t, 0)
    bwd = lambda i, t: (i, jnp.where(t == 0, 0, nt - t), 0)
    bwd_g = lambda i, t: (i, jnp.where(t == 0, 0, nt - t), 1)
    qkv = lambda f: pl.BlockSpec((1, TT, 2 * QK_W + V_W), f)
    return pl.pallas_call(
        _scan_kernel,
        grid=(b, nt),
        in_specs=[qkv(fwd), qkv(fwd), pl.BlockSpec((1, TT, QK_W), fwd),
                  qkv(bwd), qkv(bwd), pl.BlockSpec((1, TT, QK_W), bwd_g),
                  _const_spec(dmat.shape), _const_spec(erow.shape), _const_spec(kw.shape),
                  _const_spec(tri.shape)],
        out_specs=[pl.BlockSpec((1, TT, 2 * V_W), fwd), pl.BlockSpec((1, TT, 2 * V_W), bwd)],
        out_shape=[jax.ShapeDtypeStruct((b, s, 2 * V_W), BF16)] * 2,
        scratch_shapes=[pltpu.VMEM((16, DK, DV), F32), pltpu.VMEM((4, QK_W, V_W), BF16),
                        pltpu.VMEM((2, TT, V_W), F32)],
        compiler_params=_params(2),
        name="bidir_scan",
    )(ret, gla, lga, ret, gla, lga, dmat, erow, kw, tri)


def _hyena_gates(t, nt, cur_ref, prev_ref, next_row, w_ref, b_ref, x0_ref):
    has_prev = t >= 2
    has_next = jnp.logical_and(t >= 1, t <= nt - 2)
    ridx = lax.broadcasted_iota(jnp.int32, (TT, LANES), 0)
    u = []
    for part in range(3):
        cols = []
        for j in range(D_HY // LANES):
            cs = slice(part * D_HY + j * LANES, part * D_HY + (j + 1) * LANES)
            p = cur_ref[:, cs]
            pv = jnp.where(has_prev, prev_ref[7:8, cs], 0.0)
            nx = jnp.where(has_next, next_row[:, cs], 0.0)
            up = jnp.where(ridx == 0, pv, pltpu.roll(p, 1, 0))
            dn = jnp.where(ridx == TT - 1, nx, pltpu.roll(p, TT - 1, 0))
            cols.append(b_ref[:, cs] + up * w_ref[0:1, cs] + p * w_ref[1:2, cs] + dn * w_ref[2:3, cs])
        u.append(cols)
    x0_ref[0] = jnp.concatenate(u[0], axis=1).astype(BF16)
    z = jnp.concatenate([a * c for a, c in zip(u[1], u[2])], axis=1)
    return z.T


def _store_z(t, zt, zl_ref, zc_ref):
    @pl.when(t == 0)
    def _():
        zc_ref[0] = zt

    @pl.when(t > 0)
    def _():
        for r in range(TT // FFT_N2):
            zl_ref[0, :, r * FFT_CB:(r + 1) * FFT_CB, :] = _to_conv_rows(zt[:, r * FFT_N2:(r + 1) * FFT_N2])


@functools.lru_cache(maxsize=None)
def _filter_positions(length):
    n = np.arange(2 * length)
    m = np.where(n < length, n, 2 * length - n).astype(np.float64)
    m = np.where(n == length, 0.0, m)
    t = m / (length - 1)
    bands = np.linspace(1e-4, HY_BANDS - 1.0, HY_BANDS)
    ang = (2.0 * math.pi / length) * m[None, :] * bands[:, None]
    z = np.zeros((HY_EMB_PAD, 2 * length))
    z[0] = t
    z[1:1 + HY_BANDS] = np.cos(ang)
    z[1 + HY_BANDS:HY_EMB] = -np.sin(ang)
    return z.astype(np.float32)


def _filter_kernel(z_ref, w1_ref, b1_ref, w2_ref, b2_ref, w3_ref, fr_ref, dl_ref, k_ref, *, length, pb, split):
    z = z_ref[...]
    fr = fr_ref[...]
    hdn = jnp.sin(fr * (_dot(w1_ref[...], z, HIGHEST) + b1_ref[...]))
    for i in range(HY_INNER):
        hdn = jnp.sin(fr * (_dot(w2_ref[i], hdn, HIGHEST) + b2_ref[i]))
    h = _dot(w3_ref[0], hdn, HIGHEST)
    window = jnp.exp(-dl_ref[...] * z[0:1]) + HY_SHIFT
    pos = pl.program_id(0) * pb + lax.broadcasted_iota(jnp.int32, (1, pb), 1)
    k = jnp.where(pos == length, 0.0, h * window)
    if split:
        for r in range(pb // FFT_N2):
            k_ref[:, r * FFT_CB:(r + 1) * FFT_CB, :] = _to_conv_rows(k[:, r * FFT_N2:(r + 1) * FFT_N2])
    else:
        k_ref[...] = k


def _filter_call(length, w1, b1, w2, b2, w3, freq, split):
    pb = min(2048, length)
    if split:
        out_spec = pl.BlockSpec((D_HY // FFT_CB, pb // FFT_N2 * FFT_CB, FFT_N2), lambda j: (0, j, 0))
        out_shape = jax.ShapeDtypeStruct((D_HY // FFT_CB, 2 * length // FFT_N2 * FFT_CB, FFT_N2), F32)
    else:
        out_spec = pl.BlockSpec((D_HY, pb), lambda j: (0, j))
        out_shape = jax.ShapeDtypeStruct((D_HY, 2 * length), F32)
    zf = jnp.asarray(_filter_positions(length))
    w1t = jnp.pad(w1.T, ((0, 0), (0, HY_EMB_PAD - HY_EMB)))
    w2t = jnp.swapaxes(w2, 1, 2)
    w3t = w3.T.reshape(2, D_HY, HY_FILTER_WIDTH)
    col = lambda a: a.reshape(a.shape + (1,))
    deltas = np.abs(np.linspace(HY_MIN_DECAY, HY_MAX_DECAY, D_HY)).astype(np.float32)
    fw = HY_FILTER_WIDTH
    return pl.pallas_call(
        functools.partial(_filter_kernel, length=length, pb=pb, split=split),
        grid=(2 * length // pb,),
        in_specs=[pl.BlockSpec((HY_EMB_PAD, pb), lambda j: (0, j)),
                  _const_spec((fw, HY_EMB_PAD)), _const_spec((fw, 1)),
                  _const_spec((HY_INNER, fw, fw)), _const_spec((HY_INNER, fw, 1)),
                  pl.BlockSpec((1, D_HY, fw), lambda j: (jnp.where(j >= length // pb, 1, 0), 0, 0)),
                  _const_spec((fw, 1)), _const_spec((D_HY, 1))],
        out_specs=out_spec,
        out_shape=out_shape,
        compiler_params=_params(1),
        name="hyena_filter",
    )(zf, w1t, col(b1), w2t, col(b2), w3t, col(freq), jnp.asarray(deltas).reshape(D_HY, 1))


def _real_block(w):
    return np.block([[w.real, w.imag], [-w.imag, w.real]])


@functools.lru_cache(maxsize=None)
def _fft_tables(length):
    n = 2 * length
    n2 = FFT_N2
    n1 = n // n2
    f1 = np.arange(n1)
    wa = np.exp(-2j * np.pi * np.outer(f1, np.arange(n1)) / n1)
    stage_a = np.concatenate([wa.real, wa.imag], axis=0)
    tw = np.exp(-2j * np.pi * np.outer(f1, np.arange(n2)) / n)
    wb = np.exp(-2j * np.pi * np.outer(np.arange(n2), np.arange(n2)) / n2)
    wh = wa[:, :n1 // 2]
    a_pair = np.block([[wh.real, -wh.imag], [wh.imag, wh.real]])
    wi = np.conj(wa[:n1 // 2]) / n
    a_inv = np.block([[wi.real, -wi.imag], [wi.imag, wi.real]])
    f = lambda a: a.astype(np.float32)
    return dict(a_pair=f(a_pair), a_k=f(stage_a), tw_re=f(tw.real), tw_im=f(tw.imag),
                b_fwd=f(_real_block(wb)), b_inv=f(_real_block(np.conj(wb))), a_inv=f(a_inv))


def _fft_conv_kernel(bias_ref, z_ref, k_ref, ap_ref, ak_ref, twr_ref, twi_ref, bf_ref, bi_ref, ai_ref,
                     y_ref, sz_ref, sk_ref, sv_ref):
    n1, n2 = twr_ref.shape
    h, cb = n1 // 2, FFT_CB
    z0, z1 = (z_ref.at[b, 0] for b in range(2))
    y0, y1 = (y_ref.at[b, 0] for b in range(2))
    kk = k_ref.at[0]
    twr = twr_ref[...]
    twi = twi_ref[...]

    def twiddled(a):
        ar, ai = a[:n1], a[n1:]
        return jnp.concatenate([ar * twr - ai * twi, ar * twi + ai * twr], axis=1).astype(BF16)

    def stage_a(p):
        pair = (2 * p, 2 * p + 1)
        zc = jnp.concatenate([jnp.concatenate([z0[pl.ds(c, h, stride=cb), :], z1[pl.ds(c, h, stride=cb), :]], axis=0)
                              for c in pair], axis=1)
        kc = jnp.concatenate([kk[pl.ds(c, n1, stride=cb), :] for c in pair], axis=1)
        az = _dot(ap_ref[...], zc.astype(BF16))
        ak = _dot(ak_ref[...], kc.astype(BF16))
        for i, c in enumerate(pair):
            rows = slice(c * n1, (c + 1) * n1)
            sz_ref[rows, :] = twiddled(az[:, i * n2:(i + 1) * n2])
            sk_ref[rows, :] = twiddled(ak[:, i * n2:(i + 1) * n2])

    gb = 4

    def stage_b(g):
        rows = slice(g * gb * n1, (g + 1) * gb * n1)
        xs = _dot(sz_ref[rows, :], bf_ref[...])
        ks = _dot(sk_ref[rows, :], bf_ref[...])
        xr, xi, kr, ki = xs[:, :n2], xs[:, n2:], ks[:, :n2], ks[:, n2:]
        ys = jnp.concatenate([xr * kr - xi * ki, xr * ki + xi * kr], axis=1).astype(BF16)
        u = _dot(ys, bi_ref[...])
        for i in range(gb):
            ur, ui = u[i * n1:(i + 1) * n1, :n2], u[i * n1:(i + 1) * n1, n2:]
            lanes = slice((i % 2) * n2, (i % 2 + 1) * n2)
            sv_ref[g * (gb // 2) + i // 2, :n1, lanes] = (ur * twr + ui * twi).astype(BF16)
            sv_ref[g * (gb // 2) + i // 2, n1:, lanes] = (ui * twr - ur * twi).astype(BF16)

    def stage_c(p):
        y = _dot(ai_ref[...], sv_ref[p])
        for i in range(2):
            c = 2 * p + i
            bias = bias_ref[pl.program_id(0) * cb + c]
            sel = pl.ds(c, h, stride=cb)
            y0[sel, :] = y[:h, i * n2:(i + 1) * n2] + z0[sel, :] * bias
            y1[sel, :] = y[h:, i * n2:(i + 1) * n2] + z1[sel, :] * bias

    n_groups = cb // gb
    for step in range(n_groups + 2):
        if step < n_groups:
            for p in range(step * gb // 2, (step + 1) * gb // 2):
                stage_a(p)
        if 0 <= step - 1 < n_groups:
            stage_b(step - 1)
        if 0 <= step - 2 < n_groups:
            for p in range((step - 2) * gb // 2, (step - 1) * gb // 2):
                stage_c(p)


def _to_conv_rows(a):
    return a.reshape(a.shape[0] // FFT_CB, FFT_CB, a.shape[1])


def _fft_conv_call(zq, kq, bias):
    cb = FFT_CB
    b, nblk, rows, n2 = zq.shape
    assert b == 2 and n2 == FFT_N2
    h, ch = rows // cb, nblk * cb
    n1 = 2 * h
    tb = _fft_tables(h * n2)
    bf = lambda name: jnp.asarray(tb[name]).astype(BF16)
    consts = [bf("a_pair"), bf("a_k"), jnp.asarray(tb["tw_re"]), jnp.asarray(tb["tw_im"]),
              bf("b_fwd"), bf("b_inv"), bf("a_inv")]
    return pl.pallas_call(
        _fft_conv_kernel,
        grid=(nblk,),
        in_specs=[pl.BlockSpec(memory_space=pltpu.SMEM),
                  pl.BlockSpec((b, 1, h * cb, n2), lambda j: (0, j, 0, 0)),
                  pl.BlockSpec((1, n1 * cb, n2), lambda j: (j, 0, 0))]
                 + [_const_spec(a.shape) for a in consts],
        out_specs=pl.BlockSpec((b, 1, h * cb, n2), lambda j: (0, j, 0, 0)),
        out_shape=jax.ShapeDtypeStruct(zq.shape, F32),
        scratch_shapes=[pltpu.VMEM((cb * n1, 2 * n2), BF16), pltpu.VMEM((cb * n1, 2 * n2), BF16),
                        pltpu.VMEM((cb // 2, 2 * n1, 2 * n2), BF16)],
        compiler_params=_params(1),
        name="hyena_long_conv",
    )(bias, zq, kq, *consts)


@functools.lru_cache(maxsize=None)
def _dft_tables(length):
    n = 2 * length
    w = np.exp(-2j * np.pi * np.outer(np.arange(n), np.arange(n)) / n)
    fwd = np.concatenate([w.real, w.imag], axis=1)
    inv = np.concatenate([w.real[:, :length], w.imag[:, :length]], axis=0) / n
    return fwd[:length].astype(np.float32), fwd.astype(np.float32), inv.astype(np.float32)


def _dft_conv_kernel(z_ref, k_ref, bias_ref, fz_ref, fk_ref, fi_ref, y_ref):
    n = k_ref.shape[1]
    ks = _dot(k_ref[...], fk_ref[...], HIGHEST)
    kr, ki = ks[:, :n], ks[:, n:]
    for b in range(z_ref.shape[0]):
        z = z_ref[b]
        xs = _dot(z, fz_ref[...], HIGHEST)
        xr, xi = xs[:, :n], xs[:, n:]
        ys = jnp.concatenate([xr * kr - xi * ki, xr * ki + xi * kr], axis=1)
        y_ref[b] = _dot(ys, fi_ref[...], HIGHEST) + z * bias_ref[...]


def _dft_conv_call(zt, kt, bias):
    b, ch, length = zt.shape
    fz, fk, fi = (jnp.asarray(a) for a in _dft_tables(length))
    return pl.pallas_call(
        _dft_conv_kernel,
        out_shape=jax.ShapeDtypeStruct((b, ch, length), F32),
        compiler_params=pltpu.CompilerParams(vmem_limit_bytes=VMEM_LIMIT),
        name="hyena_ctx_conv",
    )(zt, kt, bias.reshape(ch, 1), fz, fk, fi)


def _head_norm(o):
    mu = jnp.mean(o, axis=-1, keepdims=True)
    var = jnp.mean(jnp.square(o - mu), axis=-1, keepdims=True)
    return (o - mu) * lax.rsqrt(var + HEAD_NORM_EPS)


def _merge_mix(of_ref, ob_ref, rg_ref, gr_ref, x0_ref, yl_ref, yc_ref, mg_ref, wb_ref, wo_ref, is_ctx):
    o = of_ref[0].astype(F32) + ob_ref[0].astype(F32)
    mixed = None
    for m, gate_ref in enumerate((rg_ref, gr_ref)):
        heads = [_head_norm(o[:, m * V_W + hd * DV:m * V_W + (hd + 1) * DV]) for hd in range(N_HEADS)]
        gh = gate_ref[0].astype(F32)
        br = jnp.concatenate(heads, axis=1) * (gh * (1.0 + jnp.tanh(gh)))
        g2 = 1.0 + jnp.tanh(mg_ref[0, :, m * D_MODEL:(m + 1) * D_MODEL].astype(F32))
        term = g2 * _dot(br.astype(BF16), wb_ref[m])
        mixed = term if mixed is None else mixed + term
    yt = jnp.concatenate([yl_ref[0, :, r * FFT_CB:(r + 1) * FFT_CB, :].reshape(D_HY, FFT_N2)
                          for r in range(TT // FFT_N2)], axis=1)
    if yc_ref is not None:
        yt = jnp.where(is_ctx, yc_ref[0], yt)
    hy = x0_ref[0].astype(F32) * yt.T
    g2 = 1.0 + jnp.tanh(mg_ref[0, :, 2 * D_MODEL:3 * D_MODEL].astype(F32))
    mixed = mixed + g2 * _dot(hy.astype(BF16), wb_ref[2])
    return _dot(mixed.astype(BF16), wo_ref[...])


def _gelu_x2(x):
    return x * (1.0 + lax.erf(x * (2.0 ** -0.5)))


def _mix_ffn_kernel(*refs, has_ctx, split, nt, final):
    refs = list(refs)
    mixers = refs[:6]
    del refs[:6]
    yc_ref = refs.pop(0) if has_ctx else None
    mg_ref = refs.pop(0)
    ctx_ref = refs.pop(0) if split else None
    (x_ref, modu_ref, modr_ref, wb_ref, wo_ref, g_ref, wu_ref, cw_ref, cb_ref, wd_ref, fg_ref,
     out_ref, a_ref, v_ref, tail_ref, acc_ref, xk_ref) = refs
    s = pl.program_id(1)
    p = lax.rem(s, 2)
    gw = GRID_W

    @pl.when(s == 0)
    def _():
        a_ref[...] = jnp.zeros_like(a_ref)
        v_ref[...] = jnp.zeros_like(v_ref)
        xk_ref[...] = jnp.zeros_like(xk_ref)

    tail_ref[...] = a_ref[p, TT - gw:, :]
    mu = modu_ref[0, 0]
    x = x_ref[0]
    if split:
        x = jnp.where(s == 0, ctx_ref[0], x)
    x1 = x + mu[2:3] * _merge_mix(*mixers, yc_ref, mg_ref, wb_ref, wo_ref, s == 0)
    xk_ref[p] = x1
    hb = _modulate(x1, g_ref[...], mu[3:4], mu[4:5]).astype(BF16)
    a_ref[p] = _dot(hb, wu_ref[:, :D_FF]).astype(BF16)
    v_ref[p] = _dot(hb, wu_ref[:, D_FF:]).astype(BF16)

    t = s - 1
    first_lat = 1 if has_ctx else 0
    is_ctx = t < first_lat
    up_ok = t >= first_lat + 1
    dn_ok = jnp.logical_and(t >= first_lat, t <= nt - 2)
    row_w = jnp.where(is_ctx, 0.0, 1.0)
    sub = lax.broadcasted_iota(jnp.int32, (8, LANES), 0)
    n_rows = TT // gw
    first_col = [jnp.logical_and(sub == 0, jnp.logical_or(k == 0, jnp.logical_not(is_ctx))) for k in range(n_rows)]
    last_col = [jnp.logical_and(sub == 7, jnp.logical_or(k == n_rows - 1, jnp.logical_not(is_ctx)))
                for k in range(n_rows)]

    def from_left(x):
        r = pltpu.roll(x, 1, 0)
        parts = []
        for k in range(n_rows):
            parts += [jnp.where(first_col[k], 0.0, r[k * gw:k * gw + 8]), r[k * gw + 8:(k + 1) * gw]]
        return jnp.concatenate(parts, axis=0)

    def from_right(x):
        r = pltpu.roll(x, TT - 1, 0)
        parts = []
        for k in range(n_rows):
            parts += [r[k * gw:(k + 1) * gw - 8], jnp.where(last_col[k], 0.0, r[(k + 1) * gw - 8:(k + 1) * gw])]
        return jnp.concatenate(parts, axis=0)

    def conv_chunk(cs):
        above = jnp.where(up_ok, tail_ref[:, cs].astype(F32), 0.0)
        below = jnp.where(dn_ok, a_ref[p, :gw, cs].astype(F32), 0.0)
        mid = a_ref[1 - p, :, cs].astype(F32)
        rows = (jnp.concatenate([above, mid[:TT - gw]], axis=0), mid, jnp.concatenate([mid[gw:], below], axis=0))
        w = [cw_ref[k:k + 1, cs] * (1.0 if k // 3 == 1 else row_w) for k in range(9)]
        taps = [rows[0] * w[dj] + rows[1] * w[3 + dj] + rows[2] * w[6 + dj] for dj in range(3)]
        acc = cb_ref[:, cs] + taps[1] + from_left(taps[0]) + from_right(taps[2])
        return _gelu_x2(acc).astype(BF16) * v_ref[1 - p, :, cs]

    kb = 2 * LANES
    for j in range(D_FF // kb):
        act = jnp.concatenate([conv_chunk(slice(j * kb + i * LANES, j * kb + (i + 1) * LANES)) for i in range(2)],
                              axis=1)
        part = _dot(act, wd_ref[j * kb:(j + 1) * kb, :])
        if j == 0:
            acc_ref[...] = part
        else:
            acc_ref[...] += part
    y = xk_ref[1 - p] + modr_ref[0, 0, 5:6] * acc_ref[...]
    if final:
        y = y * lax.rsqrt(jnp.mean(y * y, axis=-1, keepdims=True) + NORM_EPS) * fg_ref[...]
    out_ref[0] = y


def _mix_ffn_call(o_f, o_b, ret, gla, x0, y_lat, y_ctx, mg, seq_in, modsel, w_branch, w_out,
                  g2, w_up, conv_w, conv_b, w_down, final_g, final):
    split = isinstance(seq_in, tuple)
    has_ctx = y_ctx is not None
    assert has_ctx or not split
    b, s, _ = o_f.shape
    d = D_MODEL
    off = 0 if has_ctx else 1
    n = s // TT - off
    cur = lambda t: jnp.minimum(t, n - 1) + off
    tile = lambda w, j=0: pl.BlockSpec((1, TT, w), lambda i, t: (i, cur(t), j))
    mod_spec = lambda f: pl.BlockSpec((1, 1, 6, d), lambda i, t: (i, jnp.minimum(f(t), 1), 0, 0))
    in_specs = [tile(2 * V_W), tile(2 * V_W), tile(V_W, 2), tile(V_W, 2), tile(D_HY),
                pl.BlockSpec((1, D_HY // FFT_CB, TT // FFT_N2 * FFT_CB, FFT_N2),
                             lambda i, t: (i, 0, jnp.maximum(cur(t) - 1, 0), 0))]
    args = [o_f, o_b, ret, gla, x0, y_lat]
    if has_ctx:
        in_specs.append(pl.BlockSpec((1, D_HY, CTX_LEN), lambda i, t: (i, 0, 0)))
        args.append(y_ctx)
    in_specs.append(tile(3 * D_MODEL))
    args.append(mg)
    if split:
        in_specs += [pl.BlockSpec((1, TT, d), lambda i, t: (i, 0, 0)),
                     pl.BlockSpec((1, TT, d), lambda i, t: (i, jnp.maximum(cur(t) - 1, 0), 0))]
        args += list(seq_in)
    else:
        in_specs.append(tile(d))
        args.append(seq_in)
    in_specs += [mod_spec(cur), mod_spec(lambda t: jnp.maximum(t - 1, 0) + off),
                 _const_spec((3, V_W, d)), _const_spec((d, d)),
                 _const_spec((1, d)), _const_spec((d, 2 * D_FF)),
                 _const_spec((9, D_FF)), _const_spec((1, D_FF)), _const_spec((D_FF, d)), _const_spec((1, d))]
    args += [modsel, modsel, w_branch, w_out, g2, w_up, conv_w, conv_b, w_down, final_g]
    return pl.pallas_call(
        functools.partial(_mix_ffn_kernel, has_ctx=has_ctx, split=split, nt=n, final=final),
        grid=(b, n + 1),
        in_specs=in_specs,
        out_specs=pl.BlockSpec((1, TT, d), lambda i, t: (i, jnp.maximum(t - 1, 0), 0)),
        out_shape=jax.ShapeDtypeStruct((b, n * TT, d), F32),
        scratch_shapes=[pltpu.VMEM((2, TT, D_FF), BF16), pltpu.VMEM((2, TT, D_FF), BF16),
                        pltpu.VMEM((GRID_W, D_FF), BF16), pltpu.VMEM((TT, d), F32),
                        pltpu.VMEM((2, TT, d), F32)],
        compiler_params=_params(2),
        name="mix_ffn",
    )(*args)


def kernel(x, c, ctx, c_ctx, ada_w, ada_b, norm1_g, w_in, gla_wa2, gla_ba, hy_short_w, hy_short_b,
           hy_w1, hy_b1, hy_w2, hy_b2, hy_w3, hy_freq, hy_bias, w_branch, w_out, norm2_g, w_up,
           ffn_conv_w, ffn_conv_b, w_down, final_g):
    bsz, seq, d = x.shape
    depth = ada_w.shape[0]
    assert d == D_MODEL and ctx.shape[1] == CTX_LEN == TT and seq % TT == 0
    assert (2 * seq) % (2 * FFT_N2) == 0 and bsz == 2

    cvec = jnp.zeros((8, d), F32).at[:bsz].set(c).at[bsz].set(c_ctx)
    mod_all = _ada_call(cvec, ada_w, ada_b).reshape(depth, 8, 6, d)

    stream = (ctx, x)
    for l in range(depth):
        last = l == depth - 1
        m = mod_all[l]
        modsel = jnp.stack([jnp.broadcast_to(m[bsz], (bsz, 6, d)), m[:bsz]], axis=1)
        w = w_in[l]
        half = jnp.asarray(_gate_halving())
        lo, hi = LR_OFF, LR_OFF + 2 * GLA_LOWRANK
        w_l, w2 = _gate_weights(w[:, lo:hi], gla_wa2[l])
        ret, gla, mg, lga, x0, z_lat, z_ctx = _inproj_call(
            stream, modsel, norm1_g[l].reshape(1, d), (w[:, :lo] * half[:lo]).astype(BF16),
            (w[:, hi:] * half[hi:]).astype(BF16), w_l, w2, gla_ba[l].reshape(1, 2 * QK_W),
            hy_short_w[l], hy_short_b[l].reshape(1, 3 * D_HY))
        o_f, o_b = _scan_call(ret, gla, lga)

        filt = (hy_w1[l], hy_b1[l], hy_w2[l], hy_b2[l], hy_w3[l], hy_freq[l])
        y_lat = _fft_conv_call(z_lat, _filter_call(seq, *filt, split=True), hy_bias[l])
        y_ctx = None if last else _dft_conv_call(z_ctx, _filter_call(CTX_LEN, *filt, split=False), hy_bias[l])

        stream = _mix_ffn_call(o_f, o_b, ret, gla, x0, y_lat, y_ctx, mg, stream, modsel,
                               _to_bf16(w_branch, l, 0.5), _to_bf16(w_out, l),
                               norm2_g[l].reshape(1, d), _to_bf16(w_up, l),
                               ffn_conv_w[l].reshape(9, D_FF), ffn_conv_b[l].reshape(1, D_FF),
                               _to_bf16(w_down, l, 0.5), final_g.reshape(1, d), final=last)
    return stream
```
